```python
import math
import jax
import jax.numpy as jnp
from jax import lax
import numpy as np

D_MODEL = 1024
BATCH = 8
SEQ = 2048
DEPTH = 4

N_MEM = 256
N_AB_LAYERS = (DEPTH + 1) // 2
N_C_LAYERS = DEPTH // 2

MIX_HALF = D_MODEL // 2
HG_DK = 128
HG_DV = 128
HG_HEADS = MIX_HALF // HG_DK
HG_WIDTH = HG_HEADS * HG_DK
HG_CHUNK = 64
HG_MASK_LOG = -1e30
HG_EXP_CLIP = 60.0

RW_N = 64
RW_HEADS = MIX_HALF // RW_N
RW_WIDTH = RW_HEADS * RW_N
RW_DECAY_LORA = 32
RW_A_LORA = 32
RW_GATE_LORA = 96
RW_GN_EPS = 64e-5

HG_COLS = 4 * HG_WIDTH
RW_COLS = 3 * RW_WIDTH + RW_DECAY_LORA + RW_A_LORA + RW_GATE_LORA
AB_IN = HG_COLS + RW_COLS

S5_GROUP = 16
S5_GROUPS = D_MODEL // S5_GROUP
S5_STATE = 64
S5_DT_MIN = 1e-3
S5_DT_MAX = 1e-1

XA_HEADS = 4
XA_HEAD_DIM = D_MODEL // XA_HEADS

D_FF = 256 * ((8 * D_MODEL // 3 + 255) // 256)
N_EXPERTS = 8
TOP_K = 2
NORM_EPS = 1e-6

kernel_name = 'hybrid_hgrn2_rwkv7_s5_moe_trunk'


def rmsnorm(x, g, eps=NORM_EPS):
    xf = x.astype(jnp.float32)
    y = xf * lax.rsqrt(jnp.mean(xf * xf, axis=-1, keepdims=True) + eps)
    return (y * g.astype(jnp.float32)).astype(x.dtype)


def token_shift(t):
    return jnp.pad(t, ((0, 0), (1, 0), (0, 0)))[:, :-1]


def swiglu(t, w13, w2):
    gate, up = jnp.split(t @ w13, 2, axis=-1)
    return (jax.nn.silu(gate) * up) @ w2


def hgrn2_lower_bounds(logits):
    p = jax.nn.softmax(logits.astype(jnp.float32), axis=0)
    return jnp.cumsum(p, axis=0) - p[0:1]


def hgrn2_chunkwise(q, log_f, k, i):
    bsz, seq, heads, dk = q.shape
    dv = i.shape[-1]
    n_chunks = seq // HG_CHUNK

    def to_chunks(t):
        return t.reshape(bsz, n_chunks, HG_CHUNK, heads, t.shape[-1]).transpose(1, 0, 3, 2, 4)

    qc, lc, kc, ic = to_chunks(q), to_chunks(log_f), to_chunks(k), to_chunks(i)
    cum = jnp.cumsum(lc, axis=3)
    causal = jnp.tril(jnp.ones((HG_CHUNK, HG_CHUNK), dtype=bool))[:, :, None]

    def step(state, inp):
        q_c, l_c, k_c, i_c = inp
        diff = l_c[:, :, :, None, :] - l_c[:, :, None, :, :]
        decay = jnp.exp(jnp.where(causal, diff, HG_MASK_LOG))
        attn = jnp.einsum('bhtd,bhsd,bhtsd->bhts', q_c, k_c, decay)
        o = jnp.matmul(attn, i_c) + jnp.einsum('bhtd,bhdv->bhtv', q_c * jnp.exp(l_c), state)
        l_end = l_c[:, :, -1:, :]
        k_dec = k_c * jnp.exp(l_end - l_c)
        state = state * jnp.exp(l_end[:, :, 0, :, None]) + jnp.einsum('bhsd,bhsv->bhdv', k_dec, i_c)
        return state, o

    state0 = jnp.zeros((bsz, heads, dk, dv), jnp.float32)
    _, o = lax.scan(step, state0, (qc, cum, kc, ic))
    return o.transpose(1, 0, 3, 2, 4).reshape(bsz, seq, heads, dv)


def hgrn2_mix(p, lb, onorm_g):
    bsz, seq, _ = p.shape
    q, f_raw, i, g = jnp.split(p, 4, axis=-1)
    lb = lb.reshape(HG_HEADS, HG_DK)
    q = jax.nn.silu(q.reshape(bsz, seq, HG_HEADS, HG_DK))
    f_raw = f_raw.reshape(bsz, seq, HG_HEADS, HG_DK)
    log_f = jax.nn.log_sigmoid(f_raw) + jnp.log1p(lb * jnp.exp(jnp.minimum(-f_raw, HG_EXP_CLIP)))
    k = (1.0 - lb) * jax.nn.sigmoid(-f_raw)
    o = hgrn2_chunkwise(q, log_f, k, i.reshape(bsz, seq, HG_HEADS, HG_DV))
    o = rmsnorm(o, onorm_g)
    return o.reshape(bsz, seq, HG_WIDTH) * jax.nn.silu(g)


def rwkv7_scan(r, decay, k, v, kk, a):
    bsz, _, heads, n = r.shape

    def step(state, inp):
        r_t, d_t, k_t, v_t, kk_t, a_t = inp
        sa = jnp.einsum('bhvk,bhk->bhv', state, -kk_t)
        state = (state * d_t[:, :, None, :]
                 + sa[..., None] * (kk_t * a_t)[:, :, None, :]
                 + v_t[..., None] * k_t[:, :, None, :])
        return state, jnp.einsum('bhvk,bhk->bhv', state, r_t)

    xs = tuple(jnp.moveaxis(t, 1, 0) for t in (r, decay, k, v, kk, a))
    state0 = jnp.zeros((bsz, heads, n, n), jnp.float32)
    _, out = lax.scan(step, state0, xs)
    return jnp.moveaxis(out, 0, 1)


def rwkv7_mix(p, mu, w0, w2, a0, a2, g2, k_k, k_a, r_k, gn_g, gn_b):
    bsz, seq, _ = p.shape
    p = p + (token_shift(p) - p) * mu
    cuts = [RW_WIDTH, 2 * RW_WIDTH, 3 * RW_WIDTH, 3 * RW_WIDTH + RW_DECAY_LORA,
            3 * RW_WIDTH + RW_DECAY_LORA + RW_A_LORA]
    r, k, v, wd, ad, gd = jnp.split(p, cuts, axis=-1)
    w = -jax.nn.softplus(-(w0 + jnp.tanh(wd) @ w2)) - 0.5
    decay = jnp.exp(-jnp.exp(w))
    a = jax.nn.sigmoid(a0 + ad @ a2)
    g = jax.nn.sigmoid(gd) @ g2

    def heads(t):
        return t.reshape(bsz, seq, RW_HEADS, RW_N)

    kk = heads(k * k_k)
    kk = kk * lax.rsqrt(jnp.maximum(jnp.sum(kk * kk, axis=-1, keepdims=True), 1e-24))
    k = k * (1.0 + (a - 1.0) * k_a)
    r, k, v, a, decay = heads(r), heads(k), heads(v), heads(a), heads(decay)
    o = rwkv7_scan(r, decay, k, v, kk, a)
    mean = jnp.mean(o, axis=-1, keepdims=True)
    var = jnp.mean(jnp.square(o - mean), axis=-1, keepdims=True)
    o = ((o - mean) * lax.rsqrt(var + RW_GN_EPS)).reshape(bsz, seq, RW_WIDTH) * gn_g + gn_b
    bonus = jnp.sum(r * k * r_k, axis=-1, keepdims=True) * v
    return (o + bonus.reshape(bsz, seq, RW_WIDTH)) * g


def ab_mixer(h, w_in, w_out, lb, hg_onorm_g, rw_mu, rw_w0, rw_w2, rw_a0, rw_a2, rw_g2,
             rw_k_k, rw_k_a, rw_r_k, rw_gn_g, rw_gn_b):
    p = (h @ w_in).astype(jnp.float32)
    y_a = hgrn2_mix(p[..., :HG_COLS], lb, hg_onorm_g)
    y_b = rwkv7_mix(p[..., HG_COLS:], rw_mu, rw_w0, rw_w2, rw_a0, rw_a2, rw_g2,
                    rw_k_k, rw_k_a, rw_r_k, rw_gn_g, rw_gn_b)
    y = jnp.concatenate([y_a, y_b], axis=-1)
    return y.astype(h.dtype) @ w_out


def _complex_affine_combine(e1, e2):
    a1r, a1i, b1r, b1i = e1
    a2r, a2i, b2r, b2i = e2
    return (a2r * a1r - a2i * a1i,
            a2r * a1i + a2i * a1r,
            a2r * b1r - a2i * b1i + b2r,
            a2r * b1i + a2i * b1r + b2i)


def s5_mixer(h, w_in, a_re, a_im, log_dt, b_re, b_im, c_re, c_im, d, w_glu):
    bsz, seq, _ = h.shape
    u = (h @ w_in).astype(jnp.float32)
    ug = u.reshape(bsz, seq, S5_GROUPS, S5_GROUP)
    lam_re = jnp.minimum(a_re.astype(jnp.float32), -1e-4)
    lam_im = a_im.astype(jnp.float32)
    dt = jnp.exp(log_dt.astype(jnp.float32))[:, None]
    mag = jnp.exp(lam_re * dt)
    ab_re = mag * jnp.cos(lam_im * dt)
    ab_im = mag * jnp.sin(lam_im * dt)
    den = lam_re * lam_re + lam_im * lam_im
    coef_re = ((ab_re - 1.0) * lam_re + ab_im * lam_im) / den
    coef_im = (ab_im * lam_re - (ab_re - 1.0) * lam_im) / den
    bb_re = coef_re[..., None] * b_re - coef_im[..., None] * b_im
    bb_im = coef_re[..., None] * b_im + coef_im[..., None] * b_re
    bu_re = jnp.einsum('bsgm,gpm->bsgp', ug, bb_re)
    bu_im = jnp.einsum('bsgm,gpm->bsgp', ug, bb_im)
    a_seq_re = jnp.broadcast_to(ab_re, (1, seq, S5_GROUPS, S5_STATE))
    a_seq_im = jnp.broadcast_to(ab_im, (1, seq, S5_GROUPS, S5_STATE))
    _, _, x_re, x_im = lax.associative_scan(
        _complex_affine_combine, (a_seq_re, a_seq_im, bu_re, bu_im), axis=1)
    y = jnp.einsum('bsgp,gmp->bsgm', x_re, c_re) - jnp.einsum('bsgp,gmp->bsgm', x_im, c_im)
    y = y.reshape(bsz, seq, D_MODEL) + d * u
    z = jax.nn.gelu(y).astype(h.dtype) @ w_glu
    val, gate = jnp.split(z, 2, axis=-1)
    return val * jax.nn.sigmoid(gate)


def cross_attention(h, mem_n, wq, wkv, wo):
    bsz, seq, _ = h.shape
    q = (h @ wq).reshape(bsz, seq, XA_HEADS, XA_HEAD_DIM)
    k, v = jnp.split(mem_n @ wkv, 2, axis=-1)
    k = k.reshape(bsz, -1, XA_HEADS, XA_HEAD_DIM)
    v = v.reshape(bsz, -1, XA_HEADS, XA_HEAD_DIM)
    s = jnp.einsum('bshd,bmhd->bhsm', q, k).astype(jnp.float32) * (XA_HEAD_DIM ** -0.5)
    p = jax.nn.softmax(s, axis=-1).astype(v.dtype)
    o = jnp.einsum('bhsm,bmhd->bshd', p, v).reshape(bsz, seq, D_MODEL)
    return o @ wo


def moe_swiglu(h, router, w13, w2):
    bsz, seq, dm = h.shape
    t = h.reshape(-1, dm)
    logits = (t @ router).astype(jnp.float32)
    top_v, top_i = lax.top_k(logits, TOP_K)
    top_w = jax.nn.softmax(top_v, axis=-1)
    gates = jnp.einsum('nk,nke->ne', top_w, jax.nn.one_hot(top_i, N_EXPERTS, dtype=jnp.float32))
    out = jnp.zeros(t.shape, jnp.float32)
    for e in range(N_EXPERTS):
        out = out + gates[:, e:e + 1] * swiglu(t, w13[e], w2[e]).astype(jnp.float32)
    return out.astype(h.dtype).reshape(bsz, seq, dm)


def setup_inputs(seed: int = 0) -> dict:
    key = jax.random.key(seed)
    keys = jax.random.split(key, 48)
    counter = [0]

    def nxt():
        k = keys[counter[0]]
        counter[0] += 1
        return k

    def nrm(shape, scale):
        return jax.random.normal(nxt(), shape, jnp.float32) * scale

    def gain(shape):
        return 1.0 + nrm(shape, 0.02)

    def unif(shape, lo, hi):
        return jax.random.uniform(nxt(), shape, jnp.float32, lo, hi)

    D = D_MODEL
    NAB = N_AB_LAYERS
    NC = N_C_LAYERS
    G, P, M = S5_GROUPS, S5_STATE, S5_GROUP
    n_idx = jnp.arange(P, dtype=jnp.float32)
    return {
        'x': nrm((BATCH, SEQ, D), 1.0),
        'mem': nrm((BATCH, N_MEM, D), 1.0),
        'mix_norm_g': gain((DEPTH, D)),
        'xattn_norm_g': gain((DEPTH, D)),
        'ffn_norm_g': gain((DEPTH, D)),
        'mem_norm_g': gain((D,)),
        'final_norm_g': gain((D,)),
        'ab_w_in': nrm((NAB, D, AB_IN), D ** -0.5),
        'ab_w_out': nrm((NAB, D, D), D ** -0.5),
        'hg_lb_logits': nrm((NAB, HG_WIDTH), 0.1),
        'hg_onorm_g': gain((NAB, HG_DV)),
        'rw_mu': unif((NAB, RW_COLS), 0.0, 1.0),
        'rw_w0': unif((NAB, RW_WIDTH), -6.0, -1.0),
        'rw_w2': nrm((NAB, RW_DECAY_LORA, RW_WIDTH), RW_DECAY_LORA ** -0.5),
        'rw_a0': nrm((NAB, RW_WIDTH), 0.1),
        'rw_a2': nrm((NAB, RW_A_LORA, RW_WIDTH), RW_A_LORA ** -0.5),
        'rw_g2': nrm((NAB, RW_GATE_LORA, RW_WIDTH), RW_GATE_LORA ** -0.5),
        'rw_k_k': 0.85 + nrm((NAB, RW_WIDTH), 0.02),
        'rw_k_a': 1.0 + nrm((NAB, RW_WIDTH), 0.02),
        'rw_r_k': -0.04 + nrm((NAB, RW_HEADS, RW_N), 0.1),
        'rw_gn_g': gain((NAB, RW_WIDTH)),
        'rw_gn_b': nrm((NAB, RW_WIDTH), 0.02),
        'c_w_in': nrm((NC, D, D), D ** -0.5),
        's5_a_re': -0.5 + nrm((NC, G, P), 0.01),
        's5_a_im': math.pi * n_idx + nrm((NC, G, P), 0.01),
        's5_log_dt': unif((NC, G), math.log(S5_DT_MIN), math.log(S5_DT_MAX)),
        's5_b_re': nrm((NC, G, P, M), (2 * M) ** -0.5),
        's5_b_im': nrm((NC, G, P, M), (2 * M) ** -0.5),
        's5_c_re': nrm((NC, G, M, P), 0.5 ** 0.5),
        's5_c_im': nrm((NC, G, M, P), 0.5 ** 0.5),
        's5_d': nrm((NC, D), 1.0),
        'c_w_glu': nrm((NC, D, 2 * D), D ** -0.5),
        'xa_wq': nrm((DEPTH, D, D), D ** -0.5),
        'xa_wkv': nrm((DEPTH, D, 2 * D), D ** -0.5),
        'xa_wo': nrm((DEPTH, D, D), D ** -0.5),
        'ffn_w13': nrm((NAB, D, 2 * D_FF), D ** -0.5),
        'ffn_w2': nrm((NAB, D_FF, D), D_FF ** -0.5),
        'moe_router': nrm((NC, D, N_EXPERTS), D ** -0.5),
        'moe_w13': nrm((NC, N_EXPERTS, D, 2 * D_FF), D ** -0.5),
        'moe_w2': nrm((NC, N_EXPERTS, D_FF, D), D_FF ** -0.5),
    }


def reference(x, mem, mix_norm_g, xattn_norm_g, ffn_norm_g, mem_norm_g, final_norm_g,
              ab_w_in, ab_w_out, hg_lb_logits, hg_onorm_g, rw_mu, rw_w0, rw_w2, rw_a0,
              rw_a2, rw_g2, rw_k_k, rw_k_a, rw_r_k, rw_gn_g, rw_gn_b, c_w_in, s5_a_re,
              s5_a_im, s5_log_dt, s5_b_re, s5_b_im, s5_c_re, s5_c_im, s5_d, c_w_glu,
              xa_wq, xa_wkv, xa_wo, ffn_w13, ffn_w2, moe_router, moe_w13, moe_w2):
    mem_n = rmsnorm(mem, mem_norm_g)
    lb_table = hgrn2_lower_bounds(hg_lb_logits)
    for layer in range(DEPTH):
        j = layer // 2
        h = rmsnorm(x, mix_norm_g[layer])
        if layer % 2 == 0:
            x = x + ab_mixer(h, ab_w_in[j], ab_w_out[j], lb_table[j], hg_onorm_g[j],
                             rw_mu[j], rw_w0[j], rw_w2[j], rw_a0[j], rw_a2[j], rw_g2[j],
                             rw_k_k[j], rw_k_a[j], rw_r_k[j], rw_gn_g[j], rw_gn_b[j])
        else:
            x = x + s5_mixer(h, c_w_in[j], s5_a_re[j], s5_a_im[j], s5_log_dt[j], s5_b_re[j],
                             s5_b_im[j], s5_c_re[j], s5_c_im[j], s5_d[j], c_w_glu[j])
        x = x + cross_attention(rmsnorm(x, xattn_norm_g[layer]), mem_n,
                                xa_wq[layer], xa_wkv[layer], xa_wo[layer])
        h = rmsnorm(x, ffn_norm_g[layer])
        if layer % 2 == 0:
            x = x + swiglu(h, ffn_w13[j], ffn_w2[j])
        else:
            x = x + moe_swiglu(h, moe_router[j], moe_w13[j], moe_w2[j])
    return rmsnorm(x, final_norm_g)
```

```python
import functools
import math

import jax
import jax.numpy as jnp
from jax import lax
from jax.experimental import pallas as pl
from jax.experimental.pallas import tpu as pltpu

F32 = jnp.float32
BF16 = jnp.bfloat16

NORM_EPS = 1e-6
LANES = 128
SUBLANES = 8
VMEM_LIMIT_BYTES = 56 * 1024 * 1024

HG_DK = 128
HG_CHUNK = 64
HG_SUB = 16
HG_MASK_LOG = -1e30
HG_EXP_CLIP = 60.0

RW_N = 64
RW_CHUNK = 64
RW_DECAY_LORA = 32
RW_A_LORA = 32
RW_GATE_LORA = 96
RW_GN_EPS = 64e-5

S5_GROUP = 16
S5_STATE = 64
S5_GROUPS_PER_BLOCK = LANES // S5_GROUP
S5_BLOCK_STATE = S5_GROUPS_PER_BLOCK * S5_STATE

XA_HEADS = 4
N_EXPERTS = 8

HIGHEST = lax.Precision.HIGHEST


def _cparams(*sem):
    return pltpu.CompilerParams(dimension_semantics=sem, vmem_limit_bytes=VMEM_LIMIT_BYTES)


def _dot(a, b, precision=None):
    return jnp.dot(a, b, preferred_element_type=F32, precision=precision)


def _dot_nt(a, b, precision=None):
    return lax.dot_general(a, b, (((1,), (1,)), ((), ())), preferred_element_type=F32,
                           precision=precision)


def _dot_tn(a, b, precision=None):
    return lax.dot_general(a, b, (((0,), (0,)), ((), ())), preferred_element_type=F32,
                           precision=precision)


def _rms(x, g, eps=NORM_EPS):
    return x * lax.rsqrt(jnp.mean(x * x, axis=-1, keepdims=True) + eps) * g


def _silu(x):
    return x * jax.nn.sigmoid(x)


def _tri_incl(n):
    r = lax.broadcasted_iota(jnp.int32, (n, n), 0)
    c = lax.broadcasted_iota(jnp.int32, (n, n), 1)
    return jnp.where(r >= c, 1.0, 0.0).astype(F32)


def _norm_matmul_kernel(x_ref, g_ref, w_ref, o_ref, h_ref):
    @pl.when(pl.program_id(1) == 0)
    def _():
        h_ref[...] = _rms(x_ref[...], g_ref[...]).astype(BF16)

    o_ref[...] = _dot(h_ref[...], w_ref[...]).astype(o_ref.dtype)


def norm_matmul(x, g, w, *, tm, tn, out_dtype=F32, out_shape=None, out_map=None):
    m, k = x.shape
    n = w.shape[1]
    assert m % tm == 0 and n % tn == 0
    if out_shape is None:
        out_shape = (m, n)
    if out_map is None:
        out_map = lambda i, j: (i, j)
    return pl.pallas_call(
        _norm_matmul_kernel,
        grid=(m // tm, n // tn),
        in_specs=[pl.BlockSpec((tm, k), lambda i, j: (i, 0)),
                  pl.BlockSpec((1, k), lambda i, j: (0, 0)),
                  pl.BlockSpec((k, tn), lambda i, j: (0, j))],
        out_specs=pl.BlockSpec((tm, tn), out_map),
        out_shape=jax.ShapeDtypeStruct(out_shape, out_dtype),
        scratch_shapes=[pltpu.VMEM((tm, k), BF16)],
        compiler_params=_cparams("parallel", "arbitrary"),
        name="norm_matmul",
    )(x, g, w)


def _hgrn2_kernel(q_ref, f_ref, i_ref, g_ref, lbl_ref, og_ref, o_ref, st_ref, *, layer_j, n_chunks):
    c = HG_CHUNK
    nsub = c // HG_SUB

    @pl.when(pl.program_id(2) == 0)
    def _():
        st_ref[...] = jnp.zeros_like(st_ref)

    logits = lbl_ref[...]
    e = jnp.exp(logits - jnp.max(logits, axis=0, keepdims=True))
    p = e / jnp.sum(e, axis=0, keepdims=True)
    lb = jnp.zeros((1, p.shape[1]), F32)
    for r in range(1, layer_j + 1):
        lb = lb + p[r:r + 1, :]

    tri = _tri_incl(c)
    t_loc = lax.broadcasted_iota(jnp.int32, (nsub, HG_SUB, HG_DK), 1)
    og = og_ref[...]

    def chunk(ci, carry):
        r0 = pl.multiple_of(ci * c, c)
        qr = q_ref[pl.ds(r0, c), :]
        fr = f_ref[pl.ds(r0, c), :]
        iv = i_ref[pl.ds(r0, c), :]
        gr = g_ref[pl.ds(r0, c), :]
        q = _silu(qr)
        log_sig = jnp.minimum(fr, 0.0) - jnp.log1p(jnp.exp(-jnp.abs(fr)))
        log_f = log_sig + jnp.log1p(lb * jnp.exp(jnp.minimum(-fr, HG_EXP_CLIP)))
        k = (1.0 - lb) * jax.nn.sigmoid(-fr)
        l = _dot(tri, log_f, precision=HIGHEST)
        l_end = l[c - 1:c, :]
        st = st_ref[...]
        iv_b = iv.astype(BF16)

        o = _dot_nt((q * jnp.exp(l)).astype(BF16), st.astype(BF16))

        parts = [jnp.zeros((HG_SUB, HG_DK), F32)]
        for sb in range(1, nsub):
            lo = sb * HG_SUB
            rho = l[lo - 1:lo, :]
            qt = q[lo:lo + HG_SUB, :] * jnp.exp(l[lo:lo + HG_SUB, :] - rho)
            kt = k[:lo, :] * jnp.exp(rho - l[:lo, :])
            a = _dot_nt(qt.astype(BF16), kt.astype(BF16))
            parts.append(_dot(a.astype(BF16), iv_b[:lo, :]))
        o = o + jnp.concatenate(parts, axis=0)

        q3 = q.reshape(nsub, HG_SUB, HG_DK)
        k3 = k.reshape(nsub, HG_SUB, HG_DK)
        l3 = l.reshape(nsub, HG_SUB, HG_DK)
        i3 = iv.reshape(nsub, HG_SUB, HG_DK)
        od = jnp.zeros((nsub, HG_SUB, HG_DK), F32)
        for s in range(HG_SUB):
            diff = l3 - l3[:, s:s + 1, :]
            w = jnp.exp(jnp.where(t_loc >= s, diff, HG_MASK_LOG))
            col = jnp.sum(q3 * k3[:, s:s + 1, :] * w, axis=-1, keepdims=True)
            od = od + col * i3[:, s:s + 1, :]
        o = o + od.reshape(c, HG_DK)

        k_dec = k * jnp.exp(l_end - l)
        st_ref[...] = st * jnp.exp(l_end) + _dot_tn(iv_b, k_dec.astype(BF16))

        y = _rms(o, og) * _silu(gr)
        o_ref[pl.ds(r0, c), :] = y.astype(o_ref.dtype)
        return carry

    lax.fori_loop(0, n_chunks, chunk, 0)


def hgrn2_mix(p_hg, lb_logits, onorm_g, *, batch, seq, layer_j, tt=256):
    n, cols = p_hg.shape
    width = cols // 4
    heads = width // HG_DK
    assert seq % tt == 0 and tt % HG_CHUNK == 0
    tb = seq // tt
    n_ab = lb_logits.shape[0]

    def col_spec(part):
        return pl.BlockSpec((tt, HG_DK), lambda b, h, t, part=part: (b * tb + t, part * heads + h))

    kern = functools.partial(_hgrn2_kernel, layer_j=layer_j, n_chunks=tt // HG_CHUNK)
    return pl.pallas_call(
        kern,
        grid=(batch, heads, tb),
        in_specs=[col_spec(0), col_spec(1), col_spec(2), col_spec(3),
                  pl.BlockSpec((n_ab, HG_DK), lambda b, h, t: (0, h)),
                  pl.BlockSpec((1, HG_DK), lambda b, h, t: (0, 0))],
        out_specs=pl.BlockSpec((tt, HG_DK), lambda b, h, t: (b * tb + t, h)),
        out_shape=jax.ShapeDtypeStruct((n, width), BF16),
        scratch_shapes=[pltpu.VMEM((HG_DK, HG_DK), F32)],
        compiler_params=_cparams("parallel", "parallel", "arbitrary"),
        name="hgrn2_mix",
    )(p_hg, p_hg, p_hg, p_hg, lb_logits, onorm_g)


def _rwkv7_kernel(p_ref, mu_ref, w0_ref, w2_ref, a0_ref, a2_ref, g2_ref, kk_ref, ka_ref,
                  rk_ref, gng_ref, gnb_ref, o_ref, prev_ref, st_ref, *, heads):
    c = RW_CHUNK
    n = RW_N
    width = heads * n

    @pl.when(pl.program_id(1) == 0)
    def _():
        prev_ref[...] = jnp.zeros_like(prev_ref)
        st_ref[...] = jnp.zeros_like(st_ref)

    p = p_ref[...]
    row = lax.broadcasted_iota(jnp.int32, p.shape, 0)
    shifted = jnp.where(row == 0, prev_ref[0:1, :], pltpu.roll(p, 1, axis=0))
    prev_ref[0:1, :] = p[c - 1:c, :]
    p = p + (shifted - p) * mu_ref[...]

    r = p[:, 0:width]
    k = p[:, width:2 * width]
    v = p[:, 2 * width:3 * width]
    off = 3 * width
    wd = p[:, off:off + RW_DECAY_LORA]
    ad = p[:, off + RW_DECAY_LORA:off + RW_DECAY_LORA + RW_A_LORA]
    gd = p[:, off + RW_DECAY_LORA + RW_A_LORA:off + RW_DECAY_LORA + RW_A_LORA + RW_GATE_LORA]

    wz = w0_ref[...] + _dot(jnp.tanh(wd).astype(BF16), w2_ref[...])
    w = jnp.minimum(wz, 0.0) - jnp.log1p(jnp.exp(-jnp.abs(wz))) - 0.5
    lw = -jnp.exp(w)
    a = jax.nn.sigmoid(a0_ref[...] + _dot(ad.astype(BF16), a2_ref[...]))
    g = _dot(jax.nn.sigmoid(gd).astype(BF16), g2_ref[...])
    kk_raw = k * kk_ref[...]
    k2 = k * (1.0 + (a - 1.0) * ka_ref[...])

    cum = _dot(_tri_incl(c), lw, precision=HIGHEST)
    cum_prev = cum - lw
    cum_end = cum[c - 1:c, :]
    e_in = jnp.exp(cum)
    e_prev = jnp.exp(cum_prev)
    e_inv = jnp.exp(-cum)
    e_fin = jnp.exp(cum_end - cum)
    e_end = jnp.exp(cum_end)

    rr = lax.broadcasted_iota(jnp.int32, (c, c), 0)
    cc = lax.broadcasted_iota(jnp.int32, (c, c), 1)
    strict = rr > cc
    incl = rr >= cc
    eye = jnp.where(rr == cc, 1.0, 0.0).astype(F32)

    for h in range(heads):
        sl = slice(h * n, (h + 1) * n)
        r_h, k_h, v_h, a_h = r[:, sl], k2[:, sl], v[:, sl], a[:, sl]
        kk_h = kk_raw[:, sl]
        kk_h = kk_h * lax.rsqrt(jnp.maximum(jnp.sum(kk_h * kk_h, axis=-1, keepdims=True), 1e-24))
        b_h = kk_h * a_h
        at = -kk_h * e_prev[:, sl]
        rt = r_h * e_in[:, sl]
        bt = b_h * e_inv[:, sl]
        kt = k_h * e_inv[:, sl]
        bf = b_h * e_fin[:, sl]
        kf = k_h * e_fin[:, sl]
        st = st_ref[h]
        v_b = v_h.astype(BF16)

        ar = jnp.concatenate([at, rt], axis=0).astype(BF16)
        g_b = _dot_nt(ar, bt.astype(BF16))
        g_k = _dot_nt(ar, kt.astype(BF16))
        a_ab = jnp.where(strict, g_b[:c], 0.0)
        a_ak = jnp.where(strict, g_k[:c], 0.0)
        a_rb = jnp.where(incl, g_b[c:], 0.0)
        a_rk = jnp.where(incl, g_k[c:], 0.0)

        tinv = eye + a_ab
        x = a_ab
        for _ in range(int(math.log2(c)) - 1):
            x = _dot(x.astype(BF16), x.astype(BF16))
            tinv = tinv + _dot(tinv.astype(BF16), x.astype(BF16))

        arm = _dot_nt(ar, st.astype(BF16))
        akv = _dot(jnp.concatenate([a_ak, a_rk], axis=0).astype(BF16), v_b)
        u = _dot(tinv.astype(BF16), (arm[:c] + akv[:c]).astype(BF16))
        u_b = u.astype(BF16)
        o = arm[c:] + akv[c:] + _dot(a_rb.astype(BF16), u_b)

        uv = jnp.concatenate([u_b, v_b], axis=0)
        bk = jnp.concatenate([bf, kf], axis=0).astype(BF16)
        st_ref[h] = st * e_end[:, sl] + _dot_tn(uv, bk)

        mean = jnp.mean(o, axis=-1, keepdims=True)
        var = jnp.mean(jnp.square(o - mean), axis=-1, keepdims=True)
        on = (o - mean) * lax.rsqrt(var + RW_GN_EPS) * gng_ref[:, sl] + gnb_ref[:, sl]
        bonus = jnp.sum(r_h * k_h * rk_ref[:, sl], axis=-1, keepdims=True) * v_h
        o_ref[:, sl] = ((on + bonus) * g[:, sl]).astype(o_ref.dtype)


def rwkv7_mix(p_rw, mu, w0, w2, a0, a2, g2, k_k, k_a, r_k, gn_g, gn_b, *, batch, seq):
    n_tok, cols = p_rw.shape
    width = w0.shape[1]
    heads = width // RW_N
    assert seq % RW_CHUNK == 0
    cb = seq // RW_CHUNK
    full = lambda a: pl.BlockSpec(a.shape, lambda b, t: (0,) * a.ndim)
    params = (mu, w0, w2, a0, a2, g2, k_k, k_a, r_k, gn_g, gn_b)
    return pl.pallas_call(
        functools.partial(_rwkv7_kernel, heads=heads),
        grid=(batch, cb),
        in_specs=[pl.BlockSpec((RW_CHUNK, cols), lambda b, t: (b * cb + t, 0))]
                 + [full(a) for a in params],
        out_specs=pl.BlockSpec((RW_CHUNK, width), lambda b, t: (b * cb + t, 0)),
        out_shape=jax.ShapeDtypeStruct((n_tok, width), BF16),
        scratch_shapes=[pltpu.VMEM((SUBLANES, cols), F32),
                        pltpu.VMEM((heads, RW_N, RW_N), F32)],
        compiler_params=_cparams("parallel", "arbitrary"),
        name="rwkv7_mix",
    )(p_rw, *params)


def _out_proj_kernel(ya_ref, yb_ref, wa_ref, wb_ref, x_ref, o_ref):
    o_ref[...] = x_ref[...] + _dot(ya_ref[...], wa_ref[...]) + _dot(yb_ref[...], wb_ref[...])


def out_proj_residual(ya, yb, wa, wb, x, *, tm):
    n, d = x.shape
    ka, kb = ya.shape[1], yb.shape[1]
    return pl.pallas_call(
        _out_proj_kernel,
        grid=(n // tm,),
        in_specs=[pl.BlockSpec((tm, ka), lambda i: (i, 0)),
                  pl.BlockSpec((tm, kb), lambda i: (i, 0)),
                  pl.BlockSpec((ka, d), lambda i: (0, 0)),
                  pl.BlockSpec((kb, d), lambda i: (0, 0)),
                  pl.BlockSpec((tm, d), lambda i: (i, 0))],
        out_specs=pl.BlockSpec((tm, d), lambda i: (i, 0)),
        out_shape=jax.ShapeDtypeStruct((n, d), F32),
        compiler_params=_cparams("parallel"),
        name="out_proj_residual",
    )(ya, yb, wa, wb, x)


def _s5_param_kernel(are_ref, aim_ref, ldt_ref, bre_ref, bim_ref,
                     abre_ref, abim_ref, bbre_ref, bbim_ref):
    lam_re = jnp.minimum(are_ref[...], -1e-4)
    lam_im = aim_ref[...]
    dt = jnp.exp(ldt_ref[...])
    mag = jnp.exp(lam_re * dt)
    ab_re = mag * jnp.cos(lam_im * dt)
    ab_im = mag * jnp.sin(lam_im * dt)
    den = lam_re * lam_re + lam_im * lam_im
    coef_re = ((ab_re - 1.0) * lam_re + ab_im * lam_im) / den
    coef_im = (ab_im * lam_re - (ab_re - 1.0) * lam_im) / den
    abre_ref[...] = ab_re
    abim_ref[...] = ab_im
    b_re = bre_ref[...]
    b_im = bim_ref[...]
    bbre_ref[...] = coef_re * b_re - coef_im * b_im
    bbim_ref[...] = coef_re * b_im + coef_im * b_re


def s5_params(a_re, a_im, log_dt, b_re_t, b_im_t):
    g, _, p = a_re.shape
    m = b_re_t.shape[1]
    return pl.pallas_call(
        _s5_param_kernel,
        out_shape=(jax.ShapeDtypeStruct((g, 1, p), F32), jax.ShapeDtypeStruct((g, 1, p), F32),
                   jax.ShapeDtypeStruct((g, m, p), F32), jax.ShapeDtypeStruct((g, m, p), F32)),
        name="s5_params",
    )(a_re, a_im, log_dt, b_re_t, b_im_t)


def _gelu_tanh(y):
    return 0.5 * y * (1.0 + jnp.tanh(math.sqrt(2.0 / math.pi) * (y + 0.044715 * (y * y * y))))


def _s5_scan_kernel(u_ref, bb_ref, cre_ref, cim_ref, are_ref, aim_ref, d_ref, o_ref,
                    xs_ref, st_ref, *, lt, bsz):
    ns = S5_BLOCK_STATE

    @pl.when(pl.program_id(1) == 0)
    def _():
        st_ref[...] = jnp.zeros_like(st_ref)

    u2 = u_ref[...].reshape(lt * bsz, LANES)
    xs_ref[...] = _dot(u2.astype(BF16), bb_ref[...])
    a_re = jnp.broadcast_to(are_ref[...], (bsz, ns))
    a_im = jnp.broadcast_to(aim_ref[...], (bsz, ns))

    def step(t, carry):
        x_re, x_im = carry
        r0 = pl.multiple_of(t * bsz, bsz)
        n_re = a_re * x_re - a_im * x_im + xs_ref[pl.ds(r0, bsz), 0:ns]
        n_im = a_re * x_im + a_im * x_re + xs_ref[pl.ds(r0, bsz), ns:2 * ns]
        xs_ref[pl.ds(r0, bsz), 0:ns] = n_re
        xs_ref[pl.ds(r0, bsz), ns:2 * ns] = n_im
        return n_re, n_im

    x_re, x_im = lax.fori_loop(0, lt, step, (st_ref[0], st_ref[1]), unroll=8)
    st_ref[0] = x_re
    st_ref[1] = x_im

    y = (_dot(xs_ref[:, 0:ns].astype(BF16), cre_ref[...])
         - _dot(xs_ref[:, ns:2 * ns].astype(BF16), cim_ref[...]))
    y = y + d_ref[...] * u2
    o_ref[...] = _gelu_tanh(y).reshape(lt, bsz, LANES)


def s5_scan(u_tm, bb_blk, cre_blk, cim_blk, ab_re, ab_im, d, *, lt=256):
    seq, bsz, dm = u_tm.shape
    nblk = dm // LANES
    ns = S5_BLOCK_STATE
    assert seq % lt == 0
    return pl.pallas_call(
        functools.partial(_s5_scan_kernel, lt=lt, bsz=bsz),
        grid=(nblk, seq // lt),
        in_specs=[pl.BlockSpec((lt, bsz, LANES), lambda c, t: (t, 0, c)),
                  pl.BlockSpec((None, LANES, 2 * ns), lambda c, t: (c, 0, 0)),
                  pl.BlockSpec((None, ns, LANES), lambda c, t: (c, 0, 0)),
                  pl.BlockSpec((None, ns, LANES), lambda c, t: (c, 0, 0)),
                  pl.BlockSpec((None, 1, ns), lambda c, t: (c, 0, 0)),
                  pl.BlockSpec((None, 1, ns), lambda c, t: (c, 0, 0)),
                  pl.BlockSpec((1, LANES), lambda c, t: (0, c))],
        out_specs=pl.BlockSpec((lt, bsz, LANES), lambda c, t: (t, 0, c)),
        out_shape=jax.ShapeDtypeStruct((seq, bsz, dm), F32),
        scratch_shapes=[pltpu.VMEM((lt * bsz, 2 * ns), F32),
                        pltpu.VMEM((2, bsz, ns), F32)],
        compiler_params=_cparams("parallel", "arbitrary"),
        name="s5_scan",
    )(u_tm, bb_blk, cre_blk, cim_blk, ab_re, ab_im, d)


def _glu_kernel(y_ref, w_ref, x_ref, o_ref):
    d = x_ref.shape[1]
    z = _dot(y_ref[...].astype(BF16), w_ref[...])
    o_ref[...] = x_ref[...] + z[:, :d] * jax.nn.sigmoid(z[:, d:])


def glu_residual(y_tm2d, w, x, *, batch, seq, tm):
    n, d = x.shape
    tb = seq // tm
    return pl.pallas_call(
        _glu_kernel,
        grid=(batch, tb),
        in_specs=[pl.BlockSpec((tm, d), lambda b, t: (t, b)),
                  pl.BlockSpec(w.shape, lambda b, t: (0, 0)),
                  pl.BlockSpec((tm, d), lambda b, t: (b * tb + t, 0))],
        out_specs=pl.BlockSpec((tm, d), lambda b, t: (b * tb + t, 0)),
        out_shape=jax.ShapeDtypeStruct((n, d), F32),
        compiler_params=_cparams("parallel", "parallel"),
        name="glu_residual",
    )(y_tm2d, w, x)


def _xattn_kernel(x_ref, g_ref, wq_ref, k_ref, v_ref, wo_ref, o_ref, *, heads):
    x = x_ref[...]
    d = x.shape[1]
    hd = d // heads
    h = _rms(x, g_ref[...]).astype(BF16)
    q = _dot(h, wq_ref[...]).astype(BF16)
    outs = []
    for i in range(heads):
        sl = slice(i * hd, (i + 1) * hd)
        s = _dot_nt(q[:, sl], k_ref[:, sl]) * (hd ** -0.5)
        s = s - jnp.max(s, axis=-1, keepdims=True)
        e = jnp.exp(s)
        p = e / jnp.sum(e, axis=-1, keepdims=True)
        outs.append(_dot(p.astype(BF16), v_ref[:, sl]))
    o = jnp.concatenate(outs, axis=-1).astype(BF16)
    o_ref[...] = x + _dot(o, wo_ref[...])


def cross_attention_residual(x, g, wq, kv, wo, *, batch, seq, n_mem, tm):
    n, d = x.shape
    tb = seq // tm
    return pl.pallas_call(
        functools.partial(_xattn_kernel, heads=XA_HEADS),
        grid=(batch, tb),
        in_specs=[pl.BlockSpec((tm, d), lambda b, t: (b * tb + t, 0)),
                  pl.BlockSpec((1, d), lambda b, t: (0, 0)),
                  pl.BlockSpec((d, d), lambda b, t: (0, 0)),
                  pl.BlockSpec((n_mem, d), lambda b, t: (b, 0)),
                  pl.BlockSpec((n_mem, d), lambda b, t: (b, 1)),
                  pl.BlockSpec((d, d), lambda b, t: (0, 0))],
        out_specs=pl.BlockSpec((tm, d), lambda b, t: (b * tb + t, 0)),
        out_shape=jax.ShapeDtypeStruct((n, d), F32),
        compiler_params=_cparams("parallel", "parallel"),
        name="cross_attention",
    )(x, g, wq, kv, kv, wo)


def _ffn_kernel(x_ref, g_ref, wg_ref, wu_ref, w2_ref, o_ref, h_ref, acc_ref):
    c = pl.program_id(1)

    @pl.when(c == 0)
    def _():
        h_ref[...] = _rms(x_ref[...], g_ref[...]).astype(BF16)
        acc_ref[...] = jnp.zeros_like(acc_ref)

    h = h_ref[...]
    a = _silu(_dot(h, wg_ref[...])) * _dot(h, wu_ref[...])
    acc_ref[...] += _dot(a.astype(BF16), w2_ref[...])

    @pl.when(c == pl.num_programs(1) - 1)
    def _():
        o_ref[...] = x_ref[...] + acc_ref[...]


def ffn_residual(x, g, w13, w2, *, tm, tf):
    n, d = x.shape
    dff = w2.shape[0]
    nc = dff // tf
    return pl.pallas_call(
        _ffn_kernel,
        grid=(n // tm, nc),
        in_specs=[pl.BlockSpec((tm, d), lambda i, c: (i, 0)),
                  pl.BlockSpec((1, d), lambda i, c: (0, 0)),
                  pl.BlockSpec((d, tf), lambda i, c: (0, c)),
                  pl.BlockSpec((d, tf), lambda i, c: (0, nc + c)),
                  pl.BlockSpec((tf, d), lambda i, c: (c, 0))],
        out_specs=pl.BlockSpec((tm, d), lambda i, c: (i, 0)),
        out_shape=jax.ShapeDtypeStruct((n, d), F32),
        scratch_shapes=[pltpu.VMEM((tm, d), BF16), pltpu.VMEM((tm, d), F32)],
        compiler_params=_cparams("parallel", "arbitrary"),
        name="ffn_residual",
    )(x, g, w13, w13, w2)


def _top2_gates(logits, n_experts):
    lane = lax.broadcasted_iota(jnp.int32, logits.shape, 1).astype(F32)
    neg = -jnp.inf
    lg = jnp.where(lane < n_experts, logits, neg)
    m1 = jnp.max(lg, axis=-1, keepdims=True)
    i1 = jnp.min(jnp.where(lg == m1, lane, float(LANES)), axis=-1, keepdims=True)
    first = lane == i1
    lg2 = jnp.where(first, neg, lg)
    m2 = jnp.max(lg2, axis=-1, keepdims=True)
    i2 = jnp.min(jnp.where(lg2 == m2, lane, float(LANES)), axis=-1, keepdims=True)
    second = lane == i2
    e2 = jnp.exp(m2 - m1)
    den = 1.0 + e2
    return jnp.where(first, 1.0 / den, 0.0) + jnp.where(second, e2 / den, 0.0)


def _moe_kernel(x_ref, g_ref, r_ref, wg_ref, wu_ref, w2_ref, o_ref, h_ref, gate_ref, acc_ref,
                *, n_experts):
    e = pl.program_id(1)
    c = pl.program_id(2)

    @pl.when((e == 0) & (c == 0))
    def _():
        hf = _rms(x_ref[...], g_ref[...])
        h_ref[...] = hf.astype(BF16)
        gate_ref[...] = _top2_gates(_dot(hf, r_ref[...], precision=HIGHEST), n_experts)
        acc_ref[...] = jnp.zeros_like(acc_ref)

    h = h_ref[...]
    a = _silu(_dot(h, wg_ref[...])) * _dot(h, wu_ref[...])
    gates = gate_ref[...]
    lane = lax.broadcasted_iota(jnp.int32, gates.shape, 1)
    gate_e = jnp.sum(jnp.where(lane == e, gates, 0.0), axis=-1, keepdims=True)
    acc_ref[...] += gate_e * _dot(a.astype(BF16), w2_ref[...])

    @pl.when((e == pl.num_programs(1) - 1) & (c == pl.num_programs(2) - 1))
    def _():
        o_ref[...] = x_ref[...] + acc_ref[...]


def moe_residual(x, g, router, w13, w2, *, tm, tf):
    n, d = x.shape
    n_experts, dff, _ = w2.shape
    nc = dff // tf
    return pl.pallas_call(
        functools.partial(_moe_kernel, n_experts=n_experts),
        grid=(n // tm, n_experts, nc),
        in_specs=[pl.BlockSpec((tm, d), lambda i, e, c: (i, 0)),
                  pl.BlockSpec((1, d), lambda i, e, c: (0, 0)),
                  pl.BlockSpec((d, LANES), lambda i, e, c: (0, 0)),
                  pl.BlockSpec((None, d, tf), lambda i, e, c: (e, 0, c)),
                  pl.BlockSpec((None, d, tf), lambda i, e, c: (e, 0, nc + c)),
                  pl.BlockSpec((None, tf, d), lambda i, e, c: (e, c, 0))],
        out_specs=pl.BlockSpec((tm, d), lambda i, e, c: (i, 0)),
        out_shape=jax.ShapeDtypeStruct((n, d), F32),
        scratch_shapes=[pltpu.VMEM((tm, d), BF16), pltpu.VMEM((tm, LANES), F32),
                        pltpu.VMEM((tm, d), F32)],
        compiler_params=_cparams("parallel", "arbitrary", "arbitrary"),
        name="moe_residual",
    )(x, g, router, w13, w13, w2)


def _final_norm_kernel(x_ref, g_ref, o_ref):
    o_ref[...] = _rms(x_ref[...], g_ref[...])


def final_norm(x, g, *, tm):
    n, d = x.shape
    return pl.pallas_call(
        _final_norm_kernel,
        grid=(n // tm,),
        in_specs=[pl.BlockSpec((tm, d), lambda i: (i, 0)), pl.BlockSpec((1, d), lambda i: (0, 0))],
        out_specs=pl.BlockSpec((tm, d), lambda i: (i, 0)),
        out_shape=jax.ShapeDtypeStruct((n, d), F32),
        compiler_params=_cparams("parallel"),
        name="final_norm",
    )(x, g)


def _row(v):
    return v.reshape(1, -1).astype(F32)


def _pad_cols(a, cols):
    return jnp.pad(a, ((0, 0), (0, cols - a.shape[1])))


def _row_tile(n, pref):
    t = min(pref, n)
    while n % t:
        t //= 2
    return t


def ab_mixer_layer(x, norm_g, w_in, w_out, lb_logits, hg_onorm_g, rw_mu, rw_w0, rw_w2, rw_a0,
                   rw_a2, rw_g2, rw_k_k, rw_k_a, rw_r_k, rw_gn_g, rw_gn_b, *, batch, seq, layer_j):
    n, d = x.shape
    hg_width = lb_logits.shape[1]
    rw_width = rw_w0.shape[0]
    hg_cols = 4 * hg_width
    rw_cols = w_in.shape[1] - hg_cols
    rw_pad = -(-rw_cols // (2 * LANES)) * (2 * LANES)
    tm = _row_tile(n, 512)
    g = _row(norm_g)
    p_hg = norm_matmul(x, g, w_in[:, :hg_cols].astype(BF16), tm=tm, tn=hg_cols // 2)
    p_rw = norm_matmul(x, g, _pad_cols(w_in[:, hg_cols:], rw_pad).astype(BF16), tm=tm, tn=rw_pad // 2)
    y_a = hgrn2_mix(p_hg, lb_logits.astype(F32), _row(hg_onorm_g), batch=batch, seq=seq,
                    layer_j=layer_j, tt=min(256, seq))
    y_b = rwkv7_mix(p_rw, _pad_cols(_row(rw_mu), rw_pad), _row(rw_w0), rw_w2.astype(BF16),
                    _row(rw_a0), rw_a2.astype(BF16), rw_g2.astype(BF16), _row(rw_k_k),
                    _row(rw_k_a), _row(rw_r_k), _row(rw_gn_g), _row(rw_gn_b),
                    batch=batch, seq=seq)
    return out_proj_residual(y_a, y_b, w_out[:hg_width].astype(BF16), w_out[hg_width:].astype(BF16),
                             x, tm=tm)


def _block_diag(t):
    nblk, gpb, r, c = t.shape
    eye = jnp.eye(gpb, dtype=t.dtype)
    return jnp.einsum('bgrc,gh->bgrhc', t, eye).reshape(nblk, gpb * r, gpb * c)


def s5_mixer_layer(x, norm_g, w_in, a_re, a_im, log_dt, b_re, b_im, c_re, c_im, d_skip, w_glu,
                   *, batch, seq):
    n, d = x.shape
    groups, p_state = a_re.shape
    gpb = S5_GROUPS_PER_BLOCK
    nblk = groups // gpb
    tm = _row_tile(seq, 512)
    tb = seq // tm
    u_tm = norm_matmul(x, _row(norm_g), w_in.astype(BF16), tm=tm, tn=d, out_shape=(seq, batch * d),
                       out_map=lambda i, j: (i % tb, i // tb))
    ab_re, ab_im, bb_re, bb_im = s5_params(
        a_re.reshape(groups, 1, p_state).astype(F32), a_im.reshape(groups, 1, p_state).astype(F32),
        log_dt.reshape(groups, 1, 1).astype(F32),
        jnp.swapaxes(b_re, 1, 2).astype(F32), jnp.swapaxes(b_im, 1, 2).astype(F32))
    m = bb_re.shape[1]
    bb_blk = jnp.concatenate([_block_diag(bb_re.reshape(nblk, gpb, m, p_state)),
                              _block_diag(bb_im.reshape(nblk, gpb, m, p_state))], axis=-1)
    cre_blk = _block_diag(jnp.swapaxes(c_re, 1, 2).reshape(nblk, gpb, p_state, m))
    cim_blk = _block_diag(jnp.swapaxes(c_im, 1, 2).reshape(nblk, gpb, p_state, m))
    y_tm = s5_scan(u_tm.reshape(seq, batch, d), bb_blk.astype(BF16), cre_blk.astype(BF16),
                   cim_blk.astype(BF16), ab_re.reshape(nblk, 1, gpb * p_state),
                   ab_im.reshape(nblk, 1, gpb * p_state), _row(d_skip), lt=min(256, seq))
    return glu_residual(y_tm.reshape(seq, batch * d), w_glu.astype(BF16), x,
                        batch=batch, seq=seq, tm=tm)


def kernel(x, mem, mix_norm_g, xattn_norm_g, ffn_norm_g, mem_norm_g, final_norm_g, ab_w_in, ab_w_out, hg_lb_logits, hg_onorm_g, rw_mu, rw_w0, rw_w2, rw_a0, rw_a2, rw_g2, rw_k_k, rw_k_a, rw_r_k, rw_gn_g, rw_gn_b, c_w_in, s5_a_re, s5_a_im, s5_log_dt, s5_b_re, s5_b_im, s5_c_re, s5_c_im, s5_d, c_w_glu, xa_wq, xa_wkv, xa_wo, ffn_w13, ffn_w2, moe_router, moe_w13, moe_w2):
    batch, seq, d = x.shape
    n_mem = mem.shape[1]
    depth = mix_norm_g.shape[0]
    n = batch * seq
    x = x.reshape(n, d).astype(F32)
    mem2 = mem.reshape(batch * n_mem, d).astype(F32)
    tm = _row_tile(n, 512)
    tm_seq = _row_tile(seq, 512)
    dff = ffn_w2.shape[1]
    tf = dff // 2 if (dff // 2) % LANES == 0 else dff

    for layer in range(depth):
        j = layer // 2
        if layer % 2 == 0:
            x = ab_mixer_layer(x, mix_norm_g[layer], ab_w_in[j], ab_w_out[j], hg_lb_logits,
                               hg_onorm_g[j], rw_mu[j], rw_w0[j], rw_w2[j], rw_a0[j], rw_a2[j],
                               rw_g2[j], rw_k_k[j], rw_k_a[j], rw_r_k[j], rw_gn_g[j], rw_gn_b[j],
                               batch=batch, seq=seq, layer_j=j)
        else:
            x = s5_mixer_layer(x, mix_norm_g[layer], c_w_in[j], s5_a_re[j], s5_a_im[j],
                               s5_log_dt[j], s5_b_re[j], s5_b_im[j], s5_c_re[j], s5_c_im[j],
                               s5_d[j], c_w_glu[j], batch=batch, seq=seq)

        kv = norm_matmul(mem2, _row(mem_norm_g), xa_wkv[layer].astype(BF16),
                         tm=_row_tile(batch * n_mem, 512), tn=d, out_dtype=BF16)
        x = cross_attention_residual(x, _row(xattn_norm_g[layer]), xa_wq[layer].astype(BF16), kv,
                                     xa_wo[layer].astype(BF16), batch=batch, seq=seq, n_mem=n_mem,
                                     tm=tm_seq)

        if layer % 2 == 0:
            x = ffn_residual(x, _row(ffn_norm_g[layer]), ffn_w13[j].astype(BF16),
                             ffn_w2[j].astype(BF16), tm=tm, tf=tf)
        else:
            router = _pad_cols(moe_router[j].astype(F32), LANES)
            x = moe_residual(x, _row(ffn_norm_g[layer]), router, moe_w13[j].astype(BF16),
                             moe_w2[j].astype(BF16), tm=tm, tf=tf)

    return final_norm(x, _row(final_norm_g), tm=tm).reshape(batch, seq, d)
```

```python
import functools
import math

import jax
import jax.numpy as jnp
from jax import lax
from jax.experimental import pallas as pl
from jax.experimental.pallas import tpu as pltpu

F32 = jnp.float32
BF16 = jnp.bfloat16

NORM_EPS = 1e-6
LANES = 128
SUBLANES = 8
VMEM_LIMIT_BYTES = 60 * 1024 * 1024

HG_DK = 128
HG_CHUNK = 64
HG_SUB = 16
HG_MASK_LOG = -1e30
HG_EXP_CLIP = 60.0

RW_N = 64
RW_CHUNK = 64
RW_GROUP_HEADS = 4
RW_STREAMS = 2
RW_DECAY_LORA = 32
RW_A_LORA = 32
RW_GATE_LORA = 96
RW_GN_EPS = 64e-5

S5_GROUP = 16
S5_STATE = 64
S5_GROUPS_PER_BLOCK = LANES // S5_GROUP
S5_BLOCK_STATE = S5_GROUPS_PER_BLOCK * S5_STATE

XA_HEADS = 4
MOE_TOKEN_TILE = 1024
MOE_ROW_BLOCK = 128
MOE_RANK_BLOCK = 256

HIGHEST = lax.Precision.HIGHEST


def _cparams(*sem):
    return pltpu.CompilerParams(dimension_semantics=sem, vmem_limit_bytes=VMEM_LIMIT_BYTES)


def _dot(a, b, precision=None):
    return jnp.dot(a, b, preferred_element_type=F32, precision=precision)


def _dot_nt(a, b, precision=None):
    return lax.dot_general(a, b, (((1,), (1,)), ((), ())), preferred_element_type=F32,
                           precision=precision)


def _dot_tn(a, b, precision=None):
    return lax.dot_general(a, b, (((0,), (0,)), ((), ())), preferred_element_type=F32,
                           precision=precision)


def _split_bf16(t, terms):
    parts = []
    for _ in range(terms):
        hi = t.astype(BF16)
        parts.append(hi)
        t = t - hi.astype(F32)
    return parts


def _rms(x, g, eps=NORM_EPS):
    return x * lax.rsqrt(jnp.mean(x * x, axis=-1, keepdims=True) + eps) * g


def _silu(x):
    return x * jax.nn.sigmoid(x)


def _tri_incl(n):
    r = lax.broadcasted_iota(jnp.int32, (n, n), 0)
    c = lax.broadcasted_iota(jnp.int32, (n, n), 1)
    return jnp.where(r >= c, 1.0, 0.0).astype(F32)


def _norm_matmul_kernel(x_ref, g_ref, w_ref, o_ref, h_ref):
    @pl.when(pl.program_id(1) == 0)
    def _():
        h_ref[...] = _rms(x_ref[...], g_ref[...]).astype(BF16)

    o_ref[...] = _dot(h_ref[...], w_ref[...]).astype(o_ref.dtype)


def norm_matmul(x, g, w, *, tm, tn, out_dtype=F32, out_shape=None, out_map=None):
    m, k = x.shape
    n = w.shape[1]
    assert m % tm == 0 and n % tn == 0
    if out_shape is None:
        out_shape = (m, n)
    if out_map is None:
        out_map = lambda i, j: (i, j)
    return pl.pallas_call(
        _norm_matmul_kernel,
        grid=(m // tm, n // tn),
        in_specs=[pl.BlockSpec((tm, k), lambda i, j: (i, 0)),
                  pl.BlockSpec((1, k), lambda i, j: (0, 0)),
                  pl.BlockSpec((k, tn), lambda i, j: (0, j))],
        out_specs=pl.BlockSpec((tm, tn), out_map),
        out_shape=jax.ShapeDtypeStruct(out_shape, out_dtype),
        scratch_shapes=[pltpu.VMEM((tm, k), BF16)],
        compiler_params=_cparams("parallel", "arbitrary"),
        name="norm_matmul",
    )(x, g, w)


def _hgrn2_kernel(q_ref, f_ref, i_ref, g_ref, lbl_ref, og_ref, o_ref, st_ref, *, layer_j, n_chunks):
    c = HG_CHUNK
    nsub = c // HG_SUB

    @pl.when(pl.program_id(2) == 0)
    def _():
        st_ref[...] = jnp.zeros_like(st_ref)

    logits = lbl_ref[...]
    e = jnp.exp(logits - jnp.max(logits, axis=0, keepdims=True))
    p = e / jnp.sum(e, axis=0, keepdims=True)
    lb = jnp.zeros((1, p.shape[1]), F32)
    for r in range(1, layer_j + 1):
        lb = lb + p[r:r + 1, :]

    tri = _tri_incl(c).astype(BF16)
    t_loc = lax.broadcasted_iota(jnp.int32, (nsub, HG_SUB, HG_DK), 1)
    og = og_ref[...]

    def chunk(ci, carry):
        r0 = pl.multiple_of(ci * c, c)
        qr = q_ref[pl.ds(r0, c), :]
        fr = f_ref[pl.ds(r0, c), :]
        iv = i_ref[pl.ds(r0, c), :]
        gr = g_ref[pl.ds(r0, c), :]
        q = _silu(qr)
        log_sig = jnp.minimum(fr, 0.0) - jnp.log1p(jnp.exp(-jnp.abs(fr)))
        log_f = log_sig + jnp.log1p(lb * jnp.exp(jnp.minimum(-fr, HG_EXP_CLIP)))
        k = (1.0 - lb) * jax.nn.sigmoid(-fr)
        l = sum(_dot(tri, part) for part in _split_bf16(log_f, 3))
        l_end = l[c - 1:c, :]
        st = st_ref[...]
        iv_b = iv.astype(BF16)

        o = _dot_nt((q * jnp.exp(l)).astype(BF16), st.astype(BF16))

        parts = [jnp.zeros((HG_SUB, HG_DK), F32)]
        for sb in range(1, nsub):
            lo = sb * HG_SUB
            rho = l[lo - 1:lo, :]
            qt = q[lo:lo + HG_SUB, :] * jnp.exp(l[lo:lo + HG_SUB, :] - rho)
            kt = k[:lo, :] * jnp.exp(rho - l[:lo, :])
            a = _dot_nt(qt.astype(BF16), kt.astype(BF16))
            parts.append(_dot(a.astype(BF16), iv_b[:lo, :]))
        o = o + jnp.concatenate(parts, axis=0)

        q3 = q.reshape(nsub, HG_SUB, HG_DK)
        k3 = k.reshape(nsub, HG_SUB, HG_DK)
        l3 = l.reshape(nsub, HG_SUB, HG_DK)
        i3 = iv.reshape(nsub, HG_SUB, HG_DK)
        od = jnp.zeros((nsub, HG_SUB, HG_DK), F32)
        for s in range(HG_SUB):
            diff = l3 - l3[:, s:s + 1, :]
            w = jnp.exp(jnp.where(t_loc >= s, diff, HG_MASK_LOG))
            col = jnp.sum(q3 * k3[:, s:s + 1, :] * w, axis=-1, keepdims=True)
            od = od + col * i3[:, s:s + 1, :]
        o = o + od.reshape(c, HG_DK)

        k_dec = k * jnp.exp(l_end - l)
        st_ref[...] = st * jnp.exp(l_end) + _dot_tn(iv_b, k_dec.astype(BF16))

        y = _rms(o, og) * _silu(gr)
        o_ref[pl.ds(r0, c), :] = y.astype(o_ref.dtype)
        return carry

    lax.fori_loop(0, n_chunks, chunk, 0, unroll=True)


def hgrn2_mix(p_hg, lb_logits, onorm_g, *, batch, seq, layer_j, tt=256):
    n, cols = p_hg.shape
    width = cols // 4
    heads = width // HG_DK
    assert seq % tt == 0 and tt % HG_CHUNK == 0
    tb = seq // tt
    n_ab = lb_logits.shape[0]

    def col_spec(part):
        return pl.BlockSpec((tt, HG_DK), lambda b, h, t, part=part: (b * tb + t, part * heads + h))

    kern = functools.partial(_hgrn2_kernel, layer_j=layer_j, n_chunks=tt // HG_CHUNK)
    return pl.pallas_call(
        kern,
        grid=(batch, heads, tb),
        in_specs=[col_spec(0), col_spec(1), col_spec(2), col_spec(3),
                  pl.BlockSpec((n_ab, HG_DK), lambda b, h, t: (0, h)),
                  pl.BlockSpec((1, HG_DK), lambda b, h, t: (0, 0))],
        out_specs=pl.BlockSpec((tt, HG_DK), lambda b, h, t: (b * tb + t, h)),
        out_shape=jax.ShapeDtypeStruct((n, width), BF16),
        scratch_shapes=[pltpu.VMEM((HG_DK, HG_DK), F32)],
        compiler_params=_cparams("parallel", "parallel", "arbitrary"),
        name="hgrn2_mix",
    )(p_hg, p_hg, p_hg, p_hg, lb_logits, onorm_g)


def _round_robin(chains):
    chains = list(chains)
    while chains:
        alive = []
        for ch in chains:
            try:
                next(ch)
                alive.append(ch)
            except StopIteration:
                pass
        chains = alive
        yield


def _rwkv7_stream(p_ref, mu_ref, w0_ref, w2_ref, a0_ref, a2_ref, g2_ref, kk_ref, ka_ref,
                  rk_ref, gng_ref, gnb_ref, ones_ref, o_ref, prev_ref, st_ref, *, heads):
    c = RW_CHUNK
    n = RW_N
    assert c == n
    width = heads * n
    gh = RW_GROUP_HEADS
    gw = gh * n
    gr = gh * c
    shift = c.bit_length() - 1

    @pl.when(pl.program_id(1) == 0)
    def _():
        prev_ref[...] = jnp.zeros_like(prev_ref)
        st_ref[...] = jnp.zeros_like(st_ref)

    p = p_ref[...]
    row = lax.broadcasted_iota(jnp.int32, p.shape, 0)
    shifted = jnp.where(row == 0, prev_ref[0:1, :], pltpu.roll(p, 1, axis=0))
    prev_ref[0:1, :] = p[c - 1:c, :]
    p = p + (shifted - p) * mu_ref[...]

    r = p[:, 0:width]
    k = p[:, width:2 * width]
    v = p[:, 2 * width:3 * width]
    off = 3 * width
    wd = p[:, off:off + RW_DECAY_LORA]
    ad = p[:, off + RW_DECAY_LORA:off + RW_DECAY_LORA + RW_A_LORA]
    gd = p[:, off + RW_DECAY_LORA + RW_A_LORA:off + RW_DECAY_LORA + RW_A_LORA + RW_GATE_LORA]

    wz = w0_ref[...] + _dot(jnp.tanh(wd).astype(BF16), w2_ref[...])
    w = jnp.minimum(wz, 0.0) - jnp.log1p(jnp.exp(-jnp.abs(wz))) - 0.5
    lw = -jnp.exp(w)
    a = jax.nn.sigmoid(a0_ref[...] + _dot(ad.astype(BF16), a2_ref[...]))
    g = _dot(jax.nn.sigmoid(gd).astype(BF16), g2_ref[...])
    k2 = k * (1.0 + (a - 1.0) * ka_ref[...])

    ones = ones_ref[...]

    def head_sum(t):
        return sum(_dot(part, ones) for part in _split_bf16(t, 2))

    kk = k * kk_ref[...]
    kk = kk * lax.rsqrt(jnp.maximum(head_sum(kk * kk), 1e-24))
    b = kk * a

    tri = _tri_incl(c).astype(BF16)
    cum = sum(_dot(tri, part) for part in _split_bf16(lw, 3))
    cum_end = cum[c - 1:c, :]
    e_inv = jnp.exp(-cum)
    e_fin = jnp.exp(cum_end - cum)
    e_end = jnp.exp(cum_end)
    at = -kk * jnp.exp(cum - lw)
    rt = r * jnp.exp(cum)
    bt = b * e_inv
    kt = k2 * e_inv
    bf = b * e_fin
    kf = k2 * e_fin

    rb = lax.broadcasted_iota(jnp.int32, (gr, gw), 0)
    cb = lax.broadcasted_iota(jnp.int32, (gr, gw), 1)
    same = jnp.right_shift(rb, shift) == jnp.right_shift(cb, shift)
    t_in = jnp.bitwise_and(rb, c - 1)
    s_in = jnp.bitwise_and(cb, c - 1)
    strict = t_in > s_in
    incl = t_in >= s_in
    eye = jnp.where(rb == cb, 1.0, 0.0).astype(F32)

    def bd(t):
        return jnp.where(same, jnp.concatenate([t] * gh, axis=0), 0.0).astype(BF16)

    ngroups = heads // gh
    outs = [None] * ngroups

    def group_chain(gi):
        gs = slice(gi * gw, (gi + 1) * gw)
        art = jnp.concatenate([bd(at[:, gs]), bd(rt[:, gs])], axis=0)
        v_bd = bd(v[:, gs])
        g_b = _dot_nt(art, bd(bt[:, gs]))
        g_k = _dot_nt(art, bd(kt[:, gs]))
        yield
        a_ab = jnp.where(strict, g_b[:gr], 0.0)
        a_ak = jnp.where(strict, g_k[:gr], 0.0)
        a_rb = jnp.where(incl, g_b[gr:], 0.0)
        a_rk = jnp.where(incl, g_k[gr:], 0.0)

        tinv = eye + a_ab
        x = a_ab
        for _ in range(shift - 1):
            xb = x.astype(BF16)
            x = _dot(xb, xb)
            yield
            tinv = tinv + _dot(tinv.astype(BF16), x.astype(BF16))
            yield

        st = st_ref[gi]
        arm = _dot_nt(art, st.astype(BF16))
        akv = _dot(jnp.concatenate([a_ak, a_rk], axis=0).astype(BF16), v_bd)
        yield
        u_b = _dot(tinv.astype(BF16), (arm[:gr] + akv[:gr]).astype(BF16)).astype(BF16)
        yield
        o_bd = arm[gr:] + akv[gr:] + _dot(a_rb.astype(BF16), u_b)
        bk = jnp.concatenate([bd(bf[:, gs]), bd(kf[:, gs])], axis=0)
        st_ref[gi] = st * e_end[:, gs] + _dot_tn(jnp.concatenate([u_b, v_bd], axis=0), bk)
        o_g = o_bd[0:c]
        for hh in range(1, gh):
            o_g = o_g + o_bd[hh * c:(hh + 1) * c]
        outs[gi] = o_g

    yield
    yield from _round_robin([group_chain(gi) for gi in range(ngroups)])

    o = jnp.concatenate(outs, axis=-1)
    mean = head_sum(o) * (1.0 / n)
    dev = o - mean
    var = head_sum(dev * dev) * (1.0 / n)
    on = dev * lax.rsqrt(var + RW_GN_EPS) * gng_ref[...] + gnb_ref[...]
    bonus = head_sum(r * k2 * rk_ref[...]) * v
    o_ref[...] = ((on + bonus) * g).astype(o_ref.dtype)


def _rwkv7_kernel(p_ref, *refs, heads, streams):
    *param_refs, o_ref, prev_ref, st_ref = refs
    chains = [_rwkv7_stream(p_ref.at[bi], *param_refs, o_ref.at[bi], prev_ref.at[bi],
                            st_ref.at[bi], heads=heads) for bi in range(streams)]
    for _ in _round_robin(chains):
        pass


def rwkv7_mix(p_rw, mu, w0, w2, a0, a2, g2, k_k, k_a, r_k, gn_g, gn_b, *, batch, seq):
    n_tok, cols = p_rw.shape
    width = w0.shape[1]
    heads = width // RW_N
    assert seq % RW_CHUNK == 0 and heads % RW_GROUP_HEADS == 0
    cb = seq // RW_CHUNK
    head_id = jnp.arange(width, dtype=jnp.int32) // RW_N
    ones = (head_id[:, None] == head_id[None, :]).astype(BF16)
    full = lambda a: pl.BlockSpec(a.shape, lambda b, t: (0,) * a.ndim)
    params = (mu, w0, w2, a0, a2, g2, k_k, k_a, r_k, gn_g, gn_b, ones)
    gdim = RW_GROUP_HEADS * RW_N
    nb = RW_STREAMS if batch % RW_STREAMS == 0 else 1
    out = pl.pallas_call(
        functools.partial(_rwkv7_kernel, heads=heads, streams=nb),
        grid=(batch // nb, cb),
        in_specs=[pl.BlockSpec((nb, RW_CHUNK, cols), lambda b, t: (b, t, 0))]
                 + [full(a) for a in params],
        out_specs=pl.BlockSpec((nb, RW_CHUNK, width), lambda b, t: (b, t, 0)),
        out_shape=jax.ShapeDtypeStruct((batch, seq, width), BF16),
        scratch_shapes=[pltpu.VMEM((nb, SUBLANES, cols), F32),
                        pltpu.VMEM((nb, heads // RW_GROUP_HEADS, gdim, gdim), F32)],
        compiler_params=_cparams("parallel", "arbitrary"),
        name="rwkv7_mix",
    )(p_rw.reshape(batch, seq, cols), *params)
    return out.reshape(n_tok, width)


def _out_proj_kernel(ya_ref, yb_ref, wa_ref, wb_ref, x_ref, o_ref):
    o_ref[...] = x_ref[...] + _dot(ya_ref[...], wa_ref[...]) + _dot(yb_ref[...], wb_ref[...])


def out_proj_residual(ya, yb, wa, wb, x, *, tm):
    n, d = x.shape
    ka, kb = ya.shape[1], yb.shape[1]
    return pl.pallas_call(
        _out_proj_kernel,
        grid=(n // tm,),
        in_specs=[pl.BlockSpec((tm, ka), lambda i: (i, 0)),
                  pl.BlockSpec((tm, kb), lambda i: (i, 0)),
                  pl.BlockSpec((ka, d), lambda i: (0, 0)),
                  pl.BlockSpec((kb, d), lambda i: (0, 0)),
                  pl.BlockSpec((tm, d), lambda i: (i, 0))],
        out_specs=pl.BlockSpec((tm, d), lambda i: (i, 0)),
        out_shape=jax.ShapeDtypeStruct((n, d), F32),
        compiler_params=_cparams("parallel"),
        name="out_proj_residual",
    )(ya, yb, wa, wb, x)


def _s5_param_kernel(are_ref, aim_ref, ldt_ref, bre_ref, bim_ref,
                     abre_ref, abim_ref, bbre_ref, bbim_ref):
    lam_re = jnp.minimum(are_ref[...], -1e-4)
    lam_im = aim_ref[...]
    dt = jnp.exp(ldt_ref[...])
    mag = jnp.exp(lam_re * dt)
    ab_re = mag * jnp.cos(lam_im * dt)
    ab_im = mag * jnp.sin(lam_im * dt)
    den = lam_re * lam_re + lam_im * lam_im
    coef_re = ((ab_re - 1.0) * lam_re + ab_im * lam_im) / den
    coef_im = (ab_im * lam_re - (ab_re - 1.0) * lam_im) / den
    abre_ref[...] = ab_re
    abim_ref[...] = ab_im
    b_re = bre_ref[...]
    b_im = bim_ref[...]
    bbre_ref[...] = coef_re * b_re - coef_im * b_im
    bbim_ref[...] = coef_re * b_im + coef_im * b_re


def s5_params(a_re, a_im, log_dt, b_re_t, b_im_t):
    g, _, p = a_re.shape
    m = b_re_t.shape[1]
    return pl.pallas_call(
        _s5_param_kernel,
        out_shape=(jax.ShapeDtypeStruct((g, 1, p), F32), jax.ShapeDtypeStruct((g, 1, p), F32),
                   jax.ShapeDtypeStruct((g, m, p), F32), jax.ShapeDtypeStruct((g, m, p), F32)),
        name="s5_params",
    )(a_re, a_im, log_dt, b_re_t, b_im_t)


def _gelu_tanh(y):
    return 0.5 * y * (1.0 + jnp.tanh(math.sqrt(2.0 / math.pi) * (y + 0.044715 * (y * y * y))))


def _s5_scan_kernel(u_ref, bb_ref, cre_ref, cim_ref, are_ref, aim_ref, d_ref, o_ref,
                    xs_ref, st_ref, *, lt, bsz):
    ns = S5_BLOCK_STATE

    @pl.when(pl.program_id(1) == 0)
    def _():
        st_ref[...] = jnp.zeros_like(st_ref)

    u2 = u_ref[...].reshape(lt * bsz, LANES)
    xs_ref[...] = _dot(u2.astype(BF16), bb_ref[...])
    a_re = jnp.broadcast_to(are_ref[...], (bsz, ns))
    a_im = jnp.broadcast_to(aim_ref[...], (bsz, ns))

    def step(t, carry):
        x_re, x_im = carry
        r0 = pl.multiple_of(t * bsz, bsz)
        n_re = a_re * x_re - a_im * x_im + xs_ref[pl.ds(r0, bsz), 0:ns]
        n_im = a_re * x_im + a_im * x_re + xs_ref[pl.ds(r0, bsz), ns:2 * ns]
        xs_ref[pl.ds(r0, bsz), 0:ns] = n_re
        xs_ref[pl.ds(r0, bsz), ns:2 * ns] = n_im
        return n_re, n_im

    x_re, x_im = lax.fori_loop(0, lt, step, (st_ref[0], st_ref[1]), unroll=8)
    st_ref[0] = x_re
    st_ref[1] = x_im

    y = (_dot(xs_ref[:, 0:ns].astype(BF16), cre_ref[...])
         - _dot(xs_ref[:, ns:2 * ns].astype(BF16), cim_ref[...]))
    y = y + d_ref[...] * u2
    o_ref[...] = _gelu_tanh(y).reshape(lt, bsz, LANES)


def s5_scan(u_tm, bb_blk, cre_blk, cim_blk, ab_re, ab_im, d, *, lt=256):
    seq, bsz, dm = u_tm.shape
    nblk = dm // LANES
    ns = S5_BLOCK_STATE
    assert seq % lt == 0
    return pl.pallas_call(
        functools.partial(_s5_scan_kernel, lt=lt, bsz=bsz),
        grid=(nblk, seq // lt),
        in_specs=[pl.BlockSpec((lt, bsz, LANES), lambda c, t: (t, 0, c)),
                  pl.BlockSpec((None, LANES, 2 * ns), lambda c, t: (c, 0, 0)),
                  pl.BlockSpec((None, ns, LANES), lambda c, t: (c, 0, 0)),
                  pl.BlockSpec((None, ns, LANES), lambda c, t: (c, 0, 0)),
                  pl.BlockSpec((None, 1, ns), lambda c, t: (c, 0, 0)),
                  pl.BlockSpec((None, 1, ns), lambda c, t: (c, 0, 0)),
                  pl.BlockSpec((1, LANES), lambda c, t: (0, c))],
        out_specs=pl.BlockSpec((lt, bsz, LANES), lambda c, t: (t, 0, c)),
        out_shape=jax.ShapeDtypeStruct((seq, bsz, dm), F32),
        scratch_shapes=[pltpu.VMEM((lt * bsz, 2 * ns), F32),
                        pltpu.VMEM((2, bsz, ns), F32)],
        compiler_params=_cparams("parallel", "arbitrary"),
        name="s5_scan",
    )(u_tm, bb_blk, cre_blk, cim_blk, ab_re, ab_im, d)


def _glu_kernel(y_ref, w_ref, x_ref, o_ref):
    d = x_ref.shape[1]
    z = _dot(y_ref[...].astype(BF16), w_ref[...])
    o_ref[...] = x_ref[...] + z[:, :d] * jax.nn.sigmoid(z[:, d:])


def glu_residual(y_tm2d, w, x, *, batch, seq, tm):
    n, d = x.shape
    tb = seq // tm
    return pl.pallas_call(
        _glu_kernel,
        grid=(batch, tb),
        in_specs=[pl.BlockSpec((tm, d), lambda b, t: (t, b)),
                  pl.BlockSpec(w.shape, lambda b, t: (0, 0)),
                  pl.BlockSpec((tm, d), lambda b, t: (b * tb + t, 0))],
        out_specs=pl.BlockSpec((tm, d), lambda b, t: (b * tb + t, 0)),
        out_shape=jax.ShapeDtypeStruct((n, d), F32),
        compiler_params=_cparams("parallel", "parallel"),
        name="glu_residual",
    )(y_tm2d, w, x)


def _xattn_kernel(x_ref, g_ref, wq_ref, k_ref, v_ref, wo_ref, o_ref, *, heads):
    x = x_ref[...]
    d = x.shape[1]
    hd = d // heads
    h = _rms(x, g_ref[...]).astype(BF16)
    q = _dot(h, wq_ref[...]).astype(BF16)
    outs = []
    for i in range(heads):
        sl = slice(i * hd, (i + 1) * hd)
        s = _dot_nt(q[:, sl], k_ref[:, sl]) * (hd ** -0.5)
        s = s - jnp.max(s, axis=-1, keepdims=True)
        e = jnp.exp(s)
        p = e / jnp.sum(e, axis=-1, keepdims=True)
        outs.append(_dot(p.astype(BF16), v_ref[:, sl]))
    o = jnp.concatenate(outs, axis=-1).astype(BF16)
    o_ref[...] = x + _dot(o, wo_ref[...])


def cross_attention_residual(x, g, wq, kv, wo, *, batch, seq, n_mem, tm):
    n, d = x.shape
    tb = seq // tm
    return pl.pallas_call(
        functools.partial(_xattn_kernel, heads=XA_HEADS),
        grid=(batch, tb),
        in_specs=[pl.BlockSpec((tm, d), lambda b, t: (b * tb + t, 0)),
                  pl.BlockSpec((1, d), lambda b, t: (0, 0)),
                  pl.BlockSpec((d, d), lambda b, t: (0, 0)),
                  pl.BlockSpec((n_mem, d), lambda b, t: (b, 0)),
                  pl.BlockSpec((n_mem, d), lambda b, t: (b, 1)),
                  pl.BlockSpec((d, d), lambda b, t: (0, 0))],
        out_specs=pl.BlockSpec((tm, d), lambda b, t: (b * tb + t, 0)),
        out_shape=jax.ShapeDtypeStruct((n, d), F32),
        compiler_params=_cparams("parallel", "parallel"),
        name="cross_attention",
    )(x, g, wq, kv, kv, wo)


def _ffn_kernel(x_ref, g_ref, wg_ref, wu_ref, w2_ref, o_ref, h_ref, acc_ref):
    c = pl.program_id(1)

    @pl.when(c == 0)
    def _():
        h_ref[...] = _rms(x_ref[...], g_ref[...]).astype(BF16)
        acc_ref[...] = jnp.zeros_like(acc_ref)

    h = h_ref[...]
    a = _silu(_dot(h, wg_ref[...])) * _dot(h, wu_ref[...])
    acc_ref[...] += _dot(a.astype(BF16), w2_ref[...])

    @pl.when(c == pl.num_programs(1) - 1)
    def _():
        o_ref[...] = x_ref[...] + acc_ref[...]


def ffn_residual(x, g, w13, w2, *, tm, tf):
    n, d = x.shape
    dff = w2.shape[0]
    nc = dff // tf
    return pl.pallas_call(
        _ffn_kernel,
        grid=(n // tm, nc),
        in_specs=[pl.BlockSpec((tm, d), lambda i, c: (i, 0)),
                  pl.BlockSpec((1, d), lambda i, c: (0, 0)),
                  pl.BlockSpec((d, tf), lambda i, c: (0, c)),
                  pl.BlockSpec((d, tf), lambda i, c: (0, nc + c)),
                  pl.BlockSpec((tf, d), lambda i, c: (c, 0))],
        out_specs=pl.BlockSpec((tm, d), lambda i, c: (i, 0)),
        out_shape=jax.ShapeDtypeStruct((n, d), F32),
        scratch_shapes=[pltpu.VMEM((tm, d), BF16), pltpu.VMEM((tm, d), F32)],
        compiler_params=_cparams("parallel", "arbitrary"),
        name="ffn_residual",
    )(x, g, w13, w13, w2)


def _moe_route(logits_t, n_experts, rb):
    m = logits_t.shape[1]
    sub = lax.broadcasted_iota(jnp.int32, logits_t.shape, 0).astype(F32)
    neg = -jnp.inf
    m1 = jnp.max(logits_t, axis=0, keepdims=True)
    i1 = jnp.min(jnp.where(logits_t == m1, sub, float(n_experts)), axis=0, keepdims=True)
    first = sub == i1
    rest = jnp.where(first, neg, logits_t)
    m2 = jnp.max(rest, axis=0, keepdims=True)
    i2 = jnp.min(jnp.where(rest == m2, sub, float(n_experts)), axis=0, keepdims=True)
    second = sub == i2
    e2 = jnp.exp(m2 - m1)
    den = 1.0 + e2
    gate0 = 1.0 / den
    gate1 = e2 / den

    first_f = jnp.where(first, 1.0, 0.0)
    second_f = jnp.where(second, 1.0, 0.0)
    both = jnp.concatenate([first_f, second_f], axis=0)
    w = min(m, MOE_RANK_BLOCK)
    t_src = lax.broadcasted_iota(jnp.int32, (w, w), 0)
    t_dst = lax.broadcasted_iota(jnp.int32, (w, w), 1)
    before = jnp.where(t_src < t_dst, 1.0, 0.0).astype(BF16)
    cnt = jnp.zeros((2 * n_experts, 1), F32)
    ranks = []
    for kb in range(m // w):
        blk = both[:, kb * w:(kb + 1) * w]
        ranks.append(_dot(blk.astype(BF16), before) + cnt)
        cnt = cnt + jnp.sum(blk, axis=1, keepdims=True)
    ranks = jnp.concatenate(ranks, axis=1)
    cnt0 = cnt[:n_experts]
    cnt_e = cnt0 + cnt[n_experts:]
    padded = jnp.floor((cnt_e + (rb - 1)) * (1.0 / rb)) * rb
    sub_col = lax.broadcasted_iota(jnp.int32, (n_experts, 1), 0)
    start = jnp.zeros((n_experts, 1), F32)
    for ee in range(n_experts - 1):
        start = start + jnp.where(sub_col > ee, padded[ee:ee + 1, :], 0.0)
    dest0 = jnp.sum(first_f * (start + ranks[:n_experts]), axis=0, keepdims=True)
    dest1 = jnp.sum(second_f * (start + cnt0 + ranks[n_experts:]), axis=0, keepdims=True)
    return dest0, dest1, gate0, gate1, start, padded


def _moe_kernel(x_ref, g_ref, rt_ref, wg_ref, wu_ref, w2_ref, o_ref, xs_ref, gw_ref, y_ref, meta_ref,
                *, n_experts, rb):
    c = pl.program_id(1)
    e = pl.program_id(2)
    n_rows = xs_ref.shape[0]
    tm = x_ref.shape[0]

    @pl.when((c == 0) & (e == 0))
    def _route():
        x = x_ref[...]
        hf = _rms(x, g_ref[...])
        o_ref[...] = x
        y_ref[...] = jnp.zeros_like(y_ref)
        h_b = hf.astype(BF16)
        logits_t = _dot_nt(rt_ref[...], hf, precision=HIGHEST)
        dest0, dest1, gate0, gate1, start, padded = _moe_route(logits_t, n_experts, rb)
        for ee in range(n_experts):
            meta_ref[ee] = jnp.sum(start[ee:ee + 1, :]).astype(jnp.int32)
            meta_ref[n_experts + ee] = jnp.sum(padded[ee:ee + 1, :] * (1.0 / rb)).astype(jnp.int32)

        def build(j, carry):
            r0 = pl.multiple_of(j * rb, rb)
            rows = (lax.broadcasted_iota(jnp.int32, (rb, tm), 0) + r0).astype(F32)
            hit0 = rows == dest0
            hit1 = rows == dest1
            onehot = jnp.where(hit0 | hit1, 1.0, 0.0).astype(BF16)
            xs_ref[pl.ds(r0, rb), :] = _dot(onehot, h_b).astype(BF16)
            gw_ref[pl.ds(r0, rb), :] = (jnp.where(hit0, gate0, 0.0)
                                        + jnp.where(hit1, gate1, 0.0)).astype(BF16)
            return carry

        lax.fori_loop(0, n_rows // rb, build, 0)

    seg_start = meta_ref[e]
    seg_blocks = meta_ref[n_experts + e]

    def expert_block(j, carry):
        r0 = pl.multiple_of(seg_start + j * rb, rb)
        xb = xs_ref[pl.ds(r0, rb), :]
        a = _silu(_dot(xb, wg_ref[...])) * _dot(xb, wu_ref[...])
        y_ref[pl.ds(r0, rb), :] = _dot(a.astype(BF16), w2_ref[...]).astype(BF16)
        return carry

    lax.fori_loop(0, seg_blocks, expert_block, 0)

    @pl.when(e == pl.num_programs(2) - 1)
    def _combine():
        o_ref[...] += _dot_tn(gw_ref[...], y_ref[...])


def moe_residual(x, g, router_t, w13, w2, *, tm, tf, rb):
    n, d = x.shape
    n_experts, dff, _ = w2.shape
    nc = dff // tf
    n_rows = 2 * tm + n_experts * rb
    return pl.pallas_call(
        functools.partial(_moe_kernel, n_experts=n_experts, rb=rb),
        grid=(n // tm, nc, n_experts),
        in_specs=[pl.BlockSpec((tm, d), lambda i, c, e: (i, 0), pipeline_mode=pl.Buffered(1)),
                  pl.BlockSpec((1, d), lambda i, c, e: (0, 0)),
                  pl.BlockSpec((n_experts, d), lambda i, c, e: (0, 0)),
                  pl.BlockSpec((None, d, tf), lambda i, c, e: (e, 0, c)),
                  pl.BlockSpec((None, d, tf), lambda i, c, e: (e, 0, nc + c)),
                  pl.BlockSpec((None, tf, d), lambda i, c, e: (e, c, 0))],
        out_specs=pl.BlockSpec((tm, d), lambda i, c, e: (i, 0)),
        out_shape=jax.ShapeDtypeStruct((n, d), F32),
        scratch_shapes=[pltpu.VMEM((n_rows, d), BF16), pltpu.VMEM((n_rows, tm), BF16),
                        pltpu.VMEM((n_rows, d), BF16), pltpu.SMEM((2 * n_experts,), jnp.int32)],
        compiler_params=_cparams("parallel", "arbitrary", "arbitrary"),
        name="moe_residual",
    )(x, g, router_t, w13, w13, w2)


def _final_norm_kernel(x_ref, g_ref, o_ref):
    o_ref[...] = _rms(x_ref[...], g_ref[...])


def final_norm(x, g, *, tm):
    n, d = x.shape
    return pl.pallas_call(
        _final_norm_kernel,
        grid=(n // tm,),
        in_specs=[pl.BlockSpec((tm, d), lambda i: (i, 0)), pl.BlockSpec((1, d), lambda i: (0, 0))],
        out_specs=pl.BlockSpec((tm, d), lambda i: (i, 0)),
        out_shape=jax.ShapeDtypeStruct((n, d), F32),
        compiler_params=_cparams("parallel"),
        name="final_norm",
    )(x, g)


def _row(v):
    return v.reshape(1, -1).astype(F32)


def _pad_cols(a, cols):
    return jnp.pad(a, ((0, 0), (0, cols - a.shape[1])))


def _row_tile(n, pref):
    t = min(pref, n)
    while n % t:
        t //= 2
    return t


def ab_mixer_layer(x, norm_g, w_in, w_out, lb_logits, hg_onorm_g, rw_mu, rw_w0, rw_w2, rw_a0,
                   rw_a2, rw_g2, rw_k_k, rw_k_a, rw_r_k, rw_gn_g, rw_gn_b, *, batch, seq, layer_j):
    n, d = x.shape
    hg_width = lb_logits.shape[1]
    rw_width = rw_w0.shape[0]
    hg_cols = 4 * hg_width
    rw_cols = w_in.shape[1] - hg_cols
    rw_pad = -(-rw_cols // (2 * LANES)) * (2 * LANES)
    tm = _row_tile(n, 512)
    g = _row(norm_g)
    p_hg = norm_matmul(x, g, w_in[:, :hg_cols].astype(BF16), tm=tm, tn=hg_cols // 2)
    p_rw = norm_matmul(x, g, _pad_cols(w_in[:, hg_cols:], rw_pad).astype(BF16), tm=tm, tn=rw_pad // 2)
    y_a = hgrn2_mix(p_hg, lb_logits.astype(F32), _row(hg_onorm_g), batch=batch, seq=seq,
                    layer_j=layer_j, tt=min(256, seq))
    y_b = rwkv7_mix(p_rw, _pad_cols(_row(rw_mu), rw_pad), _row(rw_w0), rw_w2.astype(BF16),
                    _row(rw_a0), rw_a2.astype(BF16), rw_g2.astype(BF16), _row(rw_k_k),
                    _row(rw_k_a), _row(rw_r_k), _row(rw_gn_g), _row(rw_gn_b),
                    batch=batch, seq=seq)
    return out_proj_residual(y_a, y_b, w_out[:hg_width].astype(BF16), w_out[hg_width:].astype(BF16),
                             x, tm=tm)


def _block_diag(t):
    nblk, gpb, r, c = t.shape
    eye = jnp.eye(gpb, dtype=t.dtype)
    return jnp.einsum('bgrc,gh->bgrhc', t, eye).reshape(nblk, gpb * r, gpb * c)


def s5_mixer_layer(x, norm_g, w_in, a_re, a_im, log_dt, b_re, b_im, c_re, c_im, d_skip, w_glu,
                   *, batch, seq):
    n, d = x.shape
    groups, p_state = a_re.shape
    gpb = S5_GROUPS_PER_BLOCK
    nblk = groups // gpb
    tm = _row_tile(seq, 512)
    tb = seq // tm
    u_tm = norm_matmul(x, _row(norm_g), w_in.astype(BF16), tm=tm, tn=d, out_shape=(seq, batch * d),
                       out_map=lambda i, j: (i % tb, i // tb))
    ab_re, ab_im, bb_re, bb_im = s5_params(
        a_re.reshape(groups, 1, p_state).astype(F32), a_im.reshape(groups, 1, p_state).astype(F32),
        log_dt.reshape(groups, 1, 1).astype(F32),
        jnp.swapaxes(b_re, 1, 2).astype(F32), jnp.swapaxes(b_im, 1, 2).astype(F32))
    m = bb_re.shape[1]
    bb_blk = jnp.concatenate([_block_diag(bb_re.reshape(nblk, gpb, m, p_state)),
                              _block_diag(bb_im.reshape(nblk, gpb, m, p_state))], axis=-1)
    cre_blk = _block_diag(jnp.swapaxes(c_re, 1, 2).reshape(nblk, gpb, p_state, m))
    cim_blk = _block_diag(jnp.swapaxes(c_im, 1, 2).reshape(nblk, gpb, p_state, m))
    y_tm = s5_scan(u_tm.reshape(seq, batch, d), bb_blk.astype(BF16), cre_blk.astype(BF16),
                   cim_blk.astype(BF16), ab_re.reshape(nblk, 1, gpb * p_state),
                   ab_im.reshape(nblk, 1, gpb * p_state), _row(d_skip), lt=min(256, seq))
    return glu_residual(y_tm.reshape(seq, batch * d), w_glu.astype(BF16), x,
                        batch=batch, seq=seq, tm=tm)


def kernel(x, mem, mix_norm_g, xattn_norm_g, ffn_norm_g, mem_norm_g, final_norm_g, ab_w_in, ab_w_out, hg_lb_logits, hg_onorm_g, rw_mu, rw_w0, rw_w2, rw_a0, rw_a2, rw_g2, rw_k_k, rw_k_a, rw_r_k, rw_gn_g, rw_gn_b, c_w_in, s5_a_re, s5_a_im, s5_log_dt, s5_b_re, s5_b_im, s5_c_re, s5_c_im, s5_d, c_w_glu, xa_wq, xa_wkv, xa_wo, ffn_w13, ffn_w2, moe_router, moe_w13, moe_w2):
    batch, seq, d = x.shape
    n_mem = mem.shape[1]
    depth = mix_norm_g.shape[0]
    n = batch * seq
    x = x.reshape(n, d).astype(F32)
    mem2 = mem.reshape(batch * n_mem, d).astype(F32)
    tm = _row_tile(n, 512)
    tm_seq = _row_tile(seq, 512)
    dff = ffn_w2.shape[1]
    tf = dff // 2 if (dff // 2) % LANES == 0 else dff

    for layer in range(depth):
        j = layer // 2
        if layer % 2 == 0:
            x = ab_mixer_layer(x, mix_norm_g[layer], ab_w_in[j], ab_w_out[j], hg_lb_logits,
                               hg_onorm_g[j], rw_mu[j], rw_w0[j], rw_w2[j], rw_a0[j], rw_a2[j],
                               rw_g2[j], rw_k_k[j], rw_k_a[j], rw_r_k[j], rw_gn_g[j], rw_gn_b[j],
                               batch=batch, seq=seq, layer_j=j)
        else:
            x = s5_mixer_layer(x, mix_norm_g[layer], c_w_in[j], s5_a_re[j], s5_a_im[j],
                               s5_log_dt[j], s5_b_re[j], s5_b_im[j], s5_c_re[j], s5_c_im[j],
                               s5_d[j], c_w_glu[j], batch=batch, seq=seq)

        kv = norm_matmul(mem2, _row(mem_norm_g), xa_wkv[layer].astype(BF16),
                         tm=_row_tile(batch * n_mem, 512), tn=d, out_dtype=BF16)
        x = cross_attention_residual(x, _row(xattn_norm_g[layer]), xa_wq[layer].astype(BF16), kv,
                                     xa_wo[layer].astype(BF16), batch=batch, seq=seq, n_mem=n_mem,
                                     tm=tm_seq)

        if layer % 2 == 0:
            x = ffn_residual(x, _row(ffn_norm_g[layer]), ffn_w13[j].astype(BF16),
                             ffn_w2[j].astype(BF16), tm=tm, tf=tf)
        else:
            x = moe_residual(x, _row(ffn_norm_g[layer]), moe_router[j].T.astype(F32),
                             moe_w13[j].astype(BF16), moe_w2[j].astype(BF16),
                             tm=_row_tile(n, MOE_TOKEN_TILE), tf=tf, rb=MOE_ROW_BLOCK)

    return final_norm(x, _row(final_norm_g), tm=tm).reshape(batch, seq, d)
```

```python
import functools
import math

import jax
import jax.numpy as jnp
from jax import lax
from jax.experimental import pallas as pl
from jax.experimental.pallas import tpu as pltpu

F32 = jnp.float32
BF16 = jnp.bfloat16

NORM_EPS = 1e-6
LANES = 128
SUBLANES = 8
VMEM_LIMIT_BYTES = 60 * 1024 * 1024

HG_DK = 128
HG_CHUNK = 64
HG_EXP_CLIP = 60.0
HG_TIME_BLOCK = 512

RW_N = 64
RW_CHUNK = 64
RW_GROUP_HEADS = 4
RW_STREAMS = 4
RW_DECAY_LORA = 32
RW_A_LORA = 32
RW_GATE_LORA = 96
RW_GN_EPS = 64e-5

S5_GROUP = 16
S5_STATE = 64
S5_GROUPS_PER_BLOCK = LANES // S5_GROUP
S5_BLOCK_STATE = S5_GROUPS_PER_BLOCK * S5_STATE

XA_HEADS = 4
MOE_TOKEN_TILE = 1024
MOE_ROW_BLOCK = 128
MOE_RANK_BLOCK = 256

HIGHEST = lax.Precision.HIGHEST


def _cparams(*sem):
    return pltpu.CompilerParams(dimension_semantics=sem, vmem_limit_bytes=VMEM_LIMIT_BYTES)


def _dot(a, b, precision=None):
    return jnp.dot(a, b, preferred_element_type=F32, precision=precision)


def _dot_nt(a, b, precision=None):
    return lax.dot_general(a, b, (((1,), (1,)), ((), ())), preferred_element_type=F32,
                           precision=precision)


def _dot_tn(a, b, precision=None):
    return lax.dot_general(a, b, (((0,), (0,)), ((), ())), preferred_element_type=F32,
                           precision=precision)


def _split_bf16(t, terms):
    parts = []
    for _ in range(terms):
        hi = t.astype(BF16)
        parts.append(hi)
        t = t - hi.astype(F32)
    return parts


def _rms(x, g, eps=NORM_EPS):
    return x * lax.rsqrt(jnp.mean(x * x, axis=-1, keepdims=True) + eps) * g


def _silu(x):
    return x * jax.nn.sigmoid(x)


def _tri_incl(n):
    r = lax.broadcasted_iota(jnp.int32, (n, n), 0)
    c = lax.broadcasted_iota(jnp.int32, (n, n), 1)
    return jnp.where(r >= c, 1.0, 0.0).astype(F32)


def _norm_matmul_kernel(x_ref, g_ref, w_ref, o_ref, h_ref):
    @pl.when(pl.program_id(1) == 0)
    def _():
        h_ref[...] = _rms(x_ref[...], g_ref[...]).astype(BF16)

    o_ref[...] = _dot(h_ref[...], w_ref[...]).astype(o_ref.dtype)


def norm_matmul(x, g, w, *, tm, tn, out_dtype=F32, out_shape=None, out_map=None):
    m, k = x.shape
    n = w.shape[1]
    assert m % tm == 0 and n % tn == 0
    if out_shape is None:
        out_shape = (m, n)
    if out_map is None:
        out_map = lambda i, j: (i, j)
    return pl.pallas_call(
        _norm_matmul_kernel,
        grid=(m // tm, n // tn),
        in_specs=[pl.BlockSpec((tm, k), lambda i, j: (i, 0)),
                  pl.BlockSpec((1, k), lambda i, j: (0, 0)),
                  pl.BlockSpec((k, tn), lambda i, j: (0, j))],
        out_specs=pl.BlockSpec((tm, tn), out_map),
        out_shape=jax.ShapeDtypeStruct(out_shape, out_dtype),
        scratch_shapes=[pltpu.VMEM((tm, k), BF16)],
        compiler_params=_cparams("parallel", "arbitrary"),
        name="norm_matmul",
    )(x, g, w)


def _norm_matmul2_kernel(x_ref, g_ref, wa_ref, wb_ref, oa_ref, ob_ref):
    h = _rms(x_ref[...], g_ref[...]).astype(BF16)
    oa_ref[...] = _dot(h, wa_ref[...])
    ob_ref[...] = _dot(h, wb_ref[...])


def norm_matmul2(x, g, wa, wb, *, tm):
    m, k = x.shape
    na, nb = wa.shape[1], wb.shape[1]
    return pl.pallas_call(
        _norm_matmul2_kernel,
        grid=(m // tm,),
        in_specs=[pl.BlockSpec((tm, k), lambda i: (i, 0)),
                  pl.BlockSpec((1, k), lambda i: (0, 0)),
                  pl.BlockSpec((k, na), lambda i: (0, 0)),
                  pl.BlockSpec((k, nb), lambda i: (0, 0))],
        out_specs=(pl.BlockSpec((tm, na), lambda i: (i, 0)), pl.BlockSpec((tm, nb), lambda i: (i, 0))),
        out_shape=(jax.ShapeDtypeStruct((m, na), F32), jax.ShapeDtypeStruct((m, nb), F32)),
        compiler_params=_cparams("parallel"),
        name="norm_matmul2",
    )(x, g, wa, wb)


def _hgrn2_kernel(q_ref, f_ref, i_ref, g_ref, lbl_ref, og_ref, o_ref, st_ref, *, layer_j, n_chunks):
    c = HG_CHUNK
    halves = [c >> (i + 1) for i in range(c.bit_length() - 1)]

    @pl.when(pl.program_id(2) == 0)
    def _():
        st_ref[...] = jnp.zeros_like(st_ref)

    logits = lbl_ref[...]
    e = jnp.exp(logits - jnp.max(logits, axis=0, keepdims=True))
    p = e / jnp.sum(e, axis=0, keepdims=True)
    lb = jnp.zeros((1, p.shape[1]), F32)
    for r in range(1, layer_j + 1):
        lb = lb + p[r:r + 1, :]

    og = og_ref[...]
    rr = lax.broadcasted_iota(jnp.int32, (c, c), 0)
    cc = lax.broadcasted_iota(jnp.int32, (c, c), 1)
    row = lax.broadcasted_iota(jnp.int32, (c, HG_DK), 0)

    sums = [rr >= cc]
    pair_masks = []
    for hb in halves:
        blk = 2 * hb
        sums.append(cc <= jnp.bitwise_and(rr, -blk) + (hb - 1))
        same = jnp.bitwise_and(rr, -blk) == jnp.bitwise_and(cc, -blk)
        pair_masks.append(jnp.where(same, jnp.bitwise_and(rr, hb) - jnp.bitwise_and(cc, hb), 0) > 0)
    prefix = jnp.concatenate([jnp.where(m, 1.0, 0.0) for m in sums], axis=0).astype(BF16)

    chunks = range(n_chunks)
    rows = [slice(ci * c, (ci + 1) * c) for ci in chunks]
    qs, ks, ivs, cums = [], [], [], []
    for ci in chunks:
        fr = f_ref[rows[ci], :]
        log_sig = jnp.minimum(fr, 0.0) - jnp.log1p(jnp.exp(-jnp.abs(fr)))
        log_f = log_sig + jnp.log1p(lb * jnp.exp(jnp.minimum(-fr, HG_EXP_CLIP)))
        cums.append(sum(_dot(prefix, part) for part in _split_bf16(log_f, 3)))
        qs.append(_silu(q_ref[rows[ci], :]))
        ks.append((1.0 - lb) * jax.nn.sigmoid(-fr))
        ivs.append(i_ref[rows[ci], :])
    ls = [cm[:c] for cm in cums]

    a_mats = [jnp.zeros((c, c), F32) for _ in chunks]
    for li, hb in enumerate(halves):
        upper = jnp.bitwise_and(row, hb) != 0
        for ci in chunks:
            d = ls[ci] - cums[ci][(li + 1) * c:(li + 2) * c]
            ex = jnp.exp(jnp.where(upper, d, -d))
            prod = _dot_nt((qs[ci] * ex).astype(BF16), (ks[ci] * ex).astype(BF16))
            a_mats[ci] = a_mats[ci] + jnp.where(pair_masks[li], prod, 0.0)

    os_, kvs, qes, ends = [], [], [], []
    for ci in chunks:
        q, k, l, iv = qs[ci], ks[ci], ls[ci], ivs[ci]
        iv_b = iv.astype(BF16)
        l_end = l[c - 1:c, :]
        os_.append(_dot(a_mats[ci].astype(BF16), iv_b) + jnp.sum(q * k, axis=-1, keepdims=True) * iv)
        qes.append((q * jnp.exp(l)).astype(BF16))
        kvs.append(_dot_tn(iv_b, (k * jnp.exp(l_end - l)).astype(BF16)))
        ends.append(jnp.exp(l_end))

    st = st_ref[...]
    for ci in range(n_chunks):
        rs = slice(ci * c, (ci + 1) * c)
        o = os_[ci] + _dot_nt(qes[ci], st.astype(BF16))
        st = st * ends[ci] + kvs[ci]
        o_ref[rs, :] = (_rms(o, og) * _silu(g_ref[rs, :])).astype(o_ref.dtype)
    st_ref[...] = st


def hgrn2_mix(p_hg, lb_logits, onorm_g, *, batch, seq, layer_j, tt=256):
    n, cols = p_hg.shape
    width = cols // 4
    heads = width // HG_DK
    assert seq % tt == 0 and tt % HG_CHUNK == 0
    tb = seq // tt
    n_ab = lb_logits.shape[0]

    def col_spec(part):
        return pl.BlockSpec((tt, HG_DK), lambda b, h, t, part=part: (b * tb + t, part * heads + h))

    kern = functools.partial(_hgrn2_kernel, layer_j=layer_j, n_chunks=tt // HG_CHUNK)
    return pl.pallas_call(
        kern,
        grid=(batch, heads, tb),
        in_specs=[col_spec(0), col_spec(1), col_spec(2), col_spec(3),
                  pl.BlockSpec((n_ab, HG_DK), lambda b, h, t: (0, h)),
                  pl.BlockSpec((1, HG_DK), lambda b, h, t: (0, 0))],
        out_specs=pl.BlockSpec((tt, HG_DK), lambda b, h, t: (b * tb + t, h)),
        out_shape=jax.ShapeDtypeStruct((n, width), BF16),
        scratch_shapes=[pltpu.VMEM((HG_DK, HG_DK), F32)],
        compiler_params=_cparams("parallel", "parallel", "arbitrary"),
        name="hgrn2_mix",
    )(p_hg, p_hg, p_hg, p_hg, lb_logits, onorm_g)


def _round_robin(chains):
    chains = list(chains)
    while chains:
        alive = []
        for ch in chains:
            try:
                next(ch)
                alive.append(ch)
            except StopIteration:
                pass
        chains = alive
        yield


def _rwkv7_stream(p_ref, mu_ref, w0_ref, w2_ref, a0_ref, a2_ref, g2_ref, kk_ref, ka_ref,
                  rk_ref, gng_ref, gnb_ref, ones_ref, o_ref, prev_ref, st_ref, *, heads):
    c = RW_CHUNK
    n = RW_N
    assert c == n
    width = heads * n
    gh = RW_GROUP_HEADS
    gw = gh * n
    gr = gh * c
    shift = c.bit_length() - 1

    @pl.when(pl.program_id(1) == 0)
    def _():
        prev_ref[...] = jnp.zeros_like(prev_ref)
        st_ref[...] = jnp.zeros_like(st_ref)

    p = p_ref[...]
    row = lax.broadcasted_iota(jnp.int32, p.shape, 0)
    shifted = jnp.where(row == 0, prev_ref[0:1, :], pltpu.roll(p, 1, axis=0))
    prev_ref[0:1, :] = p[c - 1:c, :]
    p = p + (shifted - p) * mu_ref[...]

    r = p[:, 0:width]
    k = p[:, width:2 * width]
    v = p[:, 2 * width:3 * width]
    off = 3 * width
    wd = p[:, off:off + RW_DECAY_LORA]
    ad = p[:, off + RW_DECAY_LORA:off + RW_DECAY_LORA + RW_A_LORA]
    gd = p[:, off + RW_DECAY_LORA + RW_A_LORA:off + RW_DECAY_LORA + RW_A_LORA + RW_GATE_LORA]

    wz = w0_ref[...] + _dot(jnp.tanh(wd).astype(BF16), w2_ref[...])
    w = jnp.minimum(wz, 0.0) - jnp.log1p(jnp.exp(-jnp.abs(wz))) - 0.5
    lw = -jnp.exp(w)
    a = jax.nn.sigmoid(a0_ref[...] + _dot(ad.astype(BF16), a2_ref[...]))
    g = _dot(jax.nn.sigmoid(gd).astype(BF16), g2_ref[...])
    k2 = k * (1.0 + (a - 1.0) * ka_ref[...])

    ones = ones_ref[...]

    def head_sum(t):
        return sum(_dot(part, ones) for part in _split_bf16(t, 2))

    kk = k * kk_ref[...]
    kk = kk * lax.rsqrt(jnp.maximum(head_sum(kk * kk), 1e-24))
    b = kk * a

    tri = _tri_incl(c).astype(BF16)
    cum = sum(_dot(tri, part) for part in _split_bf16(lw, 3))
    cum_end = cum[c - 1:c, :]
    e_inv = jnp.exp(-cum)
    e_fin = jnp.exp(cum_end - cum)
    e_end = jnp.exp(cum_end)
    at = -kk * jnp.exp(cum - lw)
    rt = r * jnp.exp(cum)
    bt = b * e_inv
    kt = k2 * e_inv
    bf = b * e_fin
    kf = k2 * e_fin

    rb = lax.broadcasted_iota(jnp.int32, (gr, gw), 0)
    cb = lax.broadcasted_iota(jnp.int32, (gr, gw), 1)
    same = jnp.right_shift(rb, shift) == jnp.right_shift(cb, shift)
    t_in = jnp.bitwise_and(rb, c - 1)
    s_in = jnp.bitwise_and(cb, c - 1)
    strict = t_in > s_in
    incl = t_in >= s_in
    eye = jnp.where(rb == cb, 1.0, 0.0).astype(F32)

    def bd(t):
        return jnp.where(same, jnp.concatenate([t] * gh, axis=0), 0.0).astype(BF16)

    ngroups = heads // gh
    outs = [None] * ngroups

    def group_chain(gi):
        gs = slice(gi * gw, (gi + 1) * gw)
        art = jnp.concatenate([bd(at[:, gs]), bd(rt[:, gs])], axis=0)
        v_bd = bd(v[:, gs])
        g_b = _dot_nt(art, bd(bt[:, gs]))
        g_k = _dot_nt(art, bd(kt[:, gs]))
        yield
        a_ab = jnp.where(strict, g_b[:gr], 0.0)
        a_ak = jnp.where(strict, g_k[:gr], 0.0)
        a_rb = jnp.where(incl, g_b[gr:], 0.0)
        a_rk = jnp.where(incl, g_k[gr:], 0.0)

        tinv = eye + a_ab
        x = a_ab
        for _ in range(shift - 1):
            xb = x.astype(BF16)
            x = _dot(xb, xb)
            yield
            tinv = tinv + _dot(tinv.astype(BF16), x.astype(BF16))
            yield

        st = st_ref[gi]
        arm = _dot_nt(art, st.astype(BF16))
        akv = _dot(jnp.concatenate([a_ak, a_rk], axis=0).astype(BF16), v_bd)
        yield
        u_b = _dot(tinv.astype(BF16), (arm[:gr] + akv[:gr]).astype(BF16)).astype(BF16)
        yield
        o_bd = arm[gr:] + akv[gr:] + _dot(a_rb.astype(BF16), u_b)
        bk = jnp.concatenate([bd(bf[:, gs]), bd(kf[:, gs])], axis=0)
        st_ref[gi] = st * e_end[:, gs] + _dot_tn(jnp.concatenate([u_b, v_bd], axis=0), bk)
        o_g = o_bd[0:c]
        for hh in range(1, gh):
            o_g = o_g + o_bd[hh * c:(hh + 1) * c]
        outs[gi] = o_g

    yield
    yield from _round_robin([group_chain(gi) for gi in range(ngroups)])

    o = jnp.concatenate(outs, axis=-1)
    mean = head_sum(o) * (1.0 / n)
    dev = o - mean
    var = head_sum(dev * dev) * (1.0 / n)
    on = dev * lax.rsqrt(var + RW_GN_EPS) * gng_ref[...] + gnb_ref[...]
    bonus = head_sum(r * k2 * rk_ref[...]) * v
    o_ref[...] = ((on + bonus) * g).astype(o_ref.dtype)


def _rwkv7_kernel(p_ref, *refs, heads, streams):
    *param_refs, o_ref, prev_ref, st_ref = refs
    chains = [_rwkv7_stream(p_ref.at[bi], *param_refs, o_ref.at[bi], prev_ref.at[bi],
                            st_ref.at[bi], heads=heads) for bi in range(streams)]
    for _ in _round_robin(chains):
        pass


def rwkv7_mix(p_rw, mu, w0, w2, a0, a2, g2, k_k, k_a, r_k, gn_g, gn_b, *, batch, seq):
    n_tok, cols = p_rw.shape
    width = w0.shape[1]
    heads = width // RW_N
    assert seq % RW_CHUNK == 0 and heads % RW_GROUP_HEADS == 0
    cb = seq // RW_CHUNK
    head_id = jnp.arange(width, dtype=jnp.int32) // RW_N
    ones = (head_id[:, None] == head_id[None, :]).astype(BF16)
    full = lambda a: pl.BlockSpec(a.shape, lambda b, t: (0,) * a.ndim)
    params = (mu, w0, w2, a0, a2, g2, k_k, k_a, r_k, gn_g, gn_b, ones)
    gdim = RW_GROUP_HEADS * RW_N
    nb = RW_STREAMS if batch % RW_STREAMS == 0 else 1
    out = pl.pallas_call(
        functools.partial(_rwkv7_kernel, heads=heads, streams=nb),
        grid=(batch // nb, cb),
        in_specs=[pl.BlockSpec((nb, RW_CHUNK, cols), lambda b, t: (b, t, 0))]
                 + [full(a) for a in params],
        out_specs=pl.BlockSpec((nb, RW_CHUNK, width), lambda b, t: (b, t, 0)),
        out_shape=jax.ShapeDtypeStruct((batch, seq, width), BF16),
        scratch_shapes=[pltpu.VMEM((nb, SUBLANES, cols), F32),
                        pltpu.VMEM((nb, heads // RW_GROUP_HEADS, gdim, gdim), F32)],
        compiler_params=_cparams("parallel", "arbitrary"),
        name="rwkv7_mix",
    )(p_rw.reshape(batch, seq, cols), *params)
    return out.reshape(n_tok, width)


def _out_proj_kernel(ya_ref, yb_ref, wa_ref, wb_ref, x_ref, o_ref):
    o_ref[...] = x_ref[...] + _dot(ya_ref[...], wa_ref[...]) + _dot(yb_ref[...], wb_ref[...])


def out_proj_residual(ya, yb, wa, wb, x, *, tm):
    n, d = x.shape
    ka, kb = ya.shape[1], yb.shape[1]
    return pl.pallas_call(
        _out_proj_kernel,
        grid=(n // tm,),
        in_specs=[pl.BlockSpec((tm, ka), lambda i: (i, 0)),
                  pl.BlockSpec((tm, kb), lambda i: (i, 0)),
                  pl.BlockSpec((ka, d), lambda i: (0, 0)),
                  pl.BlockSpec((kb, d), lambda i: (0, 0)),
                  pl.BlockSpec((tm, d), lambda i: (i, 0))],
        out_specs=pl.BlockSpec((tm, d), lambda i: (i, 0)),
        out_shape=jax.ShapeDtypeStruct((n, d), F32),
        compiler_params=_cparams("parallel"),
        name="out_proj_residual",
    )(ya, yb, wa, wb, x)


def _s5_param_kernel(are_ref, aim_ref, ldt_ref, bre_ref, bim_ref,
                     abre_ref, abim_ref, bbre_ref, bbim_ref):
    lam_re = jnp.minimum(are_ref[...], -1e-4)
    lam_im = aim_ref[...]
    dt = jnp.exp(ldt_ref[...])
    mag = jnp.exp(lam_re * dt)
    ab_re = mag * jnp.cos(lam_im * dt)
    ab_im = mag * jnp.sin(lam_im * dt)
    den = lam_re * lam_re + lam_im * lam_im
    coef_re = ((ab_re - 1.0) * lam_re + ab_im * lam_im) / den
    coef_im = (ab_im * lam_re - (ab_re - 1.0) * lam_im) / den
    abre_ref[...] = ab_re
    abim_ref[...] = ab_im
    b_re = bre_ref[...]
    b_im = bim_ref[...]
    bbre_ref[...] = coef_re * b_re - coef_im * b_im
    bbim_ref[...] = coef_re * b_im + coef_im * b_re


def s5_params(a_re, a_im, log_dt, b_re_t, b_im_t):
    g, _, p = a_re.shape
    m = b_re_t.shape[1]
    return pl.pallas_call(
        _s5_param_kernel,
        out_shape=(jax.ShapeDtypeStruct((g, 1, p), F32), jax.ShapeDtypeStruct((g, 1, p), F32),
                   jax.ShapeDtypeStruct((g, m, p), F32), jax.ShapeDtypeStruct((g, m, p), F32)),
        name="s5_params",
    )(a_re, a_im, log_dt, b_re_t, b_im_t)


def _gelu_tanh(y):
    return 0.5 * y * (1.0 + jnp.tanh(math.sqrt(2.0 / math.pi) * (y + 0.044715 * (y * y * y))))


def _s5_scan_kernel(u_ref, bb_ref, cre_ref, cim_ref, are_ref, aim_ref, d_ref, o_ref,
                    xs_ref, st_ref, *, lt, bsz):
    ns = S5_BLOCK_STATE

    @pl.when(pl.program_id(1) == 0)
    def _():
        st_ref[...] = jnp.zeros_like(st_ref)

    u2 = u_ref[...].reshape(lt * bsz, LANES)
    xs_ref[...] = _dot(u2.astype(BF16), bb_ref[...])
    a_re = jnp.broadcast_to(are_ref[...], (bsz, ns))
    a_im = jnp.broadcast_to(aim_ref[...], (bsz, ns))

    def step(t, carry):
        x_re, x_im = carry
        r0 = pl.multiple_of(t * bsz, bsz)
        n_re = a_re * x_re - a_im * x_im + xs_ref[pl.ds(r0, bsz), 0:ns]
        n_im = a_re * x_im + a_im * x_re + xs_ref[pl.ds(r0, bsz), ns:2 * ns]
        xs_ref[pl.ds(r0, bsz), 0:ns] = n_re
        xs_ref[pl.ds(r0, bsz), ns:2 * ns] = n_im
        return n_re, n_im

    x_re, x_im = lax.fori_loop(0, lt, step, (st_ref[0], st_ref[1]), unroll=8)
    st_ref[0] = x_re
    st_ref[1] = x_im

    y = (_dot(xs_ref[:, 0:ns].astype(BF16), cre_ref[...])
         - _dot(xs_ref[:, ns:2 * ns].astype(BF16), cim_ref[...]))
    y = y + d_ref[...] * u2
    o_ref[...] = _gelu_tanh(y).reshape(lt, bsz, LANES)


def s5_scan(u_tm, bb_blk, cre_blk, cim_blk, ab_re, ab_im, d, *, lt=256):
    seq, bsz, dm = u_tm.shape
    nblk = dm // LANES
    ns = S5_BLOCK_STATE
    assert seq % lt == 0
    return pl.pallas_call(
        functools.partial(_s5_scan_kernel, lt=lt, bsz=bsz),
        grid=(nblk, seq // lt),
        in_specs=[pl.BlockSpec((lt, bsz, LANES), lambda c, t: (t, 0, c)),
                  pl.BlockSpec((None, LANES, 2 * ns), lambda c, t: (c, 0, 0)),
                  pl.BlockSpec((None, ns, LANES), lambda c, t: (c, 0, 0)),
                  pl.BlockSpec((None, ns, LANES), lambda c, t: (c, 0, 0)),
                  pl.BlockSpec((None, 1, ns), lambda c, t: (c, 0, 0)),
                  pl.BlockSpec((None, 1, ns), lambda c, t: (c, 0, 0)),
                  pl.BlockSpec((1, LANES), lambda c, t: (0, c))],
        out_specs=pl.BlockSpec((lt, bsz, LANES), lambda c, t: (t, 0, c)),
        out_shape=jax.ShapeDtypeStruct((seq, bsz, dm), F32),
        scratch_shapes=[pltpu.VMEM((lt * bsz, 2 * ns), F32),
                        pltpu.VMEM((2, bsz, ns), F32)],
        compiler_params=_cparams("parallel", "arbitrary"),
        name="s5_scan",
    )(u_tm, bb_blk, cre_blk, cim_blk, ab_re, ab_im, d)


def _glu_kernel(y_ref, w_ref, x_ref, o_ref):
    d = x_ref.shape[1]
    z = _dot(y_ref[...].astype(BF16), w_ref[...])
    o_ref[...] = x_ref[...] + z[:, :d] * jax.nn.sigmoid(z[:, d:])


def glu_residual(y_tm2d, w, x, *, batch, seq, tm):
    n, d = x.shape
    tb = seq // tm
    return pl.pallas_call(
        _glu_kernel,
        grid=(batch, tb),
        in_specs=[pl.BlockSpec((tm, d), lambda b, t: (t, b)),
                  pl.BlockSpec(w.shape, lambda b, t: (0, 0)),
                  pl.BlockSpec((tm, d), lambda b, t: (b * tb + t, 0))],
        out_specs=pl.BlockSpec((tm, d), lambda b, t: (b * tb + t, 0)),
        out_shape=jax.ShapeDtypeStruct((n, d), F32),
        compiler_params=_cparams("parallel", "parallel"),
        name="glu_residual",
    )(y_tm2d, w, x)


def _xattn_kernel(x_ref, g_ref, wq_ref, k_ref, v_ref, wo_ref, o_ref, *, heads):
    x = x_ref[...]
    d = x.shape[1]
    hd = d // heads
    h = _rms(x, g_ref[...]).astype(BF16)
    q = _dot(h, wq_ref[...]).astype(BF16)
    outs = []
    for i in range(heads):
        sl = slice(i * hd, (i + 1) * hd)
        s = _dot_nt(q[:, sl], k_ref[:, sl]) * (hd ** -0.5)
        s = s - jnp.max(s, axis=-1, keepdims=True)
        e = jnp.exp(s)
        p = e / jnp.sum(e, axis=-1, keepdims=True)
        outs.append(_dot(p.astype(BF16), v_ref[:, sl]))
    o = jnp.concatenate(outs, axis=-1).astype(BF16)
    o_ref[...] = x + _dot(o, wo_ref[...])


def cross_attention_residual(x, g, wq, kv, wo, *, batch, seq, n_mem, tm):
    n, d = x.shape
    tb = seq // tm
    return pl.pallas_call(
        functools.partial(_xattn_kernel, heads=XA_HEADS),
        grid=(batch, tb),
        in_specs=[pl.BlockSpec((tm, d), lambda b, t: (b * tb + t, 0)),
                  pl.BlockSpec((1, d), lambda b, t: (0, 0)),
                  pl.BlockSpec((d, d), lambda b, t: (0, 0)),
                  pl.BlockSpec((n_mem, d), lambda b, t: (b, 0)),
                  pl.BlockSpec((n_mem, d), lambda b, t: (b, 1)),
                  pl.BlockSpec((d, d), lambda b, t: (0, 0))],
        out_specs=pl.BlockSpec((tm, d), lambda b, t: (b * tb + t, 0)),
        out_shape=jax.ShapeDtypeStruct((n, d), F32),
        compiler_params=_cparams("parallel", "parallel"),
        name="cross_attention",
    )(x, g, wq, kv, kv, wo)


def _ffn_kernel(x_ref, g_ref, wg_ref, wu_ref, w2_ref, o_ref, h_ref, acc_ref):
    c = pl.program_id(1)

    @pl.when(c == 0)
    def _():
        h_ref[...] = _rms(x_ref[...], g_ref[...]).astype(BF16)
        acc_ref[...] = jnp.zeros_like(acc_ref)

    h = h_ref[...]
    a = _silu(_dot(h, wg_ref[...])) * _dot(h, wu_ref[...])
    acc_ref[...] += _dot(a.astype(BF16), w2_ref[...])

    @pl.when(c == pl.num_programs(1) - 1)
    def _():
        o_ref[...] = x_ref[...] + acc_ref[...]


def ffn_residual(x, g, w13, w2, *, layer, tm, tf):
    n, d = x.shape
    dff = w2.shape[1]
    nc = dff // tf
    return pl.pallas_call(
        _ffn_kernel,
        grid=(n // tm, nc),
        in_specs=[pl.BlockSpec((tm, d), lambda i, c: (i, 0)),
                  pl.BlockSpec((1, d), lambda i, c: (0, 0)),
                  pl.BlockSpec((None, d, tf), lambda i, c: (layer, 0, c)),
                  pl.BlockSpec((None, d, tf), lambda i, c: (layer, 0, nc + c)),
                  pl.BlockSpec((None, tf, d), lambda i, c: (layer, c, 0))],
        out_specs=pl.BlockSpec((tm, d), lambda i, c: (i, 0)),
        out_shape=jax.ShapeDtypeStruct((n, d), F32),
        scratch_shapes=[pltpu.VMEM((tm, d), BF16), pltpu.VMEM((tm, d), F32)],
        compiler_params=_cparams("parallel", "arbitrary"),
        name="ffn_residual",
    )(x, g, w13, w13, w2)


def _moe_route(logits_t, n_experts, rb):
    m = logits_t.shape[1]
    sub = lax.broadcasted_iota(jnp.int32, logits_t.shape, 0).astype(F32)
    neg = -jnp.inf
    m1 = jnp.max(logits_t, axis=0, keepdims=True)
    i1 = jnp.min(jnp.where(logits_t == m1, sub, float(n_experts)), axis=0, keepdims=True)
    first = sub == i1
    rest = jnp.where(first, neg, logits_t)
    m2 = jnp.max(rest, axis=0, keepdims=True)
    i2 = jnp.min(jnp.where(rest == m2, sub, float(n_experts)), axis=0, keepdims=True)
    second = sub == i2
    e2 = jnp.exp(m2 - m1)
    den = 1.0 + e2
    gate0 = 1.0 / den
    gate1 = e2 / den

    first_f = jnp.where(first, 1.0, 0.0)
    second_f = jnp.where(second, 1.0, 0.0)
    both = jnp.concatenate([first_f, second_f], axis=0)
    w = min(m, MOE_RANK_BLOCK)
    t_src = lax.broadcasted_iota(jnp.int32, (w, w), 0)
    t_dst = lax.broadcasted_iota(jnp.int32, (w, w), 1)
    before = jnp.where(t_src < t_dst, 1.0, 0.0).astype(BF16)
    cnt = jnp.zeros((2 * n_experts, 1), F32)
    ranks = []
    for kb in range(m // w):
        blk = both[:, kb * w:(kb + 1) * w]
        ranks.append(_dot(blk.astype(BF16), before) + cnt)
        cnt = cnt + jnp.sum(blk, axis=1, keepdims=True)
    ranks = jnp.concatenate(ranks, axis=1)
    cnt0 = cnt[:n_experts]
    cnt_e = cnt0 + cnt[n_experts:]
    padded = jnp.floor((cnt_e + (rb - 1)) * (1.0 / rb)) * rb
    sub_col = lax.broadcasted_iota(jnp.int32, (n_experts, 1), 0)
    start = jnp.zeros((n_experts, 1), F32)
    for ee in range(n_experts - 1):
        start = start + jnp.where(sub_col > ee, padded[ee:ee + 1, :], 0.0)
    dest0 = jnp.sum(first_f * (start + ranks[:n_experts]), axis=0, keepdims=True)
    dest1 = jnp.sum(second_f * (start + cnt0 + ranks[n_experts:]), axis=0, keepdims=True)
    return dest0, dest1, gate0, gate1, start, padded


def _moe_kernel(x_ref, g_ref, rt_ref, wg_ref, wu_ref, w2_ref, o_ref, xs_ref, gw_ref, y_ref, meta_ref,
                *, n_experts, rb):
    c = pl.program_id(1)
    e = pl.program_id(2)
    n_rows = xs_ref.shape[0]
    tm = x_ref.shape[0]

    @pl.when((c == 0) & (e == 0))
    def _route():
        x = x_ref[...]
        hf = _rms(x, g_ref[...])
        o_ref[...] = x
        y_ref[...] = jnp.zeros_like(y_ref)
        h_b = hf.astype(BF16)
        logits_t = _dot_nt(rt_ref[...], hf, precision=HIGHEST)
        dest0, dest1, gate0, gate1, start, padded = _moe_route(logits_t, n_experts, rb)
        for ee in range(n_experts):
            meta_ref[ee] = jnp.sum(start[ee:ee + 1, :]).astype(jnp.int32)
            meta_ref[n_experts + ee] = jnp.sum(padded[ee:ee + 1, :] * (1.0 / rb)).astype(jnp.int32)

        def build(j, carry):
            r0 = pl.multiple_of(j * rb, rb)
            rows = (lax.broadcasted_iota(jnp.int32, (rb, tm), 0) + r0).astype(F32)
            hit0 = rows == dest0
            hit1 = rows == dest1
            onehot = jnp.where(hit0 | hit1, 1.0, 0.0).astype(BF16)
            xs_ref[pl.ds(r0, rb), :] = _dot(onehot, h_b).astype(BF16)
            gw_ref[pl.ds(r0, rb), :] = (jnp.where(hit0, gate0, 0.0)
                                        + jnp.where(hit1, gate1, 0.0)).astype(BF16)
            return carry

        lax.fori_loop(0, n_rows // rb, build, 0)

    seg_start = meta_ref[e]
    seg_blocks = meta_ref[n_experts + e]

    def expert_rows(r0, rows):
        xb = xs_ref[pl.ds(r0, rows), :]
        a = _silu(_dot(xb, wg_ref[...])) * _dot(xb, wu_ref[...])
        y_ref[pl.ds(r0, rows), :] = _dot(a.astype(BF16), w2_ref[...]).astype(BF16)

    def expert_pair(j, carry):
        expert_rows(pl.multiple_of(seg_start + j * (2 * rb), rb), 2 * rb)
        return carry

    lax.fori_loop(0, seg_blocks // 2, expert_pair, 0)

    @pl.when(seg_blocks % 2 == 1)
    def _odd_block():
        expert_rows(pl.multiple_of(seg_start + (seg_blocks - 1) * rb, rb), rb)

    @pl.when(e == pl.num_programs(2) - 1)
    def _combine():
        o_ref[...] += _dot_tn(gw_ref[...], y_ref[...])


def moe_residual(x, g, router_t, w13, w2, *, layer, tm, tf, rb):
    n, d = x.shape
    _, n_experts, dff, _ = w2.shape
    nc = dff // tf
    n_rows = 2 * tm + n_experts * rb
    return pl.pallas_call(
        functools.partial(_moe_kernel, n_experts=n_experts, rb=rb),
        grid=(n // tm, nc, n_experts),
        in_specs=[pl.BlockSpec((tm, d), lambda i, c, e: (i, 0), pipeline_mode=pl.Buffered(1)),
                  pl.BlockSpec((1, d), lambda i, c, e: (0, 0)),
                  pl.BlockSpec((n_experts, d), lambda i, c, e: (0, 0)),
                  pl.BlockSpec((None, None, d, tf), lambda i, c, e: (layer, e, 0, c)),
                  pl.BlockSpec((None, None, d, tf), lambda i, c, e: (layer, e, 0, nc + c)),
                  pl.BlockSpec((None, None, tf, d), lambda i, c, e: (layer, e, c, 0))],
        out_specs=pl.BlockSpec((tm, d), lambda i, c, e: (i, 0)),
        out_shape=jax.ShapeDtypeStruct((n, d), F32),
        scratch_shapes=[pltpu.VMEM((n_rows, d), BF16), pltpu.VMEM((n_rows, tm), BF16),
                        pltpu.VMEM((n_rows, d), BF16), pltpu.SMEM((2 * n_experts,), jnp.int32)],
        compiler_params=_cparams("parallel", "arbitrary", "arbitrary"),
        name="moe_residual",
    )(x, g, router_t, w13, w13, w2)


def _final_norm_kernel(x_ref, g_ref, o_ref):
    o_ref[...] = _rms(x_ref[...], g_ref[...])


def final_norm(x, g, *, tm):
    n, d = x.shape
    return pl.pallas_call(
        _final_norm_kernel,
        grid=(n // tm,),
        in_specs=[pl.BlockSpec((tm, d), lambda i: (i, 0)), pl.BlockSpec((1, d), lambda i: (0, 0))],
        out_specs=pl.BlockSpec((tm, d), lambda i: (i, 0)),
        out_shape=jax.ShapeDtypeStruct((n, d), F32),
        compiler_params=_cparams("parallel"),
        name="final_norm",
    )(x, g)


def _row(v):
    return v.reshape(1, -1).astype(F32)


def _pad_cols(a, cols):
    return jnp.pad(a, ((0, 0), (0, cols - a.shape[1])))


def _row_tile(n, pref):
    t = min(pref, n)
    while n % t:
        t //= 2
    return t


def ab_mixer_layer(x, norm_g, w_in, w_out, lb_logits, hg_onorm_g, rw_mu, rw_w0, rw_w2, rw_a0,
                   rw_a2, rw_g2, rw_k_k, rw_k_a, rw_r_k, rw_gn_g, rw_gn_b, *, batch, seq, layer_j):
    n, d = x.shape
    hg_width = lb_logits.shape[1]
    rw_width = rw_w0.shape[0]
    hg_cols = 4 * hg_width
    rw_cols = w_in.shape[1] - hg_cols
    rw_pad = -(-rw_cols // LANES) * LANES
    tm = _row_tile(n, 512)
    g = _row(norm_g)
    p_hg, p_rw = norm_matmul2(x, g, w_in[:, :hg_cols].astype(BF16),
                              _pad_cols(w_in[:, hg_cols:], rw_pad).astype(BF16), tm=tm)
    y_a = hgrn2_mix(p_hg, lb_logits.astype(F32), _row(hg_onorm_g), batch=batch, seq=seq,
                    layer_j=layer_j, tt=min(HG_TIME_BLOCK, seq))
    y_b = rwkv7_mix(p_rw, _pad_cols(_row(rw_mu), rw_pad), _row(rw_w0), rw_w2.astype(BF16),
                    _row(rw_a0), rw_a2.astype(BF16), rw_g2.astype(BF16), _row(rw_k_k),
                    _row(rw_k_a), _row(rw_r_k), _row(rw_gn_g), _row(rw_gn_b),
                    batch=batch, seq=seq)
    return out_proj_residual(y_a, y_b, w_out[:hg_width].astype(BF16), w_out[hg_width:].astype(BF16),
                             x, tm=tm)


def _block_diag(t):
    nblk, gpb, r, c = t.shape
    eye = jnp.eye(gpb, dtype=t.dtype)
    return jnp.einsum('bgrc,gh->bgrhc', t, eye).reshape(nblk, gpb * r, gpb * c)


def s5_mixer_layer(x, norm_g, w_in, a_re, a_im, log_dt, b_re, b_im, c_re, c_im, d_skip, w_glu,
                   *, batch, seq):
    n, d = x.shape
    groups, p_state = a_re.shape
    gpb = S5_GROUPS_PER_BLOCK
    nblk = groups // gpb
    tm = _row_tile(seq, 512)
    tb = seq // tm
    u_tm = norm_matmul(x, _row(norm_g), w_in.astype(BF16), tm=tm, tn=d, out_shape=(seq, batch * d),
                       out_map=lambda i, j: (i % tb, i // tb))
    ab_re, ab_im, bb_re, bb_im = s5_params(
        a_re.reshape(groups, 1, p_state).astype(F32), a_im.reshape(groups, 1, p_state).astype(F32),
        log_dt.reshape(groups, 1, 1).astype(F32),
        jnp.swapaxes(b_re, 1, 2).astype(F32), jnp.swapaxes(b_im, 1, 2).astype(F32))
    m = bb_re.shape[1]
    bb_blk = jnp.concatenate([_block_diag(bb_re.reshape(nblk, gpb, m, p_state)),
                              _block_diag(bb_im.reshape(nblk, gpb, m, p_state))], axis=-1)
    cre_blk = _block_diag(jnp.swapaxes(c_re, 1, 2).reshape(nblk, gpb, p_state, m))
    cim_blk = _block_diag(jnp.swapaxes(c_im, 1, 2).reshape(nblk, gpb, p_state, m))
    y_tm = s5_scan(u_tm.reshape(seq, batch, d), bb_blk.astype(BF16), cre_blk.astype(BF16),
                   cim_blk.astype(BF16), ab_re.reshape(nblk, 1, gpb * p_state),
                   ab_im.reshape(nblk, 1, gpb * p_state), _row(d_skip), lt=min(256, seq))
    return glu_residual(y_tm.reshape(seq, batch * d), w_glu.astype(BF16), x,
                        batch=batch, seq=seq, tm=tm)


def kernel(x, mem, mix_norm_g, xattn_norm_g, ffn_norm_g, mem_norm_g, final_norm_g, ab_w_in, ab_w_out, hg_lb_logits, hg_onorm_g, rw_mu, rw_w0, rw_w2, rw_a0, rw_a2, rw_g2, rw_k_k, rw_k_a, rw_r_k, rw_gn_g, rw_gn_b, c_w_in, s5_a_re, s5_a_im, s5_log_dt, s5_b_re, s5_b_im, s5_c_re, s5_c_im, s5_d, c_w_glu, xa_wq, xa_wkv, xa_wo, ffn_w13, ffn_w2, moe_router, moe_w13, moe_w2):
    batch, seq, d = x.shape
    n_mem = mem.shape[1]
    depth = mix_norm_g.shape[0]
    n = batch * seq
    x = x.reshape(n, d).astype(F32)
    mem2 = mem.reshape(batch * n_mem, d).astype(F32)
    tm = _row_tile(n, 512)
    tm_seq = _row_tile(seq, 512)
    dff = ffn_w2.shape[1]
    tf = dff // 2 if (dff // 2) % LANES == 0 else dff
    ffn_w13_b, ffn_w2_b = ffn_w13.astype(BF16), ffn_w2.astype(BF16)
    moe_w13_b, moe_w2_b = moe_w13.astype(BF16), moe_w2.astype(BF16)

    for layer in range(depth):
        j = layer // 2
        if layer % 2 == 0:
            x = ab_mixer_layer(x, mix_norm_g[layer], ab_w_in[j], ab_w_out[j], hg_lb_logits,
                               hg_onorm_g[j], rw_mu[j], rw_w0[j], rw_w2[j], rw_a0[j], rw_a2[j],
                               rw_g2[j], rw_k_k[j], rw_k_a[j], rw_r_k[j], rw_gn_g[j], rw_gn_b[j],
                               batch=batch, seq=seq, layer_j=j)
        else:
            x = s5_mixer_layer(x, mix_norm_g[layer], c_w_in[j], s5_a_re[j], s5_a_im[j],
                               s5_log_dt[j], s5_b_re[j], s5_b_im[j], s5_c_re[j], s5_c_im[j],
                               s5_d[j], c_w_glu[j], batch=batch, seq=seq)

        kv = norm_matmul(mem2, _row(mem_norm_g), xa_wkv[layer].astype(BF16),
                         tm=_row_tile(batch * n_mem, 512), tn=d, out_dtype=BF16)
        x = cross_attention_residual(x, _row(xattn_norm_g[layer]), xa_wq[layer].astype(BF16), kv,
                                     xa_wo[layer].astype(BF16), batch=batch, seq=seq, n_mem=n_mem,
                                     tm=tm_seq)

        if layer % 2 == 0:
            x = ffn_residual(x, _row(ffn_norm_g[layer]), ffn_w13_b, ffn_w2_b, layer=j, tm=tm, tf=tf)
        else:
            x = moe_residual(x, _row(ffn_norm_g[layer]), moe_router[j].T.astype(F32),
                             moe_w13_b, moe_w2_b, layer=j,
                             tm=_row_tile(n, MOE_TOKEN_TILE), tf=tf, rb=MOE_ROW_BLOCK)

    return final_norm(x, _row(final_norm_g), tm=tm).reshape(batch, seq, d)
```

```python
import functools
import math

import jax
import jax.numpy as jnp
from jax import lax
from jax.experimental import pallas as pl
from jax.experimental.pallas import tpu as pltpu

F32 = jnp.float32
BF16 = jnp.bfloat16

NORM_EPS = 1e-6
LANES = 128
SUBLANES = 8
VMEM_LIMIT_BYTES = 60 * 1024 * 1024

HG_DK = 128
HG_CHUNK = 64
HG_EXP_CLIP = 60.0
HG_TIME_BLOCK = 512

RW_N = 64
RW_CHUNK = 64
RW_GROUP_HEADS = 4
RW_STREAMS = 4
RW_DECAY_LORA = 32
RW_A_LORA = 32
RW_GATE_LORA = 96
RW_GN_EPS = 64e-5

S5_GROUP = 16
S5_STATE = 64
S5_GROUPS_PER_BLOCK = LANES // S5_GROUP
S5_BLOCK_STATE = S5_GROUPS_PER_BLOCK * S5_STATE

XA_HEADS = 4
MOE_TOKEN_TILE = 1024
MOE_ROW_BLOCK = 128
MOE_RANK_BLOCK = 256
MOE_SLAB_ROWS = 512

HIGHEST = lax.Precision.HIGHEST


def _cparams(*sem):
    return pltpu.CompilerParams(dimension_semantics=sem, vmem_limit_bytes=VMEM_LIMIT_BYTES)


def _dot(a, b, precision=None):
    return jnp.dot(a, b, preferred_element_type=F32, precision=precision)


def _dot_nt(a, b, precision=None):
    return lax.dot_general(a, b, (((1,), (1,)), ((), ())), preferred_element_type=F32,
                           precision=precision)


def _dot_tn(a, b, precision=None):
    return lax.dot_general(a, b, (((0,), (0,)), ((), ())), preferred_element_type=F32,
                           precision=precision)


def _split_bf16(t, terms):
    parts = []
    for _ in range(terms):
        hi = t.astype(BF16)
        parts.append(hi)
        t = t - hi.astype(F32)
    return parts


def _rms(x, g, eps=NORM_EPS):
    return x * lax.rsqrt(jnp.mean(x * x, axis=-1, keepdims=True) + eps) * g


def _silu(x):
    return x * jax.nn.sigmoid(x)


def _tri_incl(n):
    r = lax.broadcasted_iota(jnp.int32, (n, n), 0)
    c = lax.broadcasted_iota(jnp.int32, (n, n), 1)
    return jnp.where(r >= c, 1.0, 0.0).astype(F32)


def _norm_matmul_kernel(x_ref, g_ref, w_ref, o_ref, h_ref):
    @pl.when(pl.program_id(1) == 0)
    def _():
        h_ref[...] = _rms(x_ref[...], g_ref[...]).astype(BF16)

    o_ref[...] = _dot(h_ref[...], w_ref[...]).astype(o_ref.dtype)


def norm_matmul(x, g, w, *, layer, tm, tn, out_dtype=F32, out_shape=None, out_map=None):
    m, k = x.shape
    n = w.shape[2]
    assert m % tm == 0 and n % tn == 0
    if out_shape is None:
        out_shape = (m, n)
    if out_map is None:
        out_map = lambda i, j: (i, j)
    return pl.pallas_call(
        _norm_matmul_kernel,
        grid=(m // tm, n // tn),
        in_specs=[pl.BlockSpec((tm, k), lambda i, j: (i, 0)),
                  pl.BlockSpec((1, k), lambda i, j: (0, 0)),
                  pl.BlockSpec((None, k, tn), lambda i, j: (layer, 0, j))],
        out_specs=pl.BlockSpec((tm, tn), out_map),
        out_shape=jax.ShapeDtypeStruct(out_shape, out_dtype),
        scratch_shapes=[pltpu.VMEM((tm, k), BF16)],
        compiler_params=_cparams("parallel", "arbitrary"),
        name="norm_matmul",
    )(x, g, w)


def _norm_matmul_layers_kernel(x_ref, g_ref, w_ref, o_ref, h_ref):
    @pl.when((pl.program_id(1) == 0) & (pl.program_id(2) == 0))
    def _():
        h_ref[...] = _rms(x_ref[...], g_ref[...]).astype(BF16)

    o_ref[...] = _dot(h_ref[...], w_ref[...]).astype(o_ref.dtype)


def norm_matmul_layers(x, g, w, *, tm, tn):
    m, k = x.shape
    layers, _, n = w.shape
    nj = n // tn
    return pl.pallas_call(
        _norm_matmul_layers_kernel,
        grid=(m // tm, layers, nj),
        in_specs=[pl.BlockSpec((tm, k), lambda i, l, j: (i, 0)),
                  pl.BlockSpec((1, k), lambda i, l, j: (0, 0)),
                  pl.BlockSpec((None, k, tn), lambda i, l, j: (l, 0, j))],
        out_specs=pl.BlockSpec((tm, tn), lambda i, l, j: (i, l * nj + j)),
        out_shape=jax.ShapeDtypeStruct((m, layers * n), BF16),
        scratch_shapes=[pltpu.VMEM((tm, k), BF16)],
        compiler_params=_cparams("parallel", "arbitrary", "arbitrary"),
        name="norm_matmul_layers",
    )(x, g, w)


def _norm_matmul2_kernel(x_ref, g_ref, wa_ref, wb_ref, oa_ref, ob_ref):
    h = _rms(x_ref[...], g_ref[...]).astype(BF16)
    oa_ref[...] = _dot(h, wa_ref[...])
    ob_ref[...] = _dot(h, wb_ref[...])


def norm_matmul2(x, g, wa, wb, *, layer, tm):
    m, k = x.shape
    na, nb = wa.shape[2], wb.shape[2]
    return pl.pallas_call(
        _norm_matmul2_kernel,
        grid=(m // tm,),
        in_specs=[pl.BlockSpec((tm, k), lambda i: (i, 0)),
                  pl.BlockSpec((1, k), lambda i: (0, 0)),
                  pl.BlockSpec((None, k, na), lambda i: (layer, 0, 0)),
                  pl.BlockSpec((None, k, nb), lambda i: (layer, 0, 0))],
        out_specs=(pl.BlockSpec((tm, na), lambda i: (i, 0)), pl.BlockSpec((tm, nb), lambda i: (i, 0))),
        out_shape=(jax.ShapeDtypeStruct((m, na), F32), jax.ShapeDtypeStruct((m, nb), F32)),
        compiler_params=_cparams("parallel"),
        name="norm_matmul2",
    )(x, g, wa, wb)


def _hgrn2_kernel(q_ref, f_ref, i_ref, g_ref, lbl_ref, og_ref, o_ref, st_ref, *, layer_j, n_chunks):
    c = HG_CHUNK
    halves = [c >> (i + 1) for i in range(c.bit_length() - 1)]

    @pl.when(pl.program_id(2) == 0)
    def _():
        st_ref[...] = jnp.zeros_like(st_ref)

    logits = lbl_ref[...]
    e = jnp.exp(logits - jnp.max(logits, axis=0, keepdims=True))
    p = e / jnp.sum(e, axis=0, keepdims=True)
    lb = jnp.zeros((1, p.shape[1]), F32)
    for r in range(1, layer_j + 1):
        lb = lb + p[r:r + 1, :]

    og = og_ref[...]
    rr = lax.broadcasted_iota(jnp.int32, (c, c), 0)
    cc = lax.broadcasted_iota(jnp.int32, (c, c), 1)
    row = lax.broadcasted_iota(jnp.int32, (c, HG_DK), 0)

    sums = [rr >= cc]
    pair_masks = []
    for hb in halves:
        blk = 2 * hb
        sums.append(cc <= jnp.bitwise_and(rr, -blk) + (hb - 1))
        same = jnp.bitwise_and(rr, -blk) == jnp.bitwise_and(cc, -blk)
        pair_masks.append(jnp.where(same, jnp.bitwise_and(rr, hb) - jnp.bitwise_and(cc, hb), 0) > 0)
    prefix = jnp.concatenate([jnp.where(m, 1.0, 0.0) for m in sums], axis=0).astype(BF16)

    chunks = range(n_chunks)
    rows = [slice(ci * c, (ci + 1) * c) for ci in chunks]
    qs, ks, ivs, cums = [], [], [], []
    for ci in chunks:
        fr = f_ref[rows[ci], :]
        log_sig = jnp.minimum(fr, 0.0) - jnp.log1p(jnp.exp(-jnp.abs(fr)))
        log_f = log_sig + jnp.log1p(lb * jnp.exp(jnp.minimum(-fr, HG_EXP_CLIP)))
        cums.append(sum(_dot(prefix, part) for part in _split_bf16(log_f, 3)))
        qs.append(_silu(q_ref[rows[ci], :]))
        ks.append((1.0 - lb) * jax.nn.sigmoid(-fr))
        ivs.append(i_ref[rows[ci], :])
    ls = [cm[:c] for cm in cums]

    a_mats = [jnp.zeros((c, c), F32) for _ in chunks]
    for li, hb in enumerate(halves):
        upper = jnp.bitwise_and(row, hb) != 0
        for ci in chunks:
            d = ls[ci] - cums[ci][(li + 1) * c:(li + 2) * c]
            ex = jnp.exp(jnp.where(upper, d, -d))
            prod = _dot_nt((qs[ci] * ex).astype(BF16), (ks[ci] * ex).astype(BF16))
            a_mats[ci] = a_mats[ci] + jnp.where(pair_masks[li], prod, 0.0)

    os_, kvs, qes, ends = [], [], [], []
    for ci in chunks:
        q, k, l, iv = qs[ci], ks[ci], ls[ci], ivs[ci]
        iv_b = iv.astype(BF16)
        l_end = l[c - 1:c, :]
        os_.append(_dot(a_mats[ci].astype(BF16), iv_b) + jnp.sum(q * k, axis=-1, keepdims=True) * iv)
        qes.append((q * jnp.exp(l)).astype(BF16))
        kvs.append(_dot_tn(iv_b, (k * jnp.exp(l_end - l)).astype(BF16)))
        ends.append(jnp.exp(l_end))

    st = st_ref[...]
    for ci in range(n_chunks):
        rs = slice(ci * c, (ci + 1) * c)
        o = os_[ci] + _dot_nt(qes[ci], st.astype(BF16))
        st = st * ends[ci] + kvs[ci]
        o_ref[rs, :] = (_rms(o, og) * _silu(g_ref[rs, :])).astype(o_ref.dtype)
    st_ref[...] = st


def hgrn2_mix(p_hg, lb_logits, onorm_g, *, batch, seq, layer_j, tt=256):
    n, cols = p_hg.shape
    width = cols // 4
    heads = width // HG_DK
    assert seq % tt == 0 and tt % HG_CHUNK == 0
    tb = seq // tt
    n_ab = lb_logits.shape[0]

    def col_spec(part):
        return pl.BlockSpec((tt, HG_DK), lambda b, h, t, part=part: (b * tb + t, part * heads + h))

    kern = functools.partial(_hgrn2_kernel, layer_j=layer_j, n_chunks=tt // HG_CHUNK)
    return pl.pallas_call(
        kern,
        grid=(batch, heads, tb),
        in_specs=[col_spec(0), col_spec(1), col_spec(2), col_spec(3),
                  pl.BlockSpec((n_ab, HG_DK), lambda b, h, t: (0, h)),
                  pl.BlockSpec((1, HG_DK), lambda b, h, t: (0, 0))],
        out_specs=pl.BlockSpec((tt, HG_DK), lambda b, h, t: (b * tb + t, h)),
        out_shape=jax.ShapeDtypeStruct((n, width), BF16),
        scratch_shapes=[pltpu.VMEM((HG_DK, HG_DK), F32)],
        compiler_params=_cparams("parallel", "parallel", "arbitrary"),
        name="hgrn2_mix",
    )(p_hg, p_hg, p_hg, p_hg, lb_logits, onorm_g)


def _round_robin(chains):
    chains = list(chains)
    while chains:
        alive = []
        for ch in chains:
            try:
                next(ch)
                alive.append(ch)
            except StopIteration:
                pass
        chains = alive
        yield


def _rwkv7_stream(p_ref, mu_ref, w0_ref, w2_ref, a0_ref, a2_ref, g2_ref, kk_ref, ka_ref,
                  rk_ref, gng_ref, gnb_ref, ones_ref, o_ref, prev_ref, st_ref, *, heads):
    c = RW_CHUNK
    n = RW_N
    assert c == n
    width = heads * n
    gh = RW_GROUP_HEADS
    gw = gh * n
    gr = gh * c
    shift = c.bit_length() - 1

    @pl.when(pl.program_id(1) == 0)
    def _():
        prev_ref[...] = jnp.zeros_like(prev_ref)
        st_ref[...] = jnp.zeros_like(st_ref)

    p = p_ref[...]
    row = lax.broadcasted_iota(jnp.int32, p.shape, 0)
    shifted = jnp.where(row == 0, prev_ref[0:1, :], pltpu.roll(p, 1, axis=0))
    prev_ref[0:1, :] = p[c - 1:c, :]
    p = p + (shifted - p) * mu_ref[...]

    r = p[:, 0:width]
    k = p[:, width:2 * width]
    v = p[:, 2 * width:3 * width]
    off = 3 * width
    wd = p[:, off:off + RW_DECAY_LORA]
    ad = p[:, off + RW_DECAY_LORA:off + RW_DECAY_LORA + RW_A_LORA]
    gd = p[:, off + RW_DECAY_LORA + RW_A_LORA:off + RW_DECAY_LORA + RW_A_LORA + RW_GATE_LORA]

    wz = w0_ref[...] + _dot(jnp.tanh(wd).astype(BF16), w2_ref[...])
    w = jnp.minimum(wz, 0.0) - jnp.log1p(jnp.exp(-jnp.abs(wz))) - 0.5
    lw = -jnp.exp(w)
    a = jax.nn.sigmoid(a0_ref[...] + _dot(ad.astype(BF16), a2_ref[...]))
    g = _dot(jax.nn.sigmoid(gd).astype(BF16), g2_ref[...])
    k2 = k * (1.0 + (a - 1.0) * ka_ref[...])

    ones = ones_ref[...]

    def head_sum(t):
        return sum(_dot(part, ones) for part in _split_bf16(t, 2))

    kk = k * kk_ref[...]
    kk = kk * lax.rsqrt(jnp.maximum(head_sum(kk * kk), 1e-24))
    b = kk * a

    tri = _tri_incl(c).astype(BF16)
    cum = sum(_dot(tri, part) for part in _split_bf16(lw, 3))
    cum_end = cum[c - 1:c, :]
    e_inv = jnp.exp(-cum)
    e_fin = jnp.exp(cum_end - cum)
    e_end = jnp.exp(cum_end)
    at = -kk * jnp.exp(cum - lw)
    rt = r * jnp.exp(cum)
    bt = b * e_inv
    kt = k2 * e_inv
    bf = b * e_fin
    kf = k2 * e_fin

    rb = lax.broadcasted_iota(jnp.int32, (gr, gw), 0)
    cb = lax.broadcasted_iota(jnp.int32, (gr, gw), 1)
    same = jnp.right_shift(rb, shift) == jnp.right_shift(cb, shift)
    t_in = jnp.bitwise_and(rb, c - 1)
    s_in = jnp.bitwise_and(cb, c - 1)
    strict = t_in > s_in
    incl = t_in >= s_in
    eye = jnp.where(rb == cb, 1.0, 0.0).astype(F32)

    def bd(t):
        return jnp.where(same, jnp.concatenate([t] * gh, axis=0), 0.0).astype(BF16)

    ngroups = heads // gh
    outs = [None] * ngroups

    def group_chain(gi):
        gs = slice(gi * gw, (gi + 1) * gw)
        art = jnp.concatenate([bd(at[:, gs]), bd(rt[:, gs])], axis=0)
        v_bd = bd(v[:, gs])
        g_b = _dot_nt(art, bd(bt[:, gs]))
        g_k = _dot_nt(art, bd(kt[:, gs]))
        yield
        a_ab = jnp.where(strict, g_b[:gr], 0.0)
        a_ak = jnp.where(strict, g_k[:gr], 0.0)
        a_rb = jnp.where(incl, g_b[gr:], 0.0)
        a_rk = jnp.where(incl, g_k[gr:], 0.0)

        tinv = eye + a_ab
        x = a_ab
        for _ in range(shift - 1):
            xb = x.astype(BF16)
            x = _dot(xb, xb)
            yield
            tinv = tinv + _dot(tinv.astype(BF16), x.astype(BF16))
            yield

        st = st_ref[gi]
        arm = _dot_nt(art, st.astype(BF16))
        akv = _dot(jnp.concatenate([a_ak, a_rk], axis=0).astype(BF16), v_bd)
        yield
        u_b = _dot(tinv.astype(BF16), (arm[:gr] + akv[:gr]).astype(BF16)).astype(BF16)
        yield
        o_bd = arm[gr:] + akv[gr:] + _dot(a_rb.astype(BF16), u_b)
        bk = jnp.concatenate([bd(bf[:, gs]), bd(kf[:, gs])], axis=0)
        st_ref[gi] = st * e_end[:, gs] + _dot_tn(jnp.concatenate([u_b, v_bd], axis=0), bk)
        o_g = o_bd[0:c]
        for hh in range(1, gh):
            o_g = o_g + o_bd[hh * c:(hh + 1) * c]
        outs[gi] = o_g

    yield
    yield from _round_robin([group_chain(gi) for gi in range(ngroups)])

    o = jnp.concatenate(outs, axis=-1)
    mean = head_sum(o) * (1.0 / n)
    dev = o - mean
    var = head_sum(dev * dev) * (1.0 / n)
    on = dev * lax.rsqrt(var + RW_GN_EPS) * gng_ref[...] + gnb_ref[...]
    bonus = head_sum(r * k2 * rk_ref[...]) * v
    o_ref[...] = ((on + bonus) * g).astype(o_ref.dtype)


def _rwkv7_kernel(p_ref, *refs, heads, streams):
    *param_refs, o_ref, prev_ref, st_ref = refs
    chains = [_rwkv7_stream(p_ref.at[bi], *param_refs, o_ref.at[bi], prev_ref.at[bi],
                            st_ref.at[bi], heads=heads) for bi in range(streams)]
    for _ in _round_robin(chains):
        pass


def rwkv7_mix(p_rw, mu, w0, w2, a0, a2, g2, k_k, k_a, r_k, gn_g, gn_b, *, batch, seq):
    n_tok, cols = p_rw.shape
    width = w0.shape[1]
    heads = width // RW_N
    assert seq % RW_CHUNK == 0 and heads % RW_GROUP_HEADS == 0
    cb = seq // RW_CHUNK
    head_id = jnp.arange(width, dtype=jnp.int32) // RW_N
    ones = (head_id[:, None] == head_id[None, :]).astype(BF16)
    full = lambda a: pl.BlockSpec(a.shape, lambda b, t: (0,) * a.ndim)
    params = (mu, w0, w2, a0, a2, g2, k_k, k_a, r_k, gn_g, gn_b, ones)
    gdim = RW_GROUP_HEADS * RW_N
    nb = RW_STREAMS if batch % RW_STREAMS == 0 else 1
    out = pl.pallas_call(
        functools.partial(_rwkv7_kernel, heads=heads, streams=nb),
        grid=(batch // nb, cb),
        in_specs=[pl.BlockSpec((nb, RW_CHUNK, cols), lambda b, t: (b, t, 0))]
                 + [full(a) for a in params],
        out_specs=pl.BlockSpec((nb, RW_CHUNK, width), lambda b, t: (b, t, 0)),
        out_shape=jax.ShapeDtypeStruct((batch, seq, width), BF16),
        scratch_shapes=[pltpu.VMEM((nb, SUBLANES, cols), F32),
                        pltpu.VMEM((nb, heads // RW_GROUP_HEADS, gdim, gdim), F32)],
        compiler_params=_cparams("parallel", "arbitrary"),
        name="rwkv7_mix",
    )(p_rw.reshape(batch, seq, cols), *params)
    return out.reshape(n_tok, width)


def _s5_param_kernel(are_ref, aim_ref, ldt_ref, bre_ref, bim_ref,
                     abre_ref, abim_ref, bbre_ref, bbim_ref):
    lam_re = jnp.minimum(are_ref[...], -1e-4)
    lam_im = aim_ref[...]
    dt = jnp.exp(ldt_ref[...])
    mag = jnp.exp(lam_re * dt)
    ab_re = mag * jnp.cos(lam_im * dt)
    ab_im = mag * jnp.sin(lam_im * dt)
    den = lam_re * lam_re + lam_im * lam_im
    coef_re = ((ab_re - 1.0) * lam_re + ab_im * lam_im) / den
    coef_im = (ab_im * lam_re - (ab_re - 1.0) * lam_im) / den
    abre_ref[...] = ab_re
    abim_ref[...] = ab_im
    b_re = bre_ref[...]
    b_im = bim_ref[...]
    bbre_ref[...] = coef_re * b_re - coef_im * b_im
    bbim_ref[...] = coef_re * b_im + coef_im * b_re


def s5_params(a_re, a_im, log_dt, b_re_t, b_im_t):
    g, _, p = a_re.shape
    m = b_re_t.shape[1]
    return pl.pallas_call(
        _s5_param_kernel,
        out_shape=(jax.ShapeDtypeStruct((g, 1, p), F32), jax.ShapeDtypeStruct((g, 1, p), F32),
                   jax.ShapeDtypeStruct((g, m, p), F32), jax.ShapeDtypeStruct((g, m, p), F32)),
        name="s5_params",
    )(a_re, a_im, log_dt, b_re_t, b_im_t)


def _gelu_tanh(y):
    return 0.5 * y * (1.0 + jnp.tanh(math.sqrt(2.0 / math.pi) * (y + 0.044715 * (y * y * y))))


def _s5_scan_kernel(u_ref, bb_ref, cre_ref, cim_ref, are_ref, aim_ref, d_ref, o_ref,
                    xs_ref, st_ref, *, lt, bsz):
    ns = S5_BLOCK_STATE

    @pl.when(pl.program_id(1) == 0)
    def _():
        st_ref[...] = jnp.zeros_like(st_ref)

    u2 = u_ref[...].reshape(lt * bsz, LANES)
    xs_ref[...] = _dot(u2.astype(BF16), bb_ref[...])
    a_re = jnp.broadcast_to(are_ref[...], (bsz, ns))
    a_im = jnp.broadcast_to(aim_ref[...], (bsz, ns))

    def step(t, carry):
        x_re, x_im = carry
        r0 = pl.multiple_of(t * bsz, bsz)
        n_re = a_re * x_re - a_im * x_im + xs_ref[pl.ds(r0, bsz), 0:ns]
        n_im = a_re * x_im + a_im * x_re + xs_ref[pl.ds(r0, bsz), ns:2 * ns]
        xs_ref[pl.ds(r0, bsz), 0:ns] = n_re
        xs_ref[pl.ds(r0, bsz), ns:2 * ns] = n_im
        return n_re, n_im

    x_re, x_im = lax.fori_loop(0, lt, step, (st_ref[0], st_ref[1]), unroll=8)
    st_ref[0] = x_re
    st_ref[1] = x_im

    y = (_dot(xs_ref[:, 0:ns].astype(BF16), cre_ref[...])
         - _dot(xs_ref[:, ns:2 * ns].astype(BF16), cim_ref[...]))
    y = y + d_ref[...] * u2
    o_ref[...] = _gelu_tanh(y).reshape(lt, bsz, LANES)


def s5_scan(u_tm, bb_blk, cre_blk, cim_blk, ab_re, ab_im, d, *, lt=256):
    seq, bsz, dm = u_tm.shape
    nblk = dm // LANES
    ns = S5_BLOCK_STATE
    assert seq % lt == 0
    return pl.pallas_call(
        functools.partial(_s5_scan_kernel, lt=lt, bsz=bsz),
        grid=(nblk, seq // lt),
        in_specs=[pl.BlockSpec((lt, bsz, LANES), lambda c, t: (t, 0, c)),
                  pl.BlockSpec((None, LANES, 2 * ns), lambda c, t: (c, 0, 0)),
                  pl.BlockSpec((None, ns, LANES), lambda c, t: (c, 0, 0)),
                  pl.BlockSpec((None, ns, LANES), lambda c, t: (c, 0, 0)),
                  pl.BlockSpec((None, 1, ns), lambda c, t: (c, 0, 0)),
                  pl.BlockSpec((None, 1, ns), lambda c, t: (c, 0, 0)),
                  pl.BlockSpec((1, LANES), lambda c, t: (0, c))],
        out_specs=pl.BlockSpec((lt, bsz, LANES), lambda c, t: (t, 0, c)),
        out_shape=jax.ShapeDtypeStruct((seq, bsz, dm), F32),
        scratch_shapes=[pltpu.VMEM((lt * bsz, 2 * ns), F32),
                        pltpu.VMEM((2, bsz, ns), F32)],
        compiler_params=_cparams("parallel", "arbitrary"),
        name="s5_scan",
    )(u_tm, bb_blk, cre_blk, cim_blk, ab_re, ab_im, d)


def _xattn_core(x, g_ref, wq_ref, k_ref, v_ref, wo_ref, heads):
    d = x.shape[1]
    hd = d // heads
    h = _rms(x, g_ref[...]).astype(BF16)
    q = _dot(h, wq_ref[...]).astype(BF16)
    outs = []
    for i in range(heads):
        sl = slice(i * hd, (i + 1) * hd)
        s = _dot_nt(q[:, sl], k_ref[:, sl]) * (hd ** -0.5)
        s = s - jnp.max(s, axis=-1, keepdims=True)
        e = jnp.exp(s)
        p = e / jnp.sum(e, axis=-1, keepdims=True)
        outs.append(_dot(p.astype(BF16), v_ref[:, sl]))
    o = jnp.concatenate(outs, axis=-1).astype(BF16)
    return x + _dot(o, wo_ref[...])


def _ab_out_xattn_kernel(ya_ref, yb_ref, wa_ref, wb_ref, x_ref, g_ref, wq_ref, k_ref, v_ref, wo_ref,
                         o_ref, *, heads):
    x = x_ref[...] + _dot(ya_ref[...], wa_ref[...]) + _dot(yb_ref[...], wb_ref[...])
    o_ref[...] = _xattn_core(x, g_ref, wq_ref, k_ref, v_ref, wo_ref, heads)


def _glu_out_xattn_kernel(y_ref, wglu_ref, x_ref, g_ref, wq_ref, k_ref, v_ref, wo_ref, o_ref, *, heads):
    d = x_ref.shape[1]
    z = _dot(y_ref[...].astype(BF16), wglu_ref[...])
    x = x_ref[...] + z[:, :d] * jax.nn.sigmoid(z[:, d:])
    o_ref[...] = _xattn_core(x, g_ref, wq_ref, k_ref, v_ref, wo_ref, heads)


def _xattn_specs(tm, d, n_mem, layer, rows):
    return [pl.BlockSpec((tm, d), rows),
            pl.BlockSpec((1, d), lambda b, t: (0, 0)),
            pl.BlockSpec((None, d, d), lambda b, t: (layer, 0, 0)),
            pl.BlockSpec((n_mem, d), lambda b, t: (b, 2 * layer)),
            pl.BlockSpec((n_mem, d), lambda b, t: (b, 2 * layer + 1)),
            pl.BlockSpec((None, d, d), lambda b, t: (layer, 0, 0))]


def ab_out_cross_attention(ya, yb, w_out, x, g, wq, kv, wo, *, layer_j, layer, batch, seq, n_mem, tm):
    n, d = x.shape
    ka, kb = ya.shape[1], yb.shape[1]
    assert ka == kb
    tb = seq // tm
    rows = lambda b, t: (b * tb + t, 0)
    xa = _xattn_specs(tm, d, n_mem, layer, rows)
    return pl.pallas_call(
        functools.partial(_ab_out_xattn_kernel, heads=XA_HEADS),
        grid=(batch, tb),
        in_specs=[pl.BlockSpec((tm, ka), rows), pl.BlockSpec((tm, kb), rows),
                  pl.BlockSpec((None, ka, d), lambda b, t: (layer_j, 0, 0)),
                  pl.BlockSpec((None, kb, d), lambda b, t: (layer_j, 1, 0))] + xa,
        out_specs=pl.BlockSpec((tm, d), rows),
        out_shape=jax.ShapeDtypeStruct((n, d), F32),
        compiler_params=_cparams("parallel", "parallel"),
        name="ab_out_cross_attention",
    )(ya, yb, w_out, w_out, x, g, wq, kv, kv, wo)


def glu_out_cross_attention(y_tm2d, w_glu, x, g, wq, kv, wo, *, layer_j, layer, batch, seq, n_mem, tm):
    n, d = x.shape
    tb = seq // tm
    rows = lambda b, t: (b * tb + t, 0)
    xa = _xattn_specs(tm, d, n_mem, layer, rows)
    return pl.pallas_call(
        functools.partial(_glu_out_xattn_kernel, heads=XA_HEADS),
        grid=(batch, tb),
        in_specs=[pl.BlockSpec((tm, d), lambda b, t: (t, b)),
                  pl.BlockSpec((None, d, 2 * d), lambda b, t: (layer_j, 0, 0))] + xa,
        out_specs=pl.BlockSpec((tm, d), rows),
        out_shape=jax.ShapeDtypeStruct((n, d), F32),
        compiler_params=_cparams("parallel", "parallel"),
        name="glu_out_cross_attention",
    )(y_tm2d, w_glu, x, g, wq, kv, kv, wo)


def _ffn_kernel(x_ref, g_ref, wg_ref, wu_ref, w2_ref, o_ref, h_ref, acc_ref):
    c = pl.program_id(1)

    @pl.when(c == 0)
    def _():
        h_ref[...] = _rms(x_ref[...], g_ref[...]).astype(BF16)
        acc_ref[...] = jnp.zeros_like(acc_ref)

    h = h_ref[...]
    a = _silu(_dot(h, wg_ref[...])) * _dot(h, wu_ref[...])
    acc_ref[...] += _dot(a.astype(BF16), w2_ref[...])

    @pl.when(c == pl.num_programs(1) - 1)
    def _():
        o_ref[...] = x_ref[...] + acc_ref[...]


def ffn_residual(x, g, w13, w2, *, layer, tm, tf):
    n, d = x.shape
    dff = w2.shape[1]
    nc = dff // tf
    return pl.pallas_call(
        _ffn_kernel,
        grid=(n // tm, nc),
        in_specs=[pl.BlockSpec((tm, d), lambda i, c: (i, 0)),
                  pl.BlockSpec((1, d), lambda i, c: (0, 0)),
                  pl.BlockSpec((None, d, tf), lambda i, c: (layer, 0, c)),
                  pl.BlockSpec((None, d, tf), lambda i, c: (layer, 0, nc + c)),
                  pl.BlockSpec((None, tf, d), lambda i, c: (layer, c, 0))],
        out_specs=pl.BlockSpec((tm, d), lambda i, c: (i, 0)),
        out_shape=jax.ShapeDtypeStruct((n, d), F32),
        scratch_shapes=[pltpu.VMEM((tm, d), BF16), pltpu.VMEM((tm, d), F32)],
        compiler_params=_cparams("parallel", "arbitrary"),
        name="ffn_residual",
    )(x, g, w13, w13, w2)


def _moe_route(logits_t, n_experts, rb):
    m = logits_t.shape[1]
    sub = lax.broadcasted_iota(jnp.int32, logits_t.shape, 0).astype(F32)
    neg = -jnp.inf
    m1 = jnp.max(logits_t, axis=0, keepdims=True)
    i1 = jnp.min(jnp.where(logits_t == m1, sub, float(n_experts)), axis=0, keepdims=True)
    first = sub == i1
    rest = jnp.where(first, neg, logits_t)
    m2 = jnp.max(rest, axis=0, keepdims=True)
    i2 = jnp.min(jnp.where(rest == m2, sub, float(n_experts)), axis=0, keepdims=True)
    second = sub == i2
    e2 = jnp.exp(m2 - m1)
    den = 1.0 + e2
    gate0 = 1.0 / den
    gate1 = e2 / den

    first_f = jnp.where(first, 1.0, 0.0)
    second_f = jnp.where(second, 1.0, 0.0)
    both = jnp.concatenate([first_f, second_f], axis=0)
    w = min(m, MOE_RANK_BLOCK)
    t_src = lax.broadcasted_iota(jnp.int32, (w, w), 0)
    t_dst = lax.broadcasted_iota(jnp.int32, (w, w), 1)
    before = jnp.where(t_src < t_dst, 1.0, 0.0).astype(BF16)
    cnt = jnp.zeros((2 * n_experts, 1), F32)
    ranks = []
    for kb in range(m // w):
        blk = both[:, kb * w:(kb + 1) * w]
        ranks.append(_dot(blk.astype(BF16), before) + cnt)
        cnt = cnt + jnp.sum(blk, axis=1, keepdims=True)
    ranks = jnp.concatenate(ranks, axis=1)
    cnt0 = cnt[:n_experts]
    cnt_e = cnt0 + cnt[n_experts:]
    padded = jnp.floor((cnt_e + (rb - 1)) * (1.0 / rb)) * rb
    sub_col = lax.broadcasted_iota(jnp.int32, (n_experts, 1), 0)
    start = jnp.zeros((n_experts, 1), F32)
    for ee in range(n_experts - 1):
        start = start + jnp.where(sub_col > ee, padded[ee:ee + 1, :], 0.0)
    dest0 = jnp.sum(first_f * (start + ranks[:n_experts]), axis=0, keepdims=True)
    dest1 = jnp.sum(second_f * (start + cnt0 + ranks[n_experts:]), axis=0, keepdims=True)
    return dest0, dest1, gate0, gate1, start, padded


def _moe_kernel(x_ref, g_ref, rt_ref, wg_ref, wu_ref, w2_ref, fg_ref, o_ref, xs_ref, gw_ref, y_ref,
                meta_ref, *, n_experts, rb, final_norm):
    c = pl.program_id(1)
    e = pl.program_id(2)
    n_rows = xs_ref.shape[0]
    tm = x_ref.shape[0]
    slab = MOE_SLAB_ROWS
    slabs_used = 2 * n_experts

    @pl.when((c == 0) & (e == 0))
    def _route():
        x = x_ref[...]
        hf = _rms(x, g_ref[...])
        o_ref[...] = x
        y_ref[...] = jnp.zeros_like(y_ref)
        h_b = hf.astype(BF16)
        h_lo = (hf - h_b.astype(F32)).astype(BF16)
        rt_hi, rt_lo = _split_bf16(rt_ref[...], 2)
        logits_t = _dot_nt(rt_hi, h_b) + _dot_nt(rt_hi, h_lo) + _dot_nt(rt_lo, h_b)
        dest0, dest1, gate0, gate1, start, padded = _moe_route(logits_t, n_experts, rb)
        for ee in range(n_experts):
            meta_ref[ee] = jnp.sum(start[ee:ee + 1, :]).astype(jnp.int32)
            meta_ref[n_experts + ee] = jnp.sum(padded[ee:ee + 1, :] * (1.0 / rb)).astype(jnp.int32)
        used = jnp.sum(jnp.floor((jnp.sum(padded, axis=0, keepdims=True) + (slab - 1)) * (1.0 / slab)))
        meta_ref[slabs_used] = used.astype(jnp.int32)

        def build(j, carry):
            r0 = pl.multiple_of(j * rb, rb)
            rows = (lax.broadcasted_iota(jnp.int32, (rb, tm), 0) + r0).astype(F32)
            hit0 = rows == dest0
            hit1 = rows == dest1
            onehot = jnp.where(hit0 | hit1, 1.0, 0.0).astype(BF16)
            xs_ref[pl.ds(r0, rb), :] = _dot(onehot, h_b).astype(BF16)
            gw_ref[pl.ds(r0, rb), :] = (jnp.where(hit0, gate0, 0.0)
                                        + jnp.where(hit1, gate1, 0.0)).astype(BF16)
            return carry

        lax.fori_loop(0, meta_ref[slabs_used] * (slab // rb), build, 0)

    seg_start = meta_ref[e]
    seg_blocks = meta_ref[n_experts + e]

    def expert_rows(r0, rows):
        xb = xs_ref[pl.ds(r0, rows), :]
        a = _silu(_dot(xb, wg_ref[...])) * _dot(xb, wu_ref[...])
        y_ref[pl.ds(r0, rows), :] = _dot(a.astype(BF16), w2_ref[...]).astype(BF16)

    def expert_pair(j, carry):
        expert_rows(pl.multiple_of(seg_start + j * (2 * rb), rb), 2 * rb)
        return carry

    lax.fori_loop(0, seg_blocks // 2, expert_pair, 0)

    @pl.when(seg_blocks % 2 == 1)
    def _odd_block():
        expert_rows(pl.multiple_of(seg_start + (seg_blocks - 1) * rb, rb), rb)

    @pl.when(e == pl.num_programs(2) - 1)
    def _combine():
        for kb in range(n_rows // slab):
            @pl.when(kb < meta_ref[slabs_used])
            def _():
                rs = slice(kb * slab, (kb + 1) * slab)
                o_ref[...] += _dot_tn(gw_ref[rs, :], y_ref[rs, :])

        if final_norm:
            @pl.when(c == pl.num_programs(1) - 1)
            def _():
                o_ref[...] = _rms(o_ref[...], fg_ref[...])


def moe_residual(x, g, router_t, w13, w2, final_g, *, layer, tm, tf, rb, final_norm):
    n, d = x.shape
    _, n_experts, dff, _ = w2.shape
    nc = dff // tf
    n_rows = -(-(2 * tm + n_experts * rb) // MOE_SLAB_ROWS) * MOE_SLAB_ROWS
    return pl.pallas_call(
        functools.partial(_moe_kernel, n_experts=n_experts, rb=rb, final_norm=final_norm),
        grid=(n // tm, nc, n_experts),
        in_specs=[pl.BlockSpec((tm, d), lambda i, c, e: (i, 0), pipeline_mode=pl.Buffered(1)),
                  pl.BlockSpec((1, d), lambda i, c, e: (0, 0)),
                  pl.BlockSpec((n_experts, d), lambda i, c, e: (0, 0)),
                  pl.BlockSpec((None, None, d, tf), lambda i, c, e: (layer, e, 0, c)),
                  pl.BlockSpec((None, None, d, tf), lambda i, c, e: (layer, e, 0, nc + c)),
                  pl.BlockSpec((None, None, tf, d), lambda i, c, e: (layer, e, c, 0)),
                  pl.BlockSpec((1, d), lambda i, c, e: (0, 0))],
        out_specs=pl.BlockSpec((tm, d), lambda i, c, e: (i, 0)),
        out_shape=jax.ShapeDtypeStruct((n, d), F32),
        scratch_shapes=[pltpu.VMEM((n_rows, d), BF16), pltpu.VMEM((n_rows, tm), BF16),
                        pltpu.VMEM((n_rows, d), BF16), pltpu.SMEM((2 * n_experts + 1,), jnp.int32)],
        compiler_params=_cparams("parallel", "arbitrary", "arbitrary"),
        name="moe_residual",
    )(x, g, router_t, w13, w13, w2, final_g)


def _final_norm_kernel(x_ref, g_ref, o_ref):
    o_ref[...] = _rms(x_ref[...], g_ref[...])


def final_norm(x, g, *, tm):
    n, d = x.shape
    return pl.pallas_call(
        _final_norm_kernel,
        grid=(n // tm,),
        in_specs=[pl.BlockSpec((tm, d), lambda i: (i, 0)), pl.BlockSpec((1, d), lambda i: (0, 0))],
        out_specs=pl.BlockSpec((tm, d), lambda i: (i, 0)),
        out_shape=jax.ShapeDtypeStruct((n, d), F32),
        compiler_params=_cparams("parallel"),
        name="final_norm",
    )(x, g)


def _row(v):
    return v.reshape(1, -1).astype(F32)


def _pad_cols(a, cols):
    return jnp.pad(a, ((0, 0), (0, cols - a.shape[1])))


def _row_tile(n, pref):
    t = min(pref, n)
    while n % t:
        t //= 2
    return t


def ab_mixer(x, norm_g, w_hg, w_rw, lb_logits, hg_onorm_g, rw_mu, rw_w0, rw_w2, rw_a0, rw_a2, rw_g2,
             rw_k_k, rw_k_a, rw_r_k, rw_gn_g, rw_gn_b, *, batch, seq, layer_j):
    n, d = x.shape
    rw_pad = w_rw.shape[2]
    p_hg, p_rw = norm_matmul2(x, _row(norm_g), w_hg, w_rw, layer=layer_j, tm=_row_tile(n, 512))
    y_a = hgrn2_mix(p_hg, lb_logits.astype(F32), _row(hg_onorm_g), batch=batch, seq=seq,
                    layer_j=layer_j, tt=min(HG_TIME_BLOCK, seq))
    y_b = rwkv7_mix(p_rw, _pad_cols(_row(rw_mu), rw_pad), _row(rw_w0), rw_w2.astype(BF16),
                    _row(rw_a0), rw_a2.astype(BF16), rw_g2.astype(BF16), _row(rw_k_k),
                    _row(rw_k_a), _row(rw_r_k), _row(rw_gn_g), _row(rw_gn_b),
                    batch=batch, seq=seq)
    return y_a, y_b


def _block_diag(t):
    nblk, gpb, r, c = t.shape
    eye = jnp.eye(gpb, dtype=t.dtype)
    return jnp.einsum('bgrc,gh->bgrhc', t, eye).reshape(nblk, gpb * r, gpb * c)


def s5_mixer(x, norm_g, w_in, a_re, a_im, log_dt, b_re, b_im, c_re, c_im, d_skip, *, batch, seq, layer_j):
    n, d = x.shape
    groups, p_state = a_re.shape
    gpb = S5_GROUPS_PER_BLOCK
    nblk = groups // gpb
    tm = _row_tile(seq, 512)
    tb = seq // tm
    u_tm = norm_matmul(x, _row(norm_g), w_in, layer=layer_j, tm=tm, tn=d, out_shape=(seq, batch * d),
                       out_map=lambda i, j: (i % tb, i // tb))
    ab_re, ab_im, bb_re, bb_im = s5_params(
        a_re.reshape(groups, 1, p_state).astype(F32), a_im.reshape(groups, 1, p_state).astype(F32),
        log_dt.reshape(groups, 1, 1).astype(F32),
        jnp.swapaxes(b_re, 1, 2).astype(F32), jnp.swapaxes(b_im, 1, 2).astype(F32))
    m = bb_re.shape[1]
    bb_blk = jnp.concatenate([_block_diag(bb_re.reshape(nblk, gpb, m, p_state)),
                              _block_diag(bb_im.reshape(nblk, gpb, m, p_state))], axis=-1)
    cre_blk = _block_diag(jnp.swapaxes(c_re, 1, 2).reshape(nblk, gpb, p_state, m))
    cim_blk = _block_diag(jnp.swapaxes(c_im, 1, 2).reshape(nblk, gpb, p_state, m))
    y_tm = s5_scan(u_tm.reshape(seq, batch, d), bb_blk.astype(BF16), cre_blk.astype(BF16),
                   cim_blk.astype(BF16), ab_re.reshape(nblk, 1, gpb * p_state),
                   ab_im.reshape(nblk, 1, gpb * p_state), _row(d_skip), lt=min(256, seq))
    return y_tm.reshape(seq, batch * d)


def kernel(x, mem, mix_norm_g, xattn_norm_g, ffn_norm_g, mem_norm_g, final_norm_g, ab_w_in, ab_w_out, hg_lb_logits, hg_onorm_g, rw_mu, rw_w0, rw_w2, rw_a0, rw_a2, rw_g2, rw_k_k, rw_k_a, rw_r_k, rw_gn_g, rw_gn_b, c_w_in, s5_a_re, s5_a_im, s5_log_dt, s5_b_re, s5_b_im, s5_c_re, s5_c_im, s5_d, c_w_glu, xa_wq, xa_wkv, xa_wo, ffn_w13, ffn_w2, moe_router, moe_w13, moe_w2):
    batch, seq, d = x.shape
    n_mem = mem.shape[1]
    depth = mix_norm_g.shape[0]
    n = batch * seq
    x = x.reshape(n, d).astype(F32)
    mem2 = mem.reshape(batch * n_mem, d).astype(F32)
    tm = _row_tile(n, 512)
    tm_seq = _row_tile(seq, 512)
    dff = ffn_w2.shape[1]
    tf = dff // 2 if (dff // 2) % LANES == 0 else dff

    hg_cols = 4 * hg_lb_logits.shape[1]
    rw_cols = ab_w_in.shape[2] - hg_cols
    rw_pad = -(-rw_cols // LANES) * LANES
    w_hg = ab_w_in[:, :, :hg_cols].astype(BF16)
    w_rw = jnp.pad(ab_w_in[:, :, hg_cols:], ((0, 0), (0, 0), (0, rw_pad - rw_cols))).astype(BF16)
    w_out_b, c_w_in_b, c_w_glu_b = ab_w_out.astype(BF16), c_w_in.astype(BF16), c_w_glu.astype(BF16)
    wq_b, wkv_b, wo_b = xa_wq.astype(BF16), xa_wkv.astype(BF16), xa_wo.astype(BF16)
    ffn_w13_b, ffn_w2_b = ffn_w13.astype(BF16), ffn_w2.astype(BF16)
    moe_w13_b, moe_w2_b = moe_w13.astype(BF16), moe_w2.astype(BF16)
    final_g = _row(final_norm_g)

    kv = norm_matmul_layers(mem2, _row(mem_norm_g), wkv_b, tm=_row_tile(batch * n_mem, 512), tn=d)

    for layer in range(depth):
        j = layer // 2
        xa_args = dict(layer_j=j, layer=layer, batch=batch, seq=seq, n_mem=n_mem, tm=tm_seq)
        if layer % 2 == 0:
            y_a, y_b = ab_mixer(x, mix_norm_g[layer], w_hg, w_rw, hg_lb_logits, hg_onorm_g[j],
                                rw_mu[j], rw_w0[j], rw_w2[j], rw_a0[j], rw_a2[j], rw_g2[j], rw_k_k[j],
                                rw_k_a[j], rw_r_k[j], rw_gn_g[j], rw_gn_b[j],
                                batch=batch, seq=seq, layer_j=j)
            x = ab_out_cross_attention(y_a, y_b, w_out_b, x, _row(xattn_norm_g[layer]), wq_b, kv, wo_b,
                                       **xa_args)
            x = ffn_residual(x, _row(ffn_norm_g[layer]), ffn_w13_b, ffn_w2_b, layer=j, tm=tm, tf=tf)
        else:
            y = s5_mixer(x, mix_norm_g[layer], c_w_in_b, s5_a_re[j], s5_a_im[j], s5_log_dt[j],
                         s5_b_re[j], s5_b_im[j], s5_c_re[j], s5_c_im[j], s5_d[j],
                         batch=batch, seq=seq, layer_j=j)
            x = glu_out_cross_attention(y, c_w_glu_b, x, _row(xattn_norm_g[layer]), wq_b, kv, wo_b,
                                        **xa_args)
            x = moe_residual(x, _row(ffn_norm_g[layer]), moe_router[j].T.astype(F32),
                             moe_w13_b, moe_w2_b, final_g, layer=j, tm=_row_tile(n, MOE_TOKEN_TILE),
                             tf=tf, rb=MOE_ROW_BLOCK, final_norm=layer == depth - 1)

    if depth % 2 == 1:
        x = final_norm(x, final_g, tm=tm)
    return x.reshape(batch, seq, d)
```

```python
import functools
import math

import jax
import jax.numpy as jnp
from jax import lax
from jax.experimental import pallas as pl
from jax.experimental.pallas import tpu as pltpu

F32 = jnp.float32
BF16 = jnp.bfloat16

NORM_EPS = 1e-6
LANES = 128
SUBLANES = 8
VMEM_LIMIT_BYTES = 60 * 1024 * 1024

HG_DK = 128
HG_CHUNK = 64
HG_EXP_CLIP = 60.0
HG_TIME_BLOCK = 512

RW_N = 64
RW_CHUNK = 64
RW_GROUP_HEADS = 4
RW_STREAMS = 4
RW_DECAY_LORA = 32
RW_A_LORA = 32
RW_GATE_LORA = 96
RW_GN_EPS = 64e-5

S5_GROUP = 16
S5_STATE = 64
S5_GROUPS_PER_BLOCK = LANES // S5_GROUP
S5_BLOCK_STATE = S5_GROUPS_PER_BLOCK * S5_STATE
S5_SUB_STEPS = 64
S5_YIELD_STEPS = 16

XA_HEADS = 4
MOE_TOKEN_TILE = 1024
MOE_ROW_BLOCK = 128
MOE_RANK_BLOCK = 256
MOE_SLAB_ROWS = 512
MOE_BUILD_ROWS = 256


def _cparams(*sem):
    return pltpu.CompilerParams(dimension_semantics=sem, vmem_limit_bytes=VMEM_LIMIT_BYTES)


def _dot(a, b, precision=None):
    return jnp.dot(a, b, preferred_element_type=F32, precision=precision)


def _dot_nt(a, b, precision=None):
    return lax.dot_general(a, b, (((1,), (1,)), ((), ())), preferred_element_type=F32,
                           precision=precision)


def _dot_tn(a, b, precision=None):
    return lax.dot_general(a, b, (((0,), (0,)), ((), ())), preferred_element_type=F32,
                           precision=precision)


def _split_bf16(t, terms):
    parts = []
    for _ in range(terms):
        hi = t.astype(BF16)
        parts.append(hi)
        t = t - hi.astype(F32)
    return parts


def _rms(x, g, eps=NORM_EPS):
    return x * lax.rsqrt(jnp.mean(x * x, axis=-1, keepdims=True) + eps) * g


def _silu(x):
    return x * jax.nn.sigmoid(x)


def _tri_incl(n):
    r = lax.broadcasted_iota(jnp.int32, (n, n), 0)
    c = lax.broadcasted_iota(jnp.int32, (n, n), 1)
    return jnp.where(r >= c, 1.0, 0.0).astype(F32)


def _norm_matmul_kernel(x_ref, g_ref, w_ref, o_ref, h_ref):
    @pl.when(pl.program_id(1) == 0)
    def _():
        h_ref[...] = _rms(x_ref[...], g_ref[...]).astype(BF16)

    o_ref[...] = _dot(h_ref[...], w_ref[...]).astype(o_ref.dtype)


def norm_matmul(x, g, w, *, layer, tm, tn, out_dtype=F32, out_shape=None, out_map=None):
    m, k = x.shape
    n = w.shape[2]
    assert m % tm == 0 and n % tn == 0
    if out_shape is None:
        out_shape = (m, n)
    if out_map is None:
        out_map = lambda i, j: (i, j)
    return pl.pallas_call(
        _norm_matmul_kernel,
        grid=(m // tm, n // tn),
        in_specs=[pl.BlockSpec((tm, k), lambda i, j: (i, 0)),
                  pl.BlockSpec((1, k), lambda i, j: (0, 0)),
                  pl.BlockSpec((None, k, tn), lambda i, j: (layer, 0, j))],
        out_specs=pl.BlockSpec((tm, tn), out_map),
        out_shape=jax.ShapeDtypeStruct(out_shape, out_dtype),
        scratch_shapes=[pltpu.VMEM((tm, k), BF16)],
        compiler_params=_cparams("parallel", "arbitrary"),
        name="norm_matmul",
    )(x, g, w)


def _norm_matmul_layers_kernel(x_ref, g_ref, w_ref, o_ref, h_ref):
    @pl.when((pl.program_id(1) == 0) & (pl.program_id(2) == 0))
    def _():
        h_ref[...] = _rms(x_ref[...], g_ref[...]).astype(BF16)

    o_ref[...] = _dot(h_ref[...], w_ref[...]).astype(o_ref.dtype)


def norm_matmul_layers(x, g, w, *, tm, tn):
    m, k = x.shape
    layers, _, n = w.shape
    nj = n // tn
    return pl.pallas_call(
        _norm_matmul_layers_kernel,
        grid=(m // tm, layers, nj),
        in_specs=[pl.BlockSpec((tm, k), lambda i, l, j: (i, 0)),
                  pl.BlockSpec((1, k), lambda i, l, j: (0, 0)),
                  pl.BlockSpec((None, k, tn), lambda i, l, j: (l, 0, j))],
        out_specs=pl.BlockSpec((tm, tn), lambda i, l, j: (i, l * nj + j)),
        out_shape=jax.ShapeDtypeStruct((m, layers * n), BF16),
        scratch_shapes=[pltpu.VMEM((tm, k), BF16)],
        compiler_params=_cparams("parallel", "arbitrary", "arbitrary"),
        name="norm_matmul_layers",
    )(x, g, w)


def _norm_matmul2_kernel(x_ref, g_ref, wa_ref, wb_ref, oa_ref, ob_ref):
    h = _rms(x_ref[...], g_ref[...]).astype(BF16)
    oa_ref[...] = _dot(h, wa_ref[...])
    ob_ref[...] = _dot(h, wb_ref[...])


def norm_matmul2(x, g, wa, wb, *, layer, tm):
    m, k = x.shape
    na, nb = wa.shape[2], wb.shape[2]
    return pl.pallas_call(
        _norm_matmul2_kernel,
        grid=(m // tm,),
        in_specs=[pl.BlockSpec((tm, k), lambda i: (i, 0)),
                  pl.BlockSpec((1, k), lambda i: (0, 0)),
                  pl.BlockSpec((None, k, na), lambda i: (layer, 0, 0)),
                  pl.BlockSpec((None, k, nb), lambda i: (layer, 0, 0))],
        out_specs=(pl.BlockSpec((tm, na), lambda i: (i, 0)), pl.BlockSpec((tm, nb), lambda i: (i, 0))),
        out_shape=(jax.ShapeDtypeStruct((m, na), F32), jax.ShapeDtypeStruct((m, nb), F32)),
        compiler_params=_cparams("parallel"),
        name="norm_matmul2",
    )(x, g, wa, wb)


def _hgrn2_kernel(q_ref, f_ref, i_ref, g_ref, lbl_ref, og_ref, o_ref, st_ref, *, layer_j, n_chunks):
    c = HG_CHUNK
    halves = [c >> (i + 1) for i in range(c.bit_length() - 1)]

    @pl.when(pl.program_id(2) == 0)
    def _():
        st_ref[...] = jnp.zeros_like(st_ref)

    logits = lbl_ref[...]
    e = jnp.exp(logits - jnp.max(logits, axis=0, keepdims=True))
    p = e / jnp.sum(e, axis=0, keepdims=True)
    lb = jnp.zeros((1, p.shape[1]), F32)
    for r in range(1, layer_j + 1):
        lb = lb + p[r:r + 1, :]

    og = og_ref[...]
    rr = lax.broadcasted_iota(jnp.int32, (c, c), 0)
    cc = lax.broadcasted_iota(jnp.int32, (c, c), 1)
    row = lax.broadcasted_iota(jnp.int32, (c, HG_DK), 0)

    sums = [rr >= cc]
    pair_masks = []
    for hb in halves:
        blk = 2 * hb
        sums.append(cc <= jnp.bitwise_and(rr, -blk) + (hb - 1))
        same = jnp.bitwise_and(rr, -blk) == jnp.bitwise_and(cc, -blk)
        pair_masks.append(jnp.where(same, jnp.bitwise_and(rr, hb) - jnp.bitwise_and(cc, hb), 0) > 0)
    prefix = jnp.concatenate([jnp.where(m, 1.0, 0.0) for m in sums], axis=0).astype(BF16)

    chunks = range(n_chunks)
    rows = [slice(ci * c, (ci + 1) * c) for ci in chunks]
    qs, ks, ivs, cums = [], [], [], []
    for ci in chunks:
        fr = f_ref[rows[ci], :]
        log_sig = jnp.minimum(fr, 0.0) - jnp.log1p(jnp.exp(-jnp.abs(fr)))
        log_f = log_sig + jnp.log1p(lb * jnp.exp(jnp.minimum(-fr, HG_EXP_CLIP)))
        cums.append(sum(_dot(prefix, part) for part in _split_bf16(log_f, 3)))
        qs.append(_silu(q_ref[rows[ci], :]))
        ks.append((1.0 - lb) * jax.nn.sigmoid(-fr))
        ivs.append(i_ref[rows[ci], :])
    ls = [cm[:c] for cm in cums]

    a_mats = [jnp.zeros((c, c), F32) for _ in chunks]
    for li, hb in enumerate(halves):
        upper = jnp.bitwise_and(row, hb) != 0
        for ci in chunks:
            d = ls[ci] - cums[ci][(li + 1) * c:(li + 2) * c]
            ex = jnp.exp(jnp.where(upper, d, -d))
            prod = _dot_nt((qs[ci] * ex).astype(BF16), (ks[ci] * ex).astype(BF16))
            a_mats[ci] = a_mats[ci] + jnp.where(pair_masks[li], prod, 0.0)

    os_, kvs, qes, ends = [], [], [], []
    for ci in chunks:
        q, k, l, iv = qs[ci], ks[ci], ls[ci], ivs[ci]
        iv_b = iv.astype(BF16)
        l_end = l[c - 1:c, :]
        os_.append(_dot(a_mats[ci].astype(BF16), iv_b) + jnp.sum(q * k, axis=-1, keepdims=True) * iv)
        qes.append((q * jnp.exp(l)).astype(BF16))
        kvs.append(_dot_tn(iv_b, (k * jnp.exp(l_end - l)).astype(BF16)))
        ends.append(jnp.exp(l_end))

    st = st_ref[...]
    for ci in range(n_chunks):
        rs = slice(ci * c, (ci + 1) * c)
        o = os_[ci] + _dot_nt(qes[ci], st.astype(BF16))
        st = st * ends[ci] + kvs[ci]
        o_ref[rs, :] = (_rms(o, og) * _silu(g_ref[rs, :])).astype(o_ref.dtype)
    st_ref[...] = st


def hgrn2_mix(p_hg, lb_logits, onorm_g, *, batch, seq, layer_j, tt=256):
    n, cols = p_hg.shape
    width = cols // 4
    heads = width // HG_DK
    assert seq % tt == 0 and tt % HG_CHUNK == 0
    tb = seq // tt
    n_ab = lb_logits.shape[0]

    def col_spec(part):
        return pl.BlockSpec((tt, HG_DK), lambda b, h, t, part=part: (b * tb + t, part * heads + h))

    kern = functools.partial(_hgrn2_kernel, layer_j=layer_j, n_chunks=tt // HG_CHUNK)
    return pl.pallas_call(
        kern,
        grid=(batch, heads, tb),
        in_specs=[col_spec(0), col_spec(1), col_spec(2), col_spec(3),
                  pl.BlockSpec((n_ab, HG_DK), lambda b, h, t: (0, h)),
                  pl.BlockSpec((1, HG_DK), lambda b, h, t: (0, 0))],
        out_specs=pl.BlockSpec((tt, HG_DK), lambda b, h, t: (b * tb + t, h)),
        out_shape=jax.ShapeDtypeStruct((n, width), BF16),
        scratch_shapes=[pltpu.VMEM((HG_DK, HG_DK), F32)],
        compiler_params=_cparams("parallel", "parallel", "arbitrary"),
        name="hgrn2_mix",
    )(p_hg, p_hg, p_hg, p_hg, lb_logits, onorm_g)


def _round_robin(chains):
    chains = list(chains)
    while chains:
        alive = []
        for ch in chains:
            try:
                next(ch)
                alive.append(ch)
            except StopIteration:
                pass
        chains = alive
        yield


def _rwkv7_stream(p_ref, mu_ref, w0_ref, w2_ref, a0_ref, a2_ref, g2_ref, kk_ref, ka_ref,
                  rk_ref, gng_ref, gnb_ref, ones_ref, o_ref, prev_ref, st_ref, *, heads):
    c = RW_CHUNK
    n = RW_N
    assert c == n
    width = heads * n
    gh = RW_GROUP_HEADS
    gw = gh * n
    gr = gh * c
    shift = c.bit_length() - 1

    @pl.when(pl.program_id(1) == 0)
    def _():
        prev_ref[...] = jnp.zeros_like(prev_ref)
        st_ref[...] = jnp.zeros_like(st_ref)

    p = p_ref[...]
    row = lax.broadcasted_iota(jnp.int32, p.shape, 0)
    shifted = jnp.where(row == 0, prev_ref[0:1, :], pltpu.roll(p, 1, axis=0))
    prev_ref[0:1, :] = p[c - 1:c, :]
    p = p + (shifted - p) * mu_ref[...]

    r = p[:, 0:width]
    k = p[:, width:2 * width]
    v = p[:, 2 * width:3 * width]
    off = 3 * width
    wd = p[:, off:off + RW_DECAY_LORA]
    ad = p[:, off + RW_DECAY_LORA:off + RW_DECAY_LORA + RW_A_LORA]
    gd = p[:, off + RW_DECAY_LORA + RW_A_LORA:off + RW_DECAY_LORA + RW_A_LORA + RW_GATE_LORA]

    wz = w0_ref[...] + _dot(jnp.tanh(wd).astype(BF16), w2_ref[...])
    w = jnp.minimum(wz, 0.0) - jnp.log1p(jnp.exp(-jnp.abs(wz))) - 0.5
    lw = -jnp.exp(w)
    a = jax.nn.sigmoid(a0_ref[...] + _dot(ad.astype(BF16), a2_ref[...]))
    g = _dot(jax.nn.sigmoid(gd).astype(BF16), g2_ref[...])
    k2 = k * (1.0 + (a - 1.0) * ka_ref[...])

    ones = ones_ref[...]

    def head_sum(t):
        return sum(_dot(part, ones) for part in _split_bf16(t, 2))

    kk = k * kk_ref[...]
    kk = kk * lax.rsqrt(jnp.maximum(head_sum(kk * kk), 1e-24))
    b = kk * a

    tri = _tri_incl(c).astype(BF16)
    cum = sum(_dot(tri, part) for part in _split_bf16(lw, 3))
    cum_end = cum[c - 1:c, :]
    e_inv = jnp.exp(-cum)
    e_fin = jnp.exp(cum_end - cum)
    e_end = jnp.exp(cum_end)
    at = -kk * jnp.exp(cum - lw)
    rt = r * jnp.exp(cum)
    bt = b * e_inv
    kt = k2 * e_inv
    bf = b * e_fin
    kf = k2 * e_fin

    rb = lax.broadcasted_iota(jnp.int32, (gr, gw), 0)
    cb = lax.broadcasted_iota(jnp.int32, (gr, gw), 1)
    same = jnp.right_shift(rb, shift) == jnp.right_shift(cb, shift)
    t_in = jnp.bitwise_and(rb, c - 1)
    s_in = jnp.bitwise_and(cb, c - 1)
    strict = t_in > s_in
    incl = t_in >= s_in
    eye = jnp.where(rb == cb, 1.0, 0.0).astype(F32)

    def bd(t):
        return jnp.where(same, jnp.concatenate([t] * gh, axis=0), 0.0).astype(BF16)

    ngroups = heads // gh
    outs = [None] * ngroups

    def group_chain(gi):
        gs = slice(gi * gw, (gi + 1) * gw)
        art = jnp.concatenate([bd(at[:, gs]), bd(rt[:, gs])], axis=0)
        v_bd = bd(v[:, gs])
        g_b = _dot_nt(art, bd(bt[:, gs]))
        g_k = _dot_nt(art, bd(kt[:, gs]))
        yield
        a_ab = jnp.where(strict, g_b[:gr], 0.0)
        a_ak = jnp.where(strict, g_k[:gr], 0.0)
        a_rb = jnp.where(incl, g_b[gr:], 0.0)
        a_rk = jnp.where(incl, g_k[gr:], 0.0)

        tinv = eye + a_ab
        x = a_ab
        for _ in range(shift - 1):
            xb = x.astype(BF16)
            x = _dot(xb, xb)
            yield
            tinv = tinv + _dot(tinv.astype(BF16), x.astype(BF16))
            yield

        st = st_ref[gi]
        arm = _dot_nt(art, st.astype(BF16))
        akv = _dot(jnp.concatenate([a_ak, a_rk], axis=0).astype(BF16), v_bd)
        yield
        u_b = _dot(tinv.astype(BF16), (arm[:gr] + akv[:gr]).astype(BF16)).astype(BF16)
        yield
        o_bd = arm[gr:] + akv[gr:] + _dot(a_rb.astype(BF16), u_b)
        bk = jnp.concatenate([bd(bf[:, gs]), bd(kf[:, gs])], axis=0)
        st_ref[gi] = st * e_end[:, gs] + _dot_tn(jnp.concatenate([u_b, v_bd], axis=0), bk)
        o_g = o_bd[0:c]
        for hh in range(1, gh):
            o_g = o_g + o_bd[hh * c:(hh + 1) * c]
        outs[gi] = o_g

    yield
    yield from _round_robin([group_chain(gi) for gi in range(ngroups)])

    o = jnp.concatenate(outs, axis=-1)
    mean = head_sum(o) * (1.0 / n)
    dev = o - mean
    var = head_sum(dev * dev) * (1.0 / n)
    on = dev * lax.rsqrt(var + RW_GN_EPS) * gng_ref[...] + gnb_ref[...]
    bonus = head_sum(r * k2 * rk_ref[...]) * v
    o_ref[...] = ((on + bonus) * g).astype(o_ref.dtype)


def _rwkv7_kernel(p_ref, *refs, heads, streams):
    *param_refs, o_ref, prev_ref, st_ref = refs
    chains = [_rwkv7_stream(p_ref.at[bi], *param_refs, o_ref.at[bi], prev_ref.at[bi],
                            st_ref.at[bi], heads=heads) for bi in range(streams)]
    for _ in _round_robin(chains):
        pass


def rwkv7_mix(p_rw, mu, w0, w2, a0, a2, g2, k_k, k_a, r_k, gn_g, gn_b, *, batch, seq):
    n_tok, cols = p_rw.shape
    width = w0.shape[1]
    heads = width // RW_N
    assert seq % RW_CHUNK == 0 and heads % RW_GROUP_HEADS == 0
    cb = seq // RW_CHUNK
    head_id = jnp.arange(width, dtype=jnp.int32) // RW_N
    ones = (head_id[:, None] == head_id[None, :]).astype(BF16)
    full = lambda a: pl.BlockSpec(a.shape, lambda b, t: (0,) * a.ndim)
    params = (mu, w0, w2, a0, a2, g2, k_k, k_a, r_k, gn_g, gn_b, ones)
    gdim = RW_GROUP_HEADS * RW_N
    nb = RW_STREAMS if batch % RW_STREAMS == 0 else 1
    out = pl.pallas_call(
        functools.partial(_rwkv7_kernel, heads=heads, streams=nb),
        grid=(batch // nb, cb),
        in_specs=[pl.BlockSpec((nb, RW_CHUNK, cols), lambda b, t: (b, t, 0))]
                 + [full(a) for a in params],
        out_specs=pl.BlockSpec((nb, RW_CHUNK, width), lambda b, t: (b, t, 0)),
        out_shape=jax.ShapeDtypeStruct((batch, seq, width), BF16),
        scratch_shapes=[pltpu.VMEM((nb, SUBLANES, cols), F32),
                        pltpu.VMEM((nb, heads // RW_GROUP_HEADS, gdim, gdim), F32)],
        compiler_params=_cparams("parallel", "arbitrary"),
        name="rwkv7_mix",
    )(p_rw.reshape(batch, seq, cols), *params)
    return out.reshape(n_tok, width)


def _s5_param_kernel(are_ref, aim_ref, ldt_ref, bre_ref, bim_ref,
                     abre_ref, abim_ref, bbre_ref, bbim_ref):
    lam_re = jnp.minimum(are_ref[...], -1e-4)
    lam_im = aim_ref[...]
    dt = jnp.exp(ldt_ref[...])
    mag = jnp.exp(lam_re * dt)
    ab_re = mag * jnp.cos(lam_im * dt)
    ab_im = mag * jnp.sin(lam_im * dt)
    den = lam_re * lam_re + lam_im * lam_im
    coef_re = ((ab_re - 1.0) * lam_re + ab_im * lam_im) / den
    coef_im = (ab_im * lam_re - (ab_re - 1.0) * lam_im) / den
    abre_ref[...] = ab_re
    abim_ref[...] = ab_im
    b_re = bre_ref[...]
    b_im = bim_ref[...]
    bbre_ref[...] = coef_re * b_re - coef_im * b_im
    bbim_ref[...] = coef_re * b_im + coef_im * b_re


def s5_params(a_re, a_im, log_dt, b_re_t, b_im_t):
    g, _, p = a_re.shape
    m = b_re_t.shape[1]
    return pl.pallas_call(
        _s5_param_kernel,
        out_shape=(jax.ShapeDtypeStruct((g, 1, p), F32), jax.ShapeDtypeStruct((g, 1, p), F32),
                   jax.ShapeDtypeStruct((g, m, p), F32), jax.ShapeDtypeStruct((g, m, p), F32)),
        name="s5_params",
    )(a_re, a_im, log_dt, b_re_t, b_im_t)


def _gelu_tanh(y):
    return 0.5 * y * (1.0 + jnp.tanh(math.sqrt(2.0 / math.pi) * (y + 0.044715 * (y * y * y))))


def _s5_scan_kernel(u_ref, bb_ref, cre_ref, cim_ref, are_ref, aim_ref, d_ref, o_ref,
                    xs_ref, st_ref, *, lt, bsz):
    ns = S5_BLOCK_STATE

    @pl.when(pl.program_id(1) == 0)
    def _():
        st_ref[...] = jnp.zeros_like(st_ref)

    a_re = jnp.broadcast_to(are_ref[...], (bsz, ns))
    a_im = jnp.broadcast_to(aim_ref[...], (bsz, ns))
    sub = min(S5_SUB_STEPS, lt)
    nsub = lt // sub
    piece = max(sub // 2, 1)
    state = [st_ref[0], st_ref[1]]

    def project(k):
        for t0 in range(k * sub, (k + 1) * sub, piece):
            u2 = u_ref[t0:t0 + piece].reshape(piece * bsz, LANES)
            xs_ref[t0 * bsz:(t0 + piece) * bsz, :] = _dot(u2.astype(BF16), bb_ref[...])
            yield

    def recur(k):
        x_re, x_im = state
        for t in range(k * sub, (k + 1) * sub):
            rs = slice(t * bsz, (t + 1) * bsz)
            n_re = a_re * x_re - a_im * x_im + xs_ref[rs, 0:ns]
            n_im = a_re * x_im + a_im * x_re + xs_ref[rs, ns:2 * ns]
            xs_ref[rs, 0:ns] = n_re
            xs_ref[rs, ns:2 * ns] = n_im
            x_re, x_im = n_re, n_im
            if (t + 1) % S5_YIELD_STEPS == 0:
                yield
        state[0], state[1] = x_re, x_im

    def readout(k):
        for t0 in range(k * sub, (k + 1) * sub, piece):
            rs = slice(t0 * bsz, (t0 + piece) * bsz)
            u2 = u_ref[t0:t0 + piece].reshape(piece * bsz, LANES)
            y = (_dot(xs_ref[rs, 0:ns].astype(BF16), cre_ref[...])
                 - _dot(xs_ref[rs, ns:2 * ns].astype(BF16), cim_ref[...]))
            y = y + d_ref[...] * u2
            o_ref[t0:t0 + piece] = _gelu_tanh(y).reshape(piece, bsz, LANES)
            yield

    for k in range(nsub + 2):
        stages = []
        if k < nsub:
            stages.append(project(k))
        if 0 <= k - 1 < nsub:
            stages.append(recur(k - 1))
        if 0 <= k - 2 < nsub:
            stages.append(readout(k - 2))
        for _ in _round_robin(stages):
            pass

    st_ref[0] = state[0]
    st_ref[1] = state[1]


def s5_scan(u_tm, bb_blk, cre_blk, cim_blk, ab_re, ab_im, d, *, lt=256):
    seq, bsz, dm = u_tm.shape
    nblk = dm // LANES
    ns = S5_BLOCK_STATE
    assert seq % lt == 0
    return pl.pallas_call(
        functools.partial(_s5_scan_kernel, lt=lt, bsz=bsz),
        grid=(nblk, seq // lt),
        in_specs=[pl.BlockSpec((lt, bsz, LANES), lambda c, t: (t, 0, c)),
                  pl.BlockSpec((None, LANES, 2 * ns), lambda c, t: (c, 0, 0)),
                  pl.BlockSpec((None, ns, LANES), lambda c, t: (c, 0, 0)),
                  pl.BlockSpec((None, ns, LANES), lambda c, t: (c, 0, 0)),
                  pl.BlockSpec((None, 1, ns), lambda c, t: (c, 0, 0)),
                  pl.BlockSpec((None, 1, ns), lambda c, t: (c, 0, 0)),
                  pl.BlockSpec((1, LANES), lambda c, t: (0, c))],
        out_specs=pl.BlockSpec((lt, bsz, LANES), lambda c, t: (t, 0, c)),
        out_shape=jax.ShapeDtypeStruct((seq, bsz, dm), F32),
        scratch_shapes=[pltpu.VMEM((lt * bsz, 2 * ns), F32),
                        pltpu.VMEM((2, bsz, ns), F32)],
        compiler_params=_cparams("parallel", "arbitrary"),
        name="s5_scan",
    )(u_tm, bb_blk, cre_blk, cim_blk, ab_re, ab_im, d)


def _xattn_core(x, g_ref, wq_ref, k_ref, v_ref, wo_ref, heads):
    d = x.shape[1]
    hd = d // heads
    h = _rms(x, g_ref[...]).astype(BF16)
    q = _dot(h, wq_ref[...]).astype(BF16)
    outs = []
    for i in range(heads):
        sl = slice(i * hd, (i + 1) * hd)
        s = _dot_nt(q[:, sl], k_ref[:, sl]) * (hd ** -0.5)
        s = s - jnp.max(s, axis=-1, keepdims=True)
        e = jnp.exp(s)
        p = e / jnp.sum(e, axis=-1, keepdims=True)
        outs.append(_dot(p.astype(BF16), v_ref[:, sl]))
    o = jnp.concatenate(outs, axis=-1).astype(BF16)
    return x + _dot(o, wo_ref[...])


def _ab_out_xattn_kernel(ya_ref, yb_ref, wa_ref, wb_ref, x_ref, g_ref, wq_ref, k_ref, v_ref, wo_ref,
                         o_ref, *, heads):
    x = x_ref[...] + _dot(ya_ref[...], wa_ref[...]) + _dot(yb_ref[...], wb_ref[...])
    o_ref[...] = _xattn_core(x, g_ref, wq_ref, k_ref, v_ref, wo_ref, heads)


def _glu_out_xattn_kernel(y_ref, wglu_ref, x_ref, g_ref, wq_ref, k_ref, v_ref, wo_ref, o_ref, *, heads):
    d = x_ref.shape[1]
    z = _dot(y_ref[...].astype(BF16), wglu_ref[...])
    x = x_ref[...] + z[:, :d] * jax.nn.sigmoid(z[:, d:])
    o_ref[...] = _xattn_core(x, g_ref, wq_ref, k_ref, v_ref, wo_ref, heads)


def _xattn_specs(tm, d, n_mem, layer, rows):
    return [pl.BlockSpec((tm, d), rows),
            pl.BlockSpec((1, d), lambda b, t: (0, 0)),
            pl.BlockSpec((None, d, d), lambda b, t: (layer, 0, 0)),
            pl.BlockSpec((n_mem, d), lambda b, t: (b, 2 * layer)),
            pl.BlockSpec((n_mem, d), lambda b, t: (b, 2 * layer + 1)),
            pl.BlockSpec((None, d, d), lambda b, t: (layer, 0, 0))]


def ab_out_cross_attention(ya, yb, w_out, x, g, wq, kv, wo, *, layer_j, layer, batch, seq, n_mem, tm):
    n, d = x.shape
    ka, kb = ya.shape[1], yb.shape[1]
    assert ka == kb
    tb = seq // tm
    rows = lambda b, t: (b * tb + t, 0)
    xa = _xattn_specs(tm, d, n_mem, layer, rows)
    return pl.pallas_call(
        functools.partial(_ab_out_xattn_kernel, heads=XA_HEADS),
        grid=(batch, tb),
        in_specs=[pl.BlockSpec((tm, ka), rows), pl.BlockSpec((tm, kb), rows),
                  pl.BlockSpec((None, ka, d), lambda b, t: (layer_j, 0, 0)),
                  pl.BlockSpec((None, kb, d), lambda b, t: (layer_j, 1, 0))] + xa,
        out_specs=pl.BlockSpec((tm, d), rows),
        out_shape=jax.ShapeDtypeStruct((n, d), F32),
        compiler_params=_cparams("parallel", "parallel"),
        name="ab_out_cross_attention",
    )(ya, yb, w_out, w_out, x, g, wq, kv, kv, wo)


def glu_out_cross_attention(y_tm2d, w_glu, x, g, wq, kv, wo, *, layer_j, layer, batch, seq, n_mem, tm):
    n, d = x.shape
    tb = seq // tm
    rows = lambda b, t: (b * tb + t, 0)
    xa = _xattn_specs(tm, d, n_mem, layer, rows)
    return pl.pallas_call(
        functools.partial(_glu_out_xattn_kernel, heads=XA_HEADS),
        grid=(batch, tb),
        in_specs=[pl.BlockSpec((tm, d), lambda b, t: (t, b)),
                  pl.BlockSpec((None, d, 2 * d), lambda b, t: (layer_j, 0, 0))] + xa,
        out_specs=pl.BlockSpec((tm, d), rows),
        out_shape=jax.ShapeDtypeStruct((n, d), F32),
        compiler_params=_cparams("parallel", "parallel"),
        name="glu_out_cross_attention",
    )(y_tm2d, w_glu, x, g, wq, kv, kv, wo)


def _ffn_kernel(x_ref, g_ref, wg_ref, wu_ref, w2_ref, o_ref, h_ref, acc_ref):
    c = pl.program_id(1)

    @pl.when(c == 0)
    def _():
        h_ref[...] = _rms(x_ref[...], g_ref[...]).astype(BF16)
        acc_ref[...] = jnp.zeros_like(acc_ref)

    h = h_ref[...]
    a = _silu(_dot(h, wg_ref[...])) * _dot(h, wu_ref[...])
    acc_ref[...] += _dot(a.astype(BF16), w2_ref[...])

    @pl.when(c == pl.num_programs(1) - 1)
    def _():
        o_ref[...] = x_ref[...] + acc_ref[...]


def ffn_residual(x, g, w13, w2, *, layer, tm, tf):
    n, d = x.shape
    dff = w2.shape[1]
    nc = dff // tf
    return pl.pallas_call(
        _ffn_kernel,
        grid=(n // tm, nc),
        in_specs=[pl.BlockSpec((tm, d), lambda i, c: (i, 0)),
                  pl.BlockSpec((1, d), lambda i, c: (0, 0)),
                  pl.BlockSpec((None, d, tf), lambda i, c: (layer, 0, c)),
                  pl.BlockSpec((None, d, tf), lambda i, c: (layer, 0, nc + c)),
                  pl.BlockSpec((None, tf, d), lambda i, c: (layer, c, 0))],
        out_specs=pl.BlockSpec((tm, d), lambda i, c: (i, 0)),
        out_shape=jax.ShapeDtypeStruct((n, d), F32),
        scratch_shapes=[pltpu.VMEM((tm, d), BF16), pltpu.VMEM((tm, d), F32)],
        compiler_params=_cparams("parallel", "arbitrary"),
        name="ffn_residual",
    )(x, g, w13, w13, w2)


def _moe_route(logits_t, n_experts, rb):
    m = logits_t.shape[1]
    sub = lax.broadcasted_iota(jnp.int32, logits_t.shape, 0).astype(F32)
    neg = -jnp.inf
    m1 = jnp.max(logits_t, axis=0, keepdims=True)
    i1 = jnp.min(jnp.where(logits_t == m1, sub, float(n_experts)), axis=0, keepdims=True)
    first = sub == i1
    rest = jnp.where(first, neg, logits_t)
    m2 = jnp.max(rest, axis=0, keepdims=True)
    i2 = jnp.min(jnp.where(rest == m2, sub, float(n_experts)), axis=0, keepdims=True)
    second = sub == i2
    e2 = jnp.exp(m2 - m1)
    den = 1.0 + e2
    gate0 = 1.0 / den
    gate1 = e2 / den

    first_f = jnp.where(first, 1.0, 0.0)
    second_f = jnp.where(second, 1.0, 0.0)
    both = jnp.concatenate([first_f, second_f], axis=0)
    w = min(m, MOE_RANK_BLOCK)
    t_src = lax.broadcasted_iota(jnp.int32, (w, w), 0)
    t_dst = lax.broadcasted_iota(jnp.int32, (w, w), 1)
    before = jnp.where(t_src < t_dst, 1.0, 0.0).astype(BF16)
    cnt = jnp.zeros((2 * n_experts, 1), F32)
    ranks = []
    for kb in range(m // w):
        blk = both[:, kb * w:(kb + 1) * w]
        ranks.append(_dot(blk.astype(BF16), before) + cnt)
        cnt = cnt + jnp.sum(blk, axis=1, keepdims=True)
    ranks = jnp.concatenate(ranks, axis=1)
    cnt0 = cnt[:n_experts]
    cnt_e = cnt0 + cnt[n_experts:]
    padded = jnp.floor((cnt_e + (rb - 1)) * (1.0 / rb)) * rb
    sub_col = lax.broadcasted_iota(jnp.int32, (n_experts, 1), 0)
    start = jnp.zeros((n_experts, 1), F32)
    for ee in range(n_experts - 1):
        start = start + jnp.where(sub_col > ee, padded[ee:ee + 1, :], 0.0)
    dest0 = jnp.sum(first_f * (start + ranks[:n_experts]), axis=0, keepdims=True)
    dest1 = jnp.sum(second_f * (start + cnt0 + ranks[n_experts:]), axis=0, keepdims=True)
    return dest0, dest1, gate0, gate1, start, padded


def _moe_kernel(x_ref, g_ref, rt_ref, wg_ref, wu_ref, w2_ref, fg_ref, o_ref, xs_ref, gw_ref, y_ref,
                meta_ref, *, n_experts, rb, final_norm):
    c = pl.program_id(1)
    e = pl.program_id(2)
    n_rows = xs_ref.shape[0]
    tm = x_ref.shape[0]
    slab = MOE_SLAB_ROWS
    slabs_used = 2 * n_experts

    @pl.when((c == 0) & (e == 0))
    def _route():
        x = x_ref[...]
        hf = _rms(x, g_ref[...])
        o_ref[...] = x
        y_ref[...] = jnp.zeros_like(y_ref)
        h_b = hf.astype(BF16)
        h_lo = (hf - h_b.astype(F32)).astype(BF16)
        rt_hi, rt_lo = _split_bf16(rt_ref[...], 2)
        logits_t = _dot_nt(rt_hi, h_b) + _dot_nt(rt_hi, h_lo) + _dot_nt(rt_lo, h_b)
        dest0, dest1, gate0, gate1, start, padded = _moe_route(logits_t, n_experts, rb)
        for ee in range(n_experts):
            meta_ref[ee] = jnp.sum(start[ee:ee + 1, :]).astype(jnp.int32)
            meta_ref[n_experts + ee] = jnp.sum(padded[ee:ee + 1, :] * (1.0 / rb)).astype(jnp.int32)
        used = jnp.sum(jnp.floor((jnp.sum(padded, axis=0, keepdims=True) + (slab - 1)) * (1.0 / slab)))
        meta_ref[slabs_used] = used.astype(jnp.int32)

        nb = MOE_BUILD_ROWS

        def build(j, carry):
            r0 = pl.multiple_of(j * nb, nb)
            rows = (lax.broadcasted_iota(jnp.int32, (nb, tm), 0) + r0).astype(F32)
            hit0 = rows == dest0
            hit1 = rows == dest1
            onehot = jnp.where(hit0 | hit1, 1.0, 0.0).astype(BF16)
            xs_ref[pl.ds(r0, nb), :] = _dot(onehot, h_b).astype(BF16)
            gw_ref[pl.ds(r0, nb), :] = (jnp.where(hit0, gate0, 0.0)
                                        + jnp.where(hit1, gate1, 0.0)).astype(BF16)
            return carry

        lax.fori_loop(0, meta_ref[slabs_used] * (slab // nb), build, 0)

    seg_start = meta_ref[e]
    seg_blocks = meta_ref[n_experts + e]

    def expert_rows(r0, rows):
        xb = xs_ref[pl.ds(r0, rows), :]
        a = _silu(_dot(xb, wg_ref[...])) * _dot(xb, wu_ref[...])
        y = y_ref[pl.ds(r0, rows), :].astype(F32) + _dot(a.astype(BF16), w2_ref[...])
        y_ref[pl.ds(r0, rows), :] = y.astype(BF16)

    def expert_pair(j, carry):
        expert_rows(pl.multiple_of(seg_start + j * (2 * rb), rb), 2 * rb)
        return carry

    lax.fori_loop(0, seg_blocks // 2, expert_pair, 0)

    @pl.when(seg_blocks % 2 == 1)
    def _odd_block():
        expert_rows(pl.multiple_of(seg_start + (seg_blocks - 1) * rb, rb), rb)

    @pl.when((c == pl.num_programs(1) - 1) & (e == pl.num_programs(2) - 1))
    def _combine():
        for kb in range(n_rows // slab):
            @pl.when(kb < meta_ref[slabs_used])
            def _():
                rs = slice(kb * slab, (kb + 1) * slab)
                o_ref[...] += _dot_tn(gw_ref[rs, :], y_ref[rs, :])

        if final_norm:
            o_ref[...] = _rms(o_ref[...], fg_ref[...])


def moe_residual(x, g, router_t, w13, w2, final_g, *, layer, tm, tf, rb, final_norm):
    n, d = x.shape
    _, n_experts, dff, _ = w2.shape
    nc = dff // tf
    n_rows = -(-(2 * tm + n_experts * rb) // MOE_SLAB_ROWS) * MOE_SLAB_ROWS
    return pl.pallas_call(
        functools.partial(_moe_kernel, n_experts=n_experts, rb=rb, final_norm=final_norm),
        grid=(n // tm, nc, n_experts),
        in_specs=[pl.BlockSpec((tm, d), lambda i, c, e: (i, 0), pipeline_mode=pl.Buffered(1)),
                  pl.BlockSpec((1, d), lambda i, c, e: (0, 0)),
                  pl.BlockSpec((n_experts, d), lambda i, c, e: (0, 0)),
                  pl.BlockSpec((None, None, d, tf), lambda i, c, e: (layer, e, 0, c)),
                  pl.BlockSpec((None, None, d, tf), lambda i, c, e: (layer, e, 0, nc + c)),
                  pl.BlockSpec((None, None, tf, d), lambda i, c, e: (layer, e, c, 0)),
                  pl.BlockSpec((1, d), lambda i, c, e: (0, 0))],
        out_specs=pl.BlockSpec((tm, d), lambda i, c, e: (i, 0)),
        out_shape=jax.ShapeDtypeStruct((n, d), F32),
        scratch_shapes=[pltpu.VMEM((n_rows, d), BF16), pltpu.VMEM((n_rows, tm), BF16),
                        pltpu.VMEM((n_rows, d), BF16), pltpu.SMEM((2 * n_experts + 1,), jnp.int32)],
        compiler_params=_cparams("parallel", "arbitrary", "arbitrary"),
        name="moe_residual",
    )(x, g, router_t, w13, w13, w2, final_g)


def _final_norm_kernel(x_ref, g_ref, o_ref):
    o_ref[...] = _rms(x_ref[...], g_ref[...])


def final_norm(x, g, *, tm):
    n, d = x.shape
    return pl.pallas_call(
        _final_norm_kernel,
        grid=(n // tm,),
        in_specs=[pl.BlockSpec((tm, d), lambda i: (i, 0)), pl.BlockSpec((1, d), lambda i: (0, 0))],
        out_specs=pl.BlockSpec((tm, d), lambda i: (i, 0)),
        out_shape=jax.ShapeDtypeStruct((n, d), F32),
        compiler_params=_cparams("parallel"),
        name="final_norm",
    )(x, g)


def _row(v):
    return v.reshape(1, -1).astype(F32)


def _pad_cols(a, cols):
    return jnp.pad(a, ((0, 0), (0, cols - a.shape[1])))


def _row_tile(n, pref):
    t = min(pref, n)
    while n % t:
        t //= 2
    return t


def ab_mixer(x, norm_g, w_hg, w_rw, lb_logits, hg_onorm_g, rw_mu, rw_w0, rw_w2, rw_a0, rw_a2, rw_g2,
             rw_k_k, rw_k_a, rw_r_k, rw_gn_g, rw_gn_b, *, batch, seq, layer_j):
    n, d = x.shape
    rw_pad = w_rw.shape[2]
    p_hg, p_rw = norm_matmul2(x, _row(norm_g), w_hg, w_rw, layer=layer_j, tm=_row_tile(n, 512))
    y_a = hgrn2_mix(p_hg, lb_logits.astype(F32), _row(hg_onorm_g), batch=batch, seq=seq,
                    layer_j=layer_j, tt=min(HG_TIME_BLOCK, seq))
    y_b = rwkv7_mix(p_rw, _pad_cols(_row(rw_mu), rw_pad), _row(rw_w0), rw_w2.astype(BF16),
                    _row(rw_a0), rw_a2.astype(BF16), rw_g2.astype(BF16), _row(rw_k_k),
                    _row(rw_k_a), _row(rw_r_k), _row(rw_gn_g), _row(rw_gn_b),
                    batch=batch, seq=seq)
    return y_a, y_b


def _block_diag(t):
    nblk, gpb, r, c = t.shape
    eye = jnp.eye(gpb, dtype=t.dtype)
    return jnp.einsum('bgrc,gh->bgrhc', t, eye).reshape(nblk, gpb * r, gpb * c)


def s5_mixer(x, norm_g, w_in, a_re, a_im, log_dt, b_re, b_im, c_re, c_im, d_skip, *, batch, seq, layer_j):
    n, d = x.shape
    groups, p_state = a_re.shape
    gpb = S5_GROUPS_PER_BLOCK
    nblk = groups // gpb
    tm = _row_tile(seq, 512)
    tb = seq // tm
    u_tm = norm_matmul(x, _row(norm_g), w_in, layer=layer_j, tm=tm, tn=d, out_shape=(seq, batch * d),
                       out_map=lambda i, j: (i % tb, i // tb))
    ab_re, ab_im, bb_re, bb_im = s5_params(
        a_re.reshape(groups, 1, p_state).astype(F32), a_im.reshape(groups, 1, p_state).astype(F32),
        log_dt.reshape(groups, 1, 1).astype(F32),
        jnp.swapaxes(b_re, 1, 2).astype(F32), jnp.swapaxes(b_im, 1, 2).astype(F32))
    m = bb_re.shape[1]
    bb_blk = jnp.concatenate([_block_diag(bb_re.reshape(nblk, gpb, m, p_state)),
                              _block_diag(bb_im.reshape(nblk, gpb, m, p_state))], axis=-1)
    cre_blk = _block_diag(jnp.swapaxes(c_re, 1, 2).reshape(nblk, gpb, p_state, m))
    cim_blk = _block_diag(jnp.swapaxes(c_im, 1, 2).reshape(nblk, gpb, p_state, m))
    y_tm = s5_scan(u_tm.reshape(seq, batch, d), bb_blk.astype(BF16), cre_blk.astype(BF16),
                   cim_blk.astype(BF16), ab_re.reshape(nblk, 1, gpb * p_state),
                   ab_im.reshape(nblk, 1, gpb * p_state), _row(d_skip), lt=min(256, seq))
    return y_tm.reshape(seq, batch * d)


def kernel(x, mem, mix_norm_g, xattn_norm_g, ffn_norm_g, mem_norm_g, final_norm_g, ab_w_in, ab_w_out, hg_lb_logits, hg_onorm_g, rw_mu, rw_w0, rw_w2, rw_a0, rw_a2, rw_g2, rw_k_k, rw_k_a, rw_r_k, rw_gn_g, rw_gn_b, c_w_in, s5_a_re, s5_a_im, s5_log_dt, s5_b_re, s5_b_im, s5_c_re, s5_c_im, s5_d, c_w_glu, xa_wq, xa_wkv, xa_wo, ffn_w13, ffn_w2, moe_router, moe_w13, moe_w2):
    batch, seq, d = x.shape
    n_mem = mem.shape[1]
    depth = mix_norm_g.shape[0]
    n = batch * seq
    x = x.reshape(n, d).astype(F32)
    mem2 = mem.reshape(batch * n_mem, d).astype(F32)
    tm = _row_tile(n, 512)
    tm_seq = _row_tile(seq, 512)
    dff = ffn_w2.shape[1]
    tf = dff // 2 if (dff // 2) % LANES == 0 else dff

    hg_cols = 4 * hg_lb_logits.shape[1]
    rw_cols = ab_w_in.shape[2] - hg_cols
    rw_pad = -(-rw_cols // LANES) * LANES
    w_hg = ab_w_in[:, :, :hg_cols].astype(BF16)
    w_rw = jnp.pad(ab_w_in[:, :, hg_cols:], ((0, 0), (0, 0), (0, rw_pad - rw_cols))).astype(BF16)
    w_out_b, c_w_in_b, c_w_glu_b = ab_w_out.astype(BF16), c_w_in.astype(BF16), c_w_glu.astype(BF16)
    wq_b, wkv_b, wo_b = xa_wq.astype(BF16), xa_wkv.astype(BF16), xa_wo.astype(BF16)
    ffn_w13_b, ffn_w2_b = ffn_w13.astype(BF16), ffn_w2.astype(BF16)
    moe_w13_b, moe_w2_b = moe_w13.astype(BF16), moe_w2.astype(BF16)
    final_g = _row(final_norm_g)

    kv = norm_matmul_layers(mem2, _row(mem_norm_g), wkv_b, tm=_row_tile(batch * n_mem, 512), tn=d)

    for layer in range(depth):
        j = layer // 2
        xa_args = dict(layer_j=j, layer=layer, batch=batch, seq=seq, n_mem=n_mem, tm=tm_seq)
        if layer % 2 == 0:
            y_a, y_b = ab_mixer(x, mix_norm_g[layer], w_hg, w_rw, hg_lb_logits, hg_onorm_g[j],
                                rw_mu[j], rw_w0[j], rw_w2[j], rw_a0[j], rw_a2[j], rw_g2[j], rw_k_k[j],
                                rw_k_a[j], rw_r_k[j], rw_gn_g[j], rw_gn_b[j],
                                batch=batch, seq=seq, layer_j=j)
            x = ab_out_cross_attention(y_a, y_b, w_out_b, x, _row(xattn_norm_g[layer]), wq_b, kv, wo_b,
                                       **xa_args)
            x = ffn_residual(x, _row(ffn_norm_g[layer]), ffn_w13_b, ffn_w2_b, layer=j, tm=tm, tf=tf)
        else:
            y = s5_mixer(x, mix_norm_g[layer], c_w_in_b, s5_a_re[j], s5_a_im[j], s5_log_dt[j],
                         s5_b_re[j], s5_b_im[j], s5_c_re[j], s5_c_im[j], s5_d[j],
                         batch=batch, seq=seq, layer_j=j)
            x = glu_out_cross_attention(y, c_w_glu_b, x, _row(xattn_norm_g[layer]), wq_b, kv, wo_b,
                                        **xa_args)
            x = moe_residual(x, _row(ffn_norm_g[layer]), moe_router[j].T.astype(F32),
                             moe_w13_b, moe_w2_b, final_g, layer=j, tm=_row_tile(n, MOE_TOKEN_TILE),
                             tf=tf, rb=MOE_ROW_BLOCK, final_norm=layer == depth - 1)

    if depth % 2 == 1:
        x = final_norm(x, final_g, tm=tm)
    return x.reshape(batch, seq, d)
```

```python
import functools
import math

import jax
import jax.numpy as jnp
from jax import lax
from jax.experimental import pallas as pl
from jax.experimental.pallas import tpu as pltpu

F32 = jnp.float32
BF16 = jnp.bfloat16

NORM_EPS = 1e-6
LANES = 128
SUBLANES = 8
VMEM_LIMIT_BYTES = 60 * 1024 * 1024

HG_DK = 128
HG_CHUNK = 64
HG_EXP_CLIP = 60.0
HG_TIME_BLOCK = 512

RW_N = 64
RW_CHUNK = 64
RW_GROUP_HEADS = 4
RW_STREAMS = 4
RW_DECAY_LORA = 32
RW_A_LORA = 32
RW_GATE_LORA = 96
RW_GN_EPS = 64e-5

S5_GROUP = 16
S5_STATE = 64
S5_GROUPS_PER_BLOCK = LANES // S5_GROUP
S5_BLOCK_STATE = S5_GROUPS_PER_BLOCK * S5_STATE
S5_SUB_STEPS = 64
S5_YIELD_STEPS = 16

XA_HEADS = 4
MOE_TOKEN_TILE = 1024
MOE_ROW_BLOCK = 128
MOE_RANK_BLOCK = 256
MOE_SLAB_ROWS = 512
MOE_BUILD_ROWS = 256


def _cparams(*sem):
    return pltpu.CompilerParams(dimension_semantics=sem, vmem_limit_bytes=VMEM_LIMIT_BYTES)


def _dot(a, b, precision=None):
    return jnp.dot(a, b, preferred_element_type=F32, precision=precision)


def _dot_nt(a, b, precision=None):
    return lax.dot_general(a, b, (((1,), (1,)), ((), ())), preferred_element_type=F32,
                           precision=precision)


def _dot_tn(a, b, precision=None):
    return lax.dot_general(a, b, (((0,), (0,)), ((), ())), preferred_element_type=F32,
                           precision=precision)


def _split_bf16(t, terms):
    parts = []
    for _ in range(terms):
        hi = t.astype(BF16)
        parts.append(hi)
        t = t - hi.astype(F32)
    return parts


def _rms(x, g, eps=NORM_EPS):
    return x * lax.rsqrt(jnp.mean(x * x, axis=-1, keepdims=True) + eps) * g


def _silu(x):
    return x * jax.nn.sigmoid(x)


def _tri_incl(n):
    r = lax.broadcasted_iota(jnp.int32, (n, n), 0)
    c = lax.broadcasted_iota(jnp.int32, (n, n), 1)
    return jnp.where(r >= c, 1.0, 0.0).astype(F32)


def _norm_matmul_kernel(x_ref, g_ref, w_ref, o_ref, h_ref):
    @pl.when(pl.program_id(1) == 0)
    def _():
        h_ref[...] = _rms(x_ref[...], g_ref[...]).astype(BF16)

    o_ref[...] = _dot(h_ref[...], w_ref[...]).astype(o_ref.dtype)


def norm_matmul(x, g, w, *, layer, tm, tn, out_dtype=F32, out_shape=None, out_map=None):
    m, k = x.shape
    n = w.shape[2]
    assert m % tm == 0 and n % tn == 0
    if out_shape is None:
        out_shape = (m, n)
    if out_map is None:
        out_map = lambda i, j: (i, j)
    return pl.pallas_call(
        _norm_matmul_kernel,
        grid=(m // tm, n // tn),
        in_specs=[pl.BlockSpec((tm, k), lambda i, j: (i, 0)),
                  pl.BlockSpec((1, k), lambda i, j: (0, 0)),
                  pl.BlockSpec((None, k, tn), lambda i, j: (layer, 0, j))],
        out_specs=pl.BlockSpec((tm, tn), out_map),
        out_shape=jax.ShapeDtypeStruct(out_shape, out_dtype),
        scratch_shapes=[pltpu.VMEM((tm, k), BF16)],
        compiler_params=_cparams("parallel", "arbitrary"),
        name="norm_matmul",
    )(x, g, w)


def _norm_matmul_layers_kernel(x_ref, g_ref, w_ref, o_ref, h_ref):
    @pl.when((pl.program_id(1) == 0) & (pl.program_id(2) == 0))
    def _():
        h_ref[...] = _rms(x_ref[...], g_ref[...]).astype(BF16)

    o_ref[...] = _dot(h_ref[...], w_ref[...]).astype(o_ref.dtype)


def norm_matmul_layers(x, g, w, *, tm, tn):
    m, k = x.shape
    layers, _, n = w.shape
    nj = n // tn
    return pl.pallas_call(
        _norm_matmul_layers_kernel,
        grid=(m // tm, layers, nj),
        in_specs=[pl.BlockSpec((tm, k), lambda i, l, j: (i, 0)),
                  pl.BlockSpec((1, k), lambda i, l, j: (0, 0)),
                  pl.BlockSpec((None, k, tn), lambda i, l, j: (l, 0, j))],
        out_specs=pl.BlockSpec((tm, tn), lambda i, l, j: (i, l * nj + j)),
        out_shape=jax.ShapeDtypeStruct((m, layers * n), BF16),
        scratch_shapes=[pltpu.VMEM((tm, k), BF16)],
        compiler_params=_cparams("parallel", "arbitrary", "arbitrary"),
        name="norm_matmul_layers",
    )(x, g, w)


def _norm_matmul2_kernel(x_ref, g_ref, w_ref, oa_ref, ob_ref):
    h = _rms(x_ref[...], g_ref[...]).astype(BF16)
    na = oa_ref.shape[1]
    oa_ref[...] = _dot(h, w_ref[:, :na])
    ob_ref[...] = _dot(h, w_ref[:, na:])


def norm_matmul2(x, g, w, *, layer, split, tm):
    m, k = x.shape
    n = w.shape[2]
    assert split % LANES == 0 and (n - split) % LANES == 0
    return pl.pallas_call(
        _norm_matmul2_kernel,
        grid=(m // tm,),
        in_specs=[pl.BlockSpec((tm, k), lambda i: (i, 0)),
                  pl.BlockSpec((1, k), lambda i: (0, 0)),
                  pl.BlockSpec((None, k, n), lambda i: (layer, 0, 0))],
        out_specs=(pl.BlockSpec((tm, split), lambda i: (i, 0)),
                   pl.BlockSpec((tm, n - split), lambda i: (i, 0))),
        out_shape=(jax.ShapeDtypeStruct((m, split), F32), jax.ShapeDtypeStruct((m, n - split), F32)),
        compiler_params=_cparams("parallel"),
        name="norm_matmul2",
    )(x, g, w)


def _hgrn2_kernel(q_ref, f_ref, i_ref, g_ref, lbl_ref, og_ref, o_ref, st_ref, *, layer_j, n_chunks):
    c = HG_CHUNK
    halves = [c >> (i + 1) for i in range(c.bit_length() - 1)]

    @pl.when(pl.program_id(2) == 0)
    def _():
        st_ref[...] = jnp.zeros_like(st_ref)

    logits = lbl_ref[...]
    e = jnp.exp(logits - jnp.max(logits, axis=0, keepdims=True))
    p = e / jnp.sum(e, axis=0, keepdims=True)
    lb = jnp.zeros((1, p.shape[1]), F32)
    for r in range(1, layer_j + 1):
        lb = lb + p[r:r + 1, :]

    og = og_ref[...]
    rr = lax.broadcasted_iota(jnp.int32, (c, c), 0)
    cc = lax.broadcasted_iota(jnp.int32, (c, c), 1)
    row = lax.broadcasted_iota(jnp.int32, (c, HG_DK), 0)

    sums = [rr >= cc]
    pair_masks = []
    for hb in halves:
        blk = 2 * hb
        sums.append(cc <= jnp.bitwise_and(rr, -blk) + (hb - 1))
        same = jnp.bitwise_and(rr, -blk) == jnp.bitwise_and(cc, -blk)
        pair_masks.append(jnp.where(same, jnp.bitwise_and(rr, hb) - jnp.bitwise_and(cc, hb), 0) > 0)
    prefix = jnp.concatenate([jnp.where(m, 1.0, 0.0) for m in sums], axis=0).astype(BF16)

    chunks = range(n_chunks)
    rows = [slice(ci * c, (ci + 1) * c) for ci in chunks]
    qs, ks, ivs, cums = [], [], [], []
    for ci in chunks:
        fr = f_ref[rows[ci], :]
        log_sig = jnp.minimum(fr, 0.0) - jnp.log1p(jnp.exp(-jnp.abs(fr)))
        log_f = log_sig + jnp.log1p(lb * jnp.exp(jnp.minimum(-fr, HG_EXP_CLIP)))
        cums.append(sum(_dot(prefix, part) for part in _split_bf16(log_f, 3)))
        qs.append(_silu(q_ref[rows[ci], :]))
        ks.append((1.0 - lb) * jax.nn.sigmoid(-fr))
        ivs.append(i_ref[rows[ci], :])
    ls = [cm[:c] for cm in cums]

    a_mats = [jnp.zeros((c, c), F32) for _ in chunks]
    for li, hb in enumerate(halves):
        upper = jnp.bitwise_and(row, hb) != 0
        for ci in chunks:
            d = ls[ci] - cums[ci][(li + 1) * c:(li + 2) * c]
            ex = jnp.exp(jnp.where(upper, d, -d))
            prod = _dot_nt((qs[ci] * ex).astype(BF16), (ks[ci] * ex).astype(BF16))
            a_mats[ci] = a_mats[ci] + jnp.where(pair_masks[li], prod, 0.0)

    os_, kvs, qes, ends = [], [], [], []
    for ci in chunks:
        q, k, l, iv = qs[ci], ks[ci], ls[ci], ivs[ci]
        iv_b = iv.astype(BF16)
        l_end = l[c - 1:c, :]
        os_.append(_dot(a_mats[ci].astype(BF16), iv_b) + jnp.sum(q * k, axis=-1, keepdims=True) * iv)
        qes.append((q * jnp.exp(l)).astype(BF16))
        kvs.append(_dot_tn(iv_b, (k * jnp.exp(l_end - l)).astype(BF16)))
        ends.append(jnp.exp(l_end))

    st = st_ref[...]
    for ci in range(n_chunks):
        rs = slice(ci * c, (ci + 1) * c)
        o = os_[ci] + _dot_nt(qes[ci], st.astype(BF16))
        st = st * ends[ci] + kvs[ci]
        o_ref[rs, :] = (_rms(o, og) * _silu(g_ref[rs, :])).astype(o_ref.dtype)
    st_ref[...] = st


def hgrn2_mix(p_hg, lb_logits, onorm_g, *, batch, seq, layer_j, tt=256):
    n, cols = p_hg.shape
    width = cols // 4
    heads = width // HG_DK
    assert seq % tt == 0 and tt % HG_CHUNK == 0
    tb = seq // tt
    n_ab = lb_logits.shape[0]

    def col_spec(part):
        return pl.BlockSpec((tt, HG_DK), lambda b, h, t, part=part: (b * tb + t, part * heads + h))

    kern = functools.partial(_hgrn2_kernel, layer_j=layer_j, n_chunks=tt // HG_CHUNK)
    return pl.pallas_call(
        kern,
        grid=(batch, heads, tb),
        in_specs=[col_spec(0), col_spec(1), col_spec(2), col_spec(3),
                  pl.BlockSpec((n_ab, HG_DK), lambda b, h, t: (0, h)),
                  pl.BlockSpec((1, HG_DK), lambda b, h, t: (0, 0))],
        out_specs=pl.BlockSpec((tt, HG_DK), lambda b, h, t: (b * tb + t, h)),
        out_shape=jax.ShapeDtypeStruct((n, width), BF16),
        scratch_shapes=[pltpu.VMEM((HG_DK, HG_DK), F32)],
        compiler_params=_cparams("parallel", "parallel", "arbitrary"),
        name="hgrn2_mix",
    )(p_hg, p_hg, p_hg, p_hg, lb_logits, onorm_g)


def _round_robin(chains):
    chains = list(chains)
    while chains:
        alive = []
        for ch in chains:
            try:
                next(ch)
                alive.append(ch)
            except StopIteration:
                pass
        chains = alive
        yield


def _rwkv7_stream(p_ref, mu_ref, w0_ref, w2_ref, a0_ref, a2_ref, g2_ref, kk_ref, ka_ref,
                  rk_ref, gng_ref, gnb_ref, ones_ref, o_ref, prev_ref, st_ref, *, heads):
    c = RW_CHUNK
    n = RW_N
    assert c == n
    width = heads * n
    gh = RW_GROUP_HEADS
    gw = gh * n
    gr = gh * c
    shift = c.bit_length() - 1

    @pl.when(pl.program_id(1) == 0)
    def _():
        prev_ref[...] = jnp.zeros_like(prev_ref)
        st_ref[...] = jnp.zeros_like(st_ref)

    p = p_ref[...]
    row = lax.broadcasted_iota(jnp.int32, p.shape, 0)
    shifted = jnp.where(row == 0, prev_ref[0:1, :], pltpu.roll(p, 1, axis=0))
    prev_ref[0:1, :] = p[c - 1:c, :]
    p = p + (shifted - p) * mu_ref[...]

    r = p[:, 0:width]
    k = p[:, width:2 * width]
    v = p[:, 2 * width:3 * width]
    off = 3 * width
    wd = p[:, off:off + RW_DECAY_LORA]
    ad = p[:, off + RW_DECAY_LORA:off + RW_DECAY_LORA + RW_A_LORA]
    gd = p[:, off + RW_DECAY_LORA + RW_A_LORA:off + RW_DECAY_LORA + RW_A_LORA + RW_GATE_LORA]

    wz = w0_ref[...] + _dot(jnp.tanh(wd).astype(BF16), w2_ref[...])
    w = jnp.minimum(wz, 0.0) - jnp.log1p(jnp.exp(-jnp.abs(wz))) - 0.5
    lw = -jnp.exp(w)
    a = jax.nn.sigmoid(a0_ref[...] + _dot(ad.astype(BF16), a2_ref[...]))
    g = _dot(jax.nn.sigmoid(gd).astype(BF16), g2_ref[...])
    k2 = k * (1.0 + (a - 1.0) * ka_ref[...])

    ones = ones_ref[...]

    def head_sum(t):
        return sum(_dot(part, ones) for part in _split_bf16(t, 2))

    kk = k * kk_ref[...]
    kk = kk * lax.rsqrt(jnp.maximum(head_sum(kk * kk), 1e-24))
    b = kk * a

    tri = _tri_incl(c).astype(BF16)
    cum = sum(_dot(tri, part) for part in _split_bf16(lw, 3))
    cum_end = cum[c - 1:c, :]
    e_inv = jnp.exp(-cum)
    e_fin = jnp.exp(cum_end - cum)
    e_end = jnp.exp(cum_end)
    at = -kk * jnp.exp(cum - lw)
    rt = r * jnp.exp(cum)
    bt = b * e_inv
    kt = k2 * e_inv
    bf = b * e_fin
    kf = k2 * e_fin

    rb = lax.broadcasted_iota(jnp.int32, (gr, gw), 0)
    cb = lax.broadcasted_iota(jnp.int32, (gr, gw), 1)
    same = jnp.right_shift(rb, shift) == jnp.right_shift(cb, shift)
    t_in = jnp.bitwise_and(rb, c - 1)
    s_in = jnp.bitwise_and(cb, c - 1)
    strict = t_in > s_in
    incl = t_in >= s_in
    eye = jnp.where(rb == cb, 1.0, 0.0).astype(F32)

    def bd(t):
        return jnp.where(same, jnp.concatenate([t] * gh, axis=0), 0.0).astype(BF16)

    ngroups = heads // gh
    outs = [None] * ngroups

    def group_chain(gi):
        gs = slice(gi * gw, (gi + 1) * gw)
        art = jnp.concatenate([bd(at[:, gs]), bd(rt[:, gs])], axis=0)
        v_bd = bd(v[:, gs])
        g_b = _dot_nt(art, bd(bt[:, gs]))
        g_k = _dot_nt(art, bd(kt[:, gs]))
        yield
        a_ab = jnp.where(strict, g_b[:gr], 0.0)
        a_ak = jnp.where(strict, g_k[:gr], 0.0)
        a_rb = jnp.where(incl, g_b[gr:], 0.0)
        a_rk = jnp.where(incl, g_k[gr:], 0.0)

        tinv = eye + a_ab
        xb = a_ab.astype(BF16)
        for _ in range(shift - 1):
            xb = _dot(xb, xb).astype(BF16)
            yield
            tinv = tinv + _dot(tinv.astype(BF16), xb)
            yield

        st = st_ref[gi]
        arm = _dot_nt(art, st.astype(BF16))
        akv = _dot(jnp.concatenate([a_ak, a_rk], axis=0).astype(BF16), v_bd)
        yield
        u_b = _dot(tinv.astype(BF16), (arm[:gr] + akv[:gr]).astype(BF16)).astype(BF16)
        yield
        o_bd = arm[gr:] + akv[gr:] + _dot(a_rb.astype(BF16), u_b)
        bk = jnp.concatenate([bd(bf[:, gs]), bd(kf[:, gs])], axis=0)
        st_ref[gi] = st * e_end[:, gs] + _dot_tn(jnp.concatenate([u_b, v_bd], axis=0), bk)
        o_g = o_bd[0:c]
        for hh in range(1, gh):
            o_g = o_g + o_bd[hh * c:(hh + 1) * c]
        outs[gi] = o_g

    yield
    yield from _round_robin([group_chain(gi) for gi in range(ngroups)])

    o = jnp.concatenate(outs, axis=-1)
    mean = head_sum(o) * (1.0 / n)
    dev = o - mean
    var = head_sum(dev * dev) * (1.0 / n)
    on = dev * lax.rsqrt(var + RW_GN_EPS) * gng_ref[...] + gnb_ref[...]
    bonus = head_sum(r * k2 * rk_ref[...]) * v
    o_ref[...] = ((on + bonus) * g).astype(o_ref.dtype)


def _rwkv7_kernel(p_ref, *refs, heads, streams):
    *param_refs, o_ref, prev_ref, st_ref = refs
    chains = [_rwkv7_stream(p_ref.at[bi], *param_refs, o_ref.at[bi], prev_ref.at[bi],
                            st_ref.at[bi], heads=heads) for bi in range(streams)]
    for _ in _round_robin(chains):
        pass


def rwkv7_mix(p_rw, mu, w0, w2, a0, a2, g2, k_k, k_a, r_k, gn_g, gn_b, *, batch, seq):
    n_tok, cols = p_rw.shape
    width = w0.shape[1]
    heads = width // RW_N
    assert seq % RW_CHUNK == 0 and heads % RW_GROUP_HEADS == 0
    cb = seq // RW_CHUNK
    head_id = jnp.arange(width, dtype=jnp.int32) // RW_N
    ones = (head_id[:, None] == head_id[None, :]).astype(BF16)
    full = lambda a: pl.BlockSpec(a.shape, lambda b, t: (0,) * a.ndim)
    params = (mu, w0, w2, a0, a2, g2, k_k, k_a, r_k, gn_g, gn_b, ones)
    gdim = RW_GROUP_HEADS * RW_N
    nb = RW_STREAMS if batch % RW_STREAMS == 0 else 1
    out = pl.pallas_call(
        functools.partial(_rwkv7_kernel, heads=heads, streams=nb),
        grid=(batch // nb, cb),
        in_specs=[pl.BlockSpec((nb, RW_CHUNK, cols), lambda b, t: (b, t, 0))]
                 + [full(a) for a in params],
        out_specs=pl.BlockSpec((nb, RW_CHUNK, width), lambda b, t: (b, t, 0)),
        out_shape=jax.ShapeDtypeStruct((batch, seq, width), BF16),
        scratch_shapes=[pltpu.VMEM((nb, SUBLANES, cols), F32),
                        pltpu.VMEM((nb, heads // RW_GROUP_HEADS, gdim, gdim), F32)],
        compiler_params=_cparams("parallel", "arbitrary"),
        name="rwkv7_mix",
    )(p_rw.reshape(batch, seq, cols), *params)
    return out.reshape(n_tok, width)


def _s5_param_kernel(are_ref, aim_ref, ldt_ref, bre_ref, bim_ref,
                     abre_ref, abim_ref, bbre_ref, bbim_ref):
    lam_re = jnp.minimum(are_ref[...], -1e-4)
    lam_im = aim_ref[...]
    dt = jnp.exp(ldt_ref[...])
    mag = jnp.exp(lam_re * dt)
    ab_re = mag * jnp.cos(lam_im * dt)
    ab_im = mag * jnp.sin(lam_im * dt)
    den = lam_re * lam_re + lam_im * lam_im
    coef_re = ((ab_re - 1.0) * lam_re + ab_im * lam_im) / den
    coef_im = (ab_im * lam_re - (ab_re - 1.0) * lam_im) / den
    abre_ref[...] = ab_re
    abim_ref[...] = ab_im
    b_re = bre_ref[...]
    b_im = bim_ref[...]
    bbre_ref[...] = coef_re * b_re - coef_im * b_im
    bbim_ref[...] = coef_re * b_im + coef_im * b_re


def s5_params(a_re, a_im, log_dt, b_re_t, b_im_t):
    g, _, p = a_re.shape
    m = b_re_t.shape[1]
    return pl.pallas_call(
        _s5_param_kernel,
        out_shape=(jax.ShapeDtypeStruct((g, 1, p), F32), jax.ShapeDtypeStruct((g, 1, p), F32),
                   jax.ShapeDtypeStruct((g, m, p), F32), jax.ShapeDtypeStruct((g, m, p), F32)),
        name="s5_params",
    )(a_re, a_im, log_dt, b_re_t, b_im_t)


def _gelu_tanh(y):
    return 0.5 * y * (1.0 + jnp.tanh(math.sqrt(2.0 / math.pi) * (y + 0.044715 * (y * y * y))))


def _s5_scan_kernel(u_ref, bb_ref, cre_ref, cim_ref, are_ref, aim_ref, d_ref, o_ref,
                    xs_ref, st_ref, *, lt, bsz):
    ns = S5_BLOCK_STATE

    @pl.when(pl.program_id(1) == 0)
    def _():
        st_ref[...] = jnp.zeros_like(st_ref)

    a_re = jnp.broadcast_to(are_ref[...], (bsz, ns))
    a_im = jnp.broadcast_to(aim_ref[...], (bsz, ns))
    sub = min(S5_SUB_STEPS, lt)
    nsub = lt // sub
    piece = max(sub // 2, 1)
    state = [st_ref[0], st_ref[1]]

    def project(k):
        for t0 in range(k * sub, (k + 1) * sub, piece):
            u2 = u_ref[t0:t0 + piece].reshape(piece * bsz, LANES)
            xs_ref[t0 * bsz:(t0 + piece) * bsz, :] = _dot(u2.astype(BF16), bb_ref[...])
            yield

    def recur(k):
        x_re, x_im = state
        for t in range(k * sub, (k + 1) * sub):
            rs = slice(t * bsz, (t + 1) * bsz)
            n_re = a_re * x_re - a_im * x_im + xs_ref[rs, 0:ns]
            n_im = a_re * x_im + a_im * x_re + xs_ref[rs, ns:2 * ns]
            xs_ref[rs, 0:ns] = n_re
            xs_ref[rs, ns:2 * ns] = n_im
            x_re, x_im = n_re, n_im
            if (t + 1) % S5_YIELD_STEPS == 0:
                yield
        state[0], state[1] = x_re, x_im

    def readout(k):
        for t0 in range(k * sub, (k + 1) * sub, piece):
            rs = slice(t0 * bsz, (t0 + piece) * bsz)
            u2 = u_ref[t0:t0 + piece].reshape(piece * bsz, LANES)
            y = (_dot(xs_ref[rs, 0:ns].astype(BF16), cre_ref[...])
                 - _dot(xs_ref[rs, ns:2 * ns].astype(BF16), cim_ref[...]))
            y = y + d_ref[...] * u2
            o_ref[t0:t0 + piece] = _gelu_tanh(y).reshape(piece, bsz, LANES)
            yield

    for k in range(nsub + 2):
        stages = []
        if k < nsub:
            stages.append(project(k))
        if 0 <= k - 1 < nsub:
            stages.append(recur(k - 1))
        if 0 <= k - 2 < nsub:
            stages.append(readout(k - 2))
        for _ in _round_robin(stages):
            pass

    st_ref[0] = state[0]
    st_ref[1] = state[1]


def s5_scan(u_tm, bb_blk, cre_blk, cim_blk, ab_re, ab_im, d, *, lt=256):
    seq, bsz, dm = u_tm.shape
    nblk = dm // LANES
    ns = S5_BLOCK_STATE
    assert seq % lt == 0
    return pl.pallas_call(
        functools.partial(_s5_scan_kernel, lt=lt, bsz=bsz),
        grid=(nblk, seq // lt),
        in_specs=[pl.BlockSpec((lt, bsz, LANES), lambda c, t: (t, 0, c)),
                  pl.BlockSpec((None, LANES, 2 * ns), lambda c, t: (c, 0, 0)),
                  pl.BlockSpec((None, ns, LANES), lambda c, t: (c, 0, 0)),
                  pl.BlockSpec((None, ns, LANES), lambda c, t: (c, 0, 0)),
                  pl.BlockSpec((None, 1, ns), lambda c, t: (c, 0, 0)),
                  pl.BlockSpec((None, 1, ns), lambda c, t: (c, 0, 0)),
                  pl.BlockSpec((1, LANES), lambda c, t: (0, c))],
        out_specs=pl.BlockSpec((lt, bsz, LANES), lambda c, t: (t, 0, c)),
        out_shape=jax.ShapeDtypeStruct((seq, bsz, dm), F32),
        scratch_shapes=[pltpu.VMEM((lt * bsz, 2 * ns), F32),
                        pltpu.VMEM((2, bsz, ns), F32)],
        compiler_params=_cparams("parallel", "arbitrary"),
        name="s5_scan",
    )(u_tm, bb_blk, cre_blk, cim_blk, ab_re, ab_im, d)


def _xattn_core(x, g_ref, wq_ref, k_ref, v_ref, wo_ref, heads):
    d = x.shape[1]
    hd = d // heads
    h = _rms(x, g_ref[...]).astype(BF16)
    q = _dot(h, wq_ref[...]).astype(BF16)
    outs = []
    for i in range(heads):
        sl = slice(i * hd, (i + 1) * hd)
        s = _dot_nt(q[:, sl], k_ref[:, sl]) * (hd ** -0.5)
        s = s - jnp.max(s, axis=-1, keepdims=True)
        e = jnp.exp(s)
        p = e / jnp.sum(e, axis=-1, keepdims=True)
        outs.append(_dot(p.astype(BF16), v_ref[:, sl]))
    o = jnp.concatenate(outs, axis=-1).astype(BF16)
    return x + _dot(o, wo_ref[...])


def _ab_out_xattn_kernel(ya_ref, yb_ref, wa_ref, wb_ref, x_ref, g_ref, wq_ref, k_ref, v_ref, wo_ref,
                         o_ref, *, heads):
    x = x_ref[...] + _dot(ya_ref[...], wa_ref[...]) + _dot(yb_ref[...], wb_ref[...])
    o_ref[...] = _xattn_core(x, g_ref, wq_ref, k_ref, v_ref, wo_ref, heads)


def _glu_out_xattn_kernel(y_ref, wglu_ref, x_ref, g_ref, wq_ref, k_ref, v_ref, wo_ref, o_ref, *, heads):
    d = x_ref.shape[1]
    z = _dot(y_ref[...].astype(BF16), wglu_ref[...])
    x = x_ref[...] + z[:, :d] * jax.nn.sigmoid(z[:, d:])
    o_ref[...] = _xattn_core(x, g_ref, wq_ref, k_ref, v_ref, wo_ref, heads)


def _xattn_specs(tm, d, n_mem, layer, rows):
    return [pl.BlockSpec((tm, d), rows),
            pl.BlockSpec((1, d), lambda b, t: (0, 0)),
            pl.BlockSpec((None, d, d), lambda b, t: (layer, 0, 0)),
            pl.BlockSpec((n_mem, d), lambda b, t: (b, 2 * layer)),
            pl.BlockSpec((n_mem, d), lambda b, t: (b, 2 * layer + 1)),
            pl.BlockSpec((None, d, d), lambda b, t: (layer, 0, 0))]


def ab_out_cross_attention(ya, yb, w_out, x, g, wq, kv, wo, *, layer_j, layer, batch, seq, n_mem, tm):
    n, d = x.shape
    ka, kb = ya.shape[1], yb.shape[1]
    assert ka == kb
    tb = seq // tm
    rows = lambda b, t: (b * tb + t, 0)
    xa = _xattn_specs(tm, d, n_mem, layer, rows)
    return pl.pallas_call(
        functools.partial(_ab_out_xattn_kernel, heads=XA_HEADS),
        grid=(batch, tb),
        in_specs=[pl.BlockSpec((tm, ka), rows), pl.BlockSpec((tm, kb), rows),
                  pl.BlockSpec((None, ka, d), lambda b, t: (layer_j, 0, 0)),
                  pl.BlockSpec((None, kb, d), lambda b, t: (layer_j, 1, 0))] + xa,
        out_specs=pl.BlockSpec((tm, d), rows),
        out_shape=jax.ShapeDtypeStruct((n, d), F32),
        compiler_params=_cparams("parallel", "parallel"),
        name="ab_out_cross_attention",
    )(ya, yb, w_out, w_out, x, g, wq, kv, kv, wo)


def glu_out_cross_attention(y_tm2d, w_glu, x, g, wq, kv, wo, *, layer_j, layer, batch, seq, n_mem, tm):
    n, d = x.shape
    tb = seq // tm
    rows = lambda b, t: (b * tb + t, 0)
    xa = _xattn_specs(tm, d, n_mem, layer, rows)
    return pl.pallas_call(
        functools.partial(_glu_out_xattn_kernel, heads=XA_HEADS),
        grid=(batch, tb),
        in_specs=[pl.BlockSpec((tm, d), lambda b, t: (t, b)),
                  pl.BlockSpec((None, d, 2 * d), lambda b, t: (layer_j, 0, 0))] + xa,
        out_specs=pl.BlockSpec((tm, d), rows),
        out_shape=jax.ShapeDtypeStruct((n, d), F32),
        compiler_params=_cparams("parallel", "parallel"),
        name="glu_out_cross_attention",
    )(y_tm2d, w_glu, x, g, wq, kv, kv, wo)


def _ffn_kernel(x_ref, g_ref, wg_ref, wu_ref, w2_ref, *refs, n_cast):
    cast_in, (o_ref, *cast_out), (h_ref, acc_ref) = refs[:n_cast], refs[n_cast:2 * n_cast + 1], refs[-2:]
    c = pl.program_id(1)

    @pl.when(c == 0)
    def _():
        h_ref[...] = _rms(x_ref[...], g_ref[...]).astype(BF16)
        acc_ref[...] = jnp.zeros_like(acc_ref)

    h = h_ref[...]
    a = _silu(_dot(h, wg_ref[...])) * _dot(h, wu_ref[...])
    acc_ref[...] += _dot(a.astype(BF16), w2_ref[...])

    for src, dst in zip(cast_in, cast_out):
        dst[...] = src[...].astype(BF16)

    @pl.when(c == pl.num_programs(1) - 1)
    def _():
        o_ref[...] = x_ref[...] + acc_ref[...]


def ffn_residual(x, g, w13, w2, *, layer, tm, tf, cast=(), cast_layer=0):
    n, d = x.shape
    dff = w2.shape[1]
    nc = dff // tf
    steps = (n // tm) * nc
    cast_specs, cast_out_specs, cast_shapes = [], [], []
    for arr in cast:
        _, rows, cols = arr.shape
        assert rows % (steps * 2 * SUBLANES) == 0
        blk = rows // steps
        cast_specs.append(pl.BlockSpec((None, blk, cols), lambda i, c: (cast_layer, i * nc + c, 0)))
        cast_out_specs.append(pl.BlockSpec((blk, cols), lambda i, c: (i * nc + c, 0)))
        cast_shapes.append(jax.ShapeDtypeStruct((rows, cols), BF16))
    out = pl.pallas_call(
        functools.partial(_ffn_kernel, n_cast=len(cast)),
        grid=(n // tm, nc),
        in_specs=[pl.BlockSpec((tm, d), lambda i, c: (i, 0)),
                  pl.BlockSpec((1, d), lambda i, c: (0, 0)),
                  pl.BlockSpec((None, d, tf), lambda i, c: (layer, 0, c)),
                  pl.BlockSpec((None, d, tf), lambda i, c: (layer, 0, nc + c)),
                  pl.BlockSpec((None, tf, d), lambda i, c: (layer, c, 0))] + cast_specs,
        out_specs=[pl.BlockSpec((tm, d), lambda i, c: (i, 0))] + cast_out_specs,
        out_shape=[jax.ShapeDtypeStruct((n, d), F32)] + cast_shapes,
        scratch_shapes=[pltpu.VMEM((tm, d), BF16), pltpu.VMEM((tm, d), F32)],
        compiler_params=_cparams("parallel", "arbitrary"),
        name="ffn_residual",
    )(x, g, w13, w13, w2, *cast)
    return out[0], tuple(out[1:])


def _moe_route(logits_t, n_experts, rb):
    m = logits_t.shape[1]
    sub = lax.broadcasted_iota(jnp.int32, logits_t.shape, 0).astype(F32)
    neg = -jnp.inf
    m1 = jnp.max(logits_t, axis=0, keepdims=True)
    i1 = jnp.min(jnp.where(logits_t == m1, sub, float(n_experts)), axis=0, keepdims=True)
    first = sub == i1
    rest = jnp.where(first, neg, logits_t)
    m2 = jnp.max(rest, axis=0, keepdims=True)
    i2 = jnp.min(jnp.where(rest == m2, sub, float(n_experts)), axis=0, keepdims=True)
    second = sub == i2
    e2 = jnp.exp(m2 - m1)
    den = 1.0 + e2
    gate0 = 1.0 / den
    gate1 = e2 / den

    first_f = jnp.where(first, 1.0, 0.0)
    second_f = jnp.where(second, 1.0, 0.0)
    both = jnp.concatenate([first_f, second_f], axis=0)
    w = min(m, MOE_RANK_BLOCK)
    t_src = lax.broadcasted_iota(jnp.int32, (w, w), 0)
    t_dst = lax.broadcasted_iota(jnp.int32, (w, w), 1)
    before = jnp.where(t_src < t_dst, 1.0, 0.0).astype(BF16)
    cnt = jnp.zeros((2 * n_experts, 1), F32)
    ranks = []
    for kb in range(m // w):
        blk = both[:, kb * w:(kb + 1) * w]
        ranks.append(_dot(blk.astype(BF16), before) + cnt)
        cnt = cnt + jnp.sum(blk, axis=1, keepdims=True)
    ranks = jnp.concatenate(ranks, axis=1)
    cnt0 = cnt[:n_experts]
    cnt_e = cnt0 + cnt[n_experts:]
    padded = jnp.floor((cnt_e + (rb - 1)) * (1.0 / rb)) * rb
    sub_col = lax.broadcasted_iota(jnp.int32, (n_experts, 1), 0)
    start = jnp.zeros((n_experts, 1), F32)
    for ee in range(n_experts - 1):
        start = start + jnp.where(sub_col > ee, padded[ee:ee + 1, :], 0.0)
    dest0 = jnp.sum(first_f * (start + ranks[:n_experts]), axis=0, keepdims=True)
    dest1 = jnp.sum(second_f * (start + cnt0 + ranks[n_experts:]), axis=0, keepdims=True)
    return dest0, dest1, gate0, gate1, start, padded


def _moe_kernel(x_ref, g_ref, rt_ref, wg_ref, wu_ref, w2_ref, fg_ref, o_ref, xs_ref, gw_ref, y_ref,
                meta_ref, *, n_experts, rb, final_norm):
    c = pl.program_id(1)
    e = pl.program_id(2)
    n_rows = xs_ref.shape[0]
    tm = x_ref.shape[0]
    slab = MOE_SLAB_ROWS
    slabs_used = 2 * n_experts

    @pl.when((c == 0) & (e == 0))
    def _route():
        x = x_ref[...]
        hf = _rms(x, g_ref[...])
        o_ref[...] = x
        y_ref[...] = jnp.zeros_like(y_ref)
        h_b = hf.astype(BF16)
        h_lo = (hf - h_b.astype(F32)).astype(BF16)
        rt_hi, rt_lo = _split_bf16(rt_ref[...], 2)
        logits_t = _dot_nt(rt_hi, h_b) + _dot_nt(rt_hi, h_lo) + _dot_nt(rt_lo, h_b)
        dest0, dest1, gate0, gate1, start, padded = _moe_route(logits_t, n_experts, rb)
        for ee in range(n_experts):
            meta_ref[ee] = jnp.sum(start[ee:ee + 1, :]).astype(jnp.int32)
            meta_ref[n_experts + ee] = jnp.sum(padded[ee:ee + 1, :] * (1.0 / rb)).astype(jnp.int32)
        used = jnp.sum(jnp.floor((jnp.sum(padded, axis=0, keepdims=True) + (slab - 1)) * (1.0 / slab)))
        meta_ref[slabs_used] = used.astype(jnp.int32)

        nb = MOE_BUILD_ROWS

        def build(j, carry):
            r0 = pl.multiple_of(j * nb, nb)
            rows = (lax.broadcasted_iota(jnp.int32, (nb, tm), 0) + r0).astype(F32)
            hit0 = rows == dest0
            hit1 = rows == dest1
            onehot = jnp.where(hit0 | hit1, 1.0, 0.0).astype(BF16)
            xs_ref[pl.ds(r0, nb), :] = _dot(onehot, h_b).astype(BF16)
            gw_ref[pl.ds(r0, nb), :] = (jnp.where(hit0, gate0, 0.0)
                                        + jnp.where(hit1, gate1, 0.0)).astype(BF16)
            return carry

        lax.fori_loop(0, meta_ref[slabs_used] * (slab // nb), build, 0)

    seg_start = meta_ref[e]
    seg_blocks = meta_ref[n_experts + e]

    def expert_rows(r0, rows):
        xb = xs_ref[pl.ds(r0, rows), :]
        a = _silu(_dot(xb, wg_ref[...])) * _dot(xb, wu_ref[...])
        y = y_ref[pl.ds(r0, rows), :].astype(F32) + _dot(a.astype(BF16), w2_ref[...])
        y_ref[pl.ds(r0, rows), :] = y.astype(BF16)

    def expert_pair(j, carry):
        expert_rows(pl.multiple_of(seg_start + j * (2 * rb), rb), 2 * rb)
        return carry

    lax.fori_loop(0, seg_blocks // 2, expert_pair, 0)

    @pl.when(seg_blocks % 2 == 1)
    def _odd_block():
        expert_rows(pl.multiple_of(seg_start + (seg_blocks - 1) * rb, rb), rb)

    @pl.when((c == pl.num_programs(1) - 1) & (e == pl.num_programs(2) - 1))
    def _combine():
        for kb in range(n_rows // slab):
            @pl.when(kb < meta_ref[slabs_used])
            def _():
                rs = slice(kb * slab, (kb + 1) * slab)
                o_ref[...] += _dot_tn(gw_ref[rs, :], y_ref[rs, :])

        if final_norm:
            o_ref[...] = _rms(o_ref[...], fg_ref[...])


def moe_residual(x, g, router_t, w13, w2, final_g, *, layer, tm, tf, rb, final_norm):
    n, d = x.shape
    _, n_experts, dff, _ = w2.shape
    nc = dff // tf
    n_rows = -(-(2 * tm + n_experts * rb) // MOE_SLAB_ROWS) * MOE_SLAB_ROWS
    return pl.pallas_call(
        functools.partial(_moe_kernel, n_experts=n_experts, rb=rb, final_norm=final_norm),
        grid=(n // tm, nc, n_experts),
        in_specs=[pl.BlockSpec((tm, d), lambda i, c, e: (i, 0), pipeline_mode=pl.Buffered(1)),
                  pl.BlockSpec((1, d), lambda i, c, e: (0, 0)),
                  pl.BlockSpec((n_experts, d), lambda i, c, e: (0, 0)),
                  pl.BlockSpec((None, None, d, tf), lambda i, c, e: (layer, e, 0, c)),
                  pl.BlockSpec((None, None, d, tf), lambda i, c, e: (layer, e, 0, nc + c)),
                  pl.BlockSpec((None, None, tf, d), lambda i, c, e: (layer, e, c, 0)),
                  pl.BlockSpec((1, d), lambda i, c, e: (0, 0))],
        out_specs=pl.BlockSpec((tm, d), lambda i, c, e: (i, 0)),
        out_shape=jax.ShapeDtypeStruct((n, d), F32),
        scratch_shapes=[pltpu.VMEM((n_rows, d), BF16), pltpu.VMEM((n_rows, tm), BF16),
                        pltpu.VMEM((n_rows, d), BF16), pltpu.SMEM((2 * n_experts + 1,), jnp.int32)],
        compiler_params=_cparams("parallel", "arbitrary", "arbitrary"),
        name="moe_residual",
    )(x, g, router_t, w13, w13, w2, final_g)


def _final_norm_kernel(x_ref, g_ref, o_ref):
    o_ref[...] = _rms(x_ref[...], g_ref[...])


def final_norm(x, g, *, tm):
    n, d = x.shape
    return pl.pallas_call(
        _final_norm_kernel,
        grid=(n // tm,),
        in_specs=[pl.BlockSpec((tm, d), lambda i: (i, 0)), pl.BlockSpec((1, d), lambda i: (0, 0))],
        out_specs=pl.BlockSpec((tm, d), lambda i: (i, 0)),
        out_shape=jax.ShapeDtypeStruct((n, d), F32),
        compiler_params=_cparams("parallel"),
        name="final_norm",
    )(x, g)


def _row(v):
    return v.reshape(1, -1).astype(F32)


def _pad_cols(a, cols):
    return jnp.pad(a, ((0, 0), (0, cols - a.shape[1])))


def _row_tile(n, pref):
    t = min(pref, n)
    while n % t:
        t //= 2
    return t


def ab_mixer(x, norm_g, w_in, lb_logits, hg_onorm_g, rw_mu, rw_w0, rw_w2, rw_a0, rw_a2, rw_g2,
             rw_k_k, rw_k_a, rw_r_k, rw_gn_g, rw_gn_b, *, batch, seq, layer_j):
    n, d = x.shape
    hg_cols = 4 * lb_logits.shape[1]
    rw_pad = w_in.shape[2] - hg_cols
    p_hg, p_rw = norm_matmul2(x, _row(norm_g), w_in, layer=layer_j, split=hg_cols,
                              tm=_row_tile(n, 512))
    y_a = hgrn2_mix(p_hg, lb_logits.astype(F32), _row(hg_onorm_g), batch=batch, seq=seq,
                    layer_j=layer_j, tt=min(HG_TIME_BLOCK, seq))
    y_b = rwkv7_mix(p_rw, _pad_cols(_row(rw_mu), rw_pad), _row(rw_w0), rw_w2.astype(BF16),
                    _row(rw_a0), rw_a2.astype(BF16), rw_g2.astype(BF16), _row(rw_k_k),
                    _row(rw_k_a), _row(rw_r_k), _row(rw_gn_g), _row(rw_gn_b),
                    batch=batch, seq=seq)
    return y_a, y_b


def _block_diag(t):
    nblk, gpb, r, c = t.shape
    eye = jnp.eye(gpb, dtype=t.dtype)
    return jnp.einsum('bgrc,gh->bgrhc', t, eye).reshape(nblk, gpb * r, gpb * c)


def s5_mixer(x, norm_g, w_in, a_re, a_im, log_dt, b_re, b_im, c_re, c_im, d_skip, *, batch, seq, layer_j):
    n, d = x.shape
    groups, p_state = a_re.shape
    gpb = S5_GROUPS_PER_BLOCK
    nblk = groups // gpb
    tm = _row_tile(seq, 512)
    tb = seq // tm
    u_tm = norm_matmul(x, _row(norm_g), w_in, layer=layer_j, tm=tm, tn=d, out_shape=(seq, batch * d),
                       out_map=lambda i, j: (i % tb, i // tb))
    ab_re, ab_im, bb_re, bb_im = s5_params(
        a_re.reshape(groups, 1, p_state).astype(F32), a_im.reshape(groups, 1, p_state).astype(F32),
        log_dt.reshape(groups, 1, 1).astype(F32),
        jnp.swapaxes(b_re, 1, 2).astype(F32), jnp.swapaxes(b_im, 1, 2).astype(F32))
    m = bb_re.shape[1]
    bb_blk = jnp.concatenate([_block_diag(bb_re.reshape(nblk, gpb, m, p_state)),
                              _block_diag(bb_im.reshape(nblk, gpb, m, p_state))], axis=-1)
    cre_blk = _block_diag(jnp.swapaxes(c_re, 1, 2).reshape(nblk, gpb, p_state, m))
    cim_blk = _block_diag(jnp.swapaxes(c_im, 1, 2).reshape(nblk, gpb, p_state, m))
    y_tm = s5_scan(u_tm.reshape(seq, batch, d), bb_blk.astype(BF16), cre_blk.astype(BF16),
                   cim_blk.astype(BF16), ab_re.reshape(nblk, 1, gpb * p_state),
                   ab_im.reshape(nblk, 1, gpb * p_state), _row(d_skip), lt=min(256, seq))
    return y_tm.reshape(seq, batch * d)


def kernel(x, mem, mix_norm_g, xattn_norm_g, ffn_norm_g, mem_norm_g, final_norm_g, ab_w_in, ab_w_out, hg_lb_logits, hg_onorm_g, rw_mu, rw_w0, rw_w2, rw_a0, rw_a2, rw_g2, rw_k_k, rw_k_a, rw_r_k, rw_gn_g, rw_gn_b, c_w_in, s5_a_re, s5_a_im, s5_log_dt, s5_b_re, s5_b_im, s5_c_re, s5_c_im, s5_d, c_w_glu, xa_wq, xa_wkv, xa_wo, ffn_w13, ffn_w2, moe_router, moe_w13, moe_w2):
    batch, seq, d = x.shape
    n_mem = mem.shape[1]
    depth = mix_norm_g.shape[0]
    n = batch * seq
    x = x.reshape(n, d).astype(F32)
    mem2 = mem.reshape(batch * n_mem, d).astype(F32)
    tm = _row_tile(n, 512)
    tm_seq = _row_tile(seq, 512)
    dff = ffn_w2.shape[1]
    tf = dff // 2 if (dff // 2) % LANES == 0 else dff

    in_cols = ab_w_in.shape[2]
    w_in_b = jnp.pad(ab_w_in, ((0, 0), (0, 0), (0, -in_cols % LANES))).astype(BF16)
    w_out_b, c_w_in_b, c_w_glu_b = ab_w_out.astype(BF16), c_w_in.astype(BF16), c_w_glu.astype(BF16)
    wq_b, wkv_b, wo_b = xa_wq.astype(BF16), xa_wkv.astype(BF16), xa_wo.astype(BF16)
    ffn_w13_b, ffn_w2_b = ffn_w13.astype(BF16), ffn_w2.astype(BF16)
    n_moe, n_experts = moe_w13.shape[:2]
    moe_f32 = (moe_w13.reshape(n_moe, n_experts * d, 2 * dff), moe_w2.reshape(n_moe, n_experts * dff, d))
    ffn_steps = (n // tm) * (dff // tf)
    piggyback = all(a.shape[1] % (ffn_steps * 2 * SUBLANES) == 0 for a in moe_f32)
    moe_bf16 = {}
    final_g = _row(final_norm_g)

    kv = norm_matmul_layers(mem2, _row(mem_norm_g), wkv_b, tm=_row_tile(batch * n_mem, 512), tn=d)

    for layer in range(depth):
        j = layer // 2
        xa_args = dict(layer_j=j, layer=layer, batch=batch, seq=seq, n_mem=n_mem, tm=tm_seq)
        if layer % 2 == 0:
            y_a, y_b = ab_mixer(x, mix_norm_g[layer], w_in_b, hg_lb_logits, hg_onorm_g[j],
                                rw_mu[j], rw_w0[j], rw_w2[j], rw_a0[j], rw_a2[j], rw_g2[j], rw_k_k[j],
                                rw_k_a[j], rw_r_k[j], rw_gn_g[j], rw_gn_b[j],
                                batch=batch, seq=seq, layer_j=j)
            x = ab_out_cross_attention(y_a, y_b, w_out_b, x, _row(xattn_norm_g[layer]), wq_b, kv, wo_b,
                                       **xa_args)
            nxt = (layer + 1) // 2
            do_cast = piggyback and layer + 1 < depth
            x, cast_out = ffn_residual(x, _row(ffn_norm_g[layer]), ffn_w13_b, ffn_w2_b, layer=j, tm=tm,
                                       tf=tf, cast=moe_f32 if do_cast else (), cast_layer=nxt)
            if do_cast:
                moe_bf16[nxt] = cast_out
        else:
            y = s5_mixer(x, mix_norm_g[layer], c_w_in_b, s5_a_re[j], s5_a_im[j], s5_log_dt[j],
                         s5_b_re[j], s5_b_im[j], s5_c_re[j], s5_c_im[j], s5_d[j],
                         batch=batch, seq=seq, layer_j=j)
            x = glu_out_cross_attention(y, c_w_glu_b, x, _row(xattn_norm_g[layer]), wq_b, kv, wo_b,
                                        **xa_args)
            w13_b, w2_b = moe_bf16.get(j) or tuple(a[j].astype(BF16) for a in moe_f32)
            x = moe_residual(x, _row(ffn_norm_g[layer]), moe_router[j].T.astype(F32),
                             w13_b.reshape(1, n_experts, d, 2 * dff), w2_b.reshape(1, n_experts, dff, d),
                             final_g, layer=0, tm=_row_tile(n, MOE_TOKEN_TILE),
                             tf=tf, rb=MOE_ROW_BLOCK, final_norm=layer == depth - 1)

    if depth % 2 == 1:
        x = final_norm(x, final_g, tm=tm)
    return x.reshape(batch, seq, d)
```

```python
import functools
import math

import jax
import jax.numpy as jnp
from jax import lax
from jax.experimental import pallas as pl
from jax.experimental.pallas import tpu as pltpu

F32 = jnp.float32
BF16 = jnp.bfloat16

NORM_EPS = 1e-6
LANES = 128
SUBLANES = 8
VMEM_LIMIT_BYTES = 60 * 1024 * 1024

HG_DK = 128
HG_CHUNK = 64
HG_EXP_CLIP = 60.0
HG_TIME_BLOCK = 1024

RW_N = 64
RW_CHUNK = 64
RW_GROUP_HEADS = 4
RW_STREAMS = 2
RW_CHUNKS_PER_STEP = 4
RW_DECAY_LORA = 32
RW_A_LORA = 32
RW_GATE_LORA = 96
RW_GN_EPS = 64e-5

S5_GROUP = 16
S5_STATE = 64
S5_GROUPS_PER_BLOCK = LANES // S5_GROUP
S5_BLOCK_STATE = S5_GROUPS_PER_BLOCK * S5_STATE
S5_SUB_STEPS = 64
S5_YIELD_STEPS = 16

XA_HEADS = 4
MOE_TOKEN_TILE = 1024
MOE_ROW_BLOCK = 128
MOE_RANK_BLOCK = 256
MOE_SLAB_ROWS = 512
MOE_BUILD_ROWS = 256


def _cparams(*sem):
    return pltpu.CompilerParams(dimension_semantics=sem, vmem_limit_bytes=VMEM_LIMIT_BYTES)


def _dot(a, b, precision=None):
    return jnp.dot(a, b, preferred_element_type=F32, precision=precision)


def _dot_nt(a, b, precision=None):
    return lax.dot_general(a, b, (((1,), (1,)), ((), ())), preferred_element_type=F32,
                           precision=precision)


def _dot_tn(a, b, precision=None):
    return lax.dot_general(a, b, (((0,), (0,)), ((), ())), preferred_element_type=F32,
                           precision=precision)


def _split_bf16(t, terms):
    parts = []
    for _ in range(terms):
        hi = t.astype(BF16)
        parts.append(hi)
        t = t - hi.astype(F32)
    return parts


def _rms(x, g, eps=NORM_EPS):
    return x * lax.rsqrt(jnp.mean(x * x, axis=-1, keepdims=True) + eps) * g


def _silu(x):
    return x * jax.nn.sigmoid(x)


def _tri_incl(n):
    r = lax.broadcasted_iota(jnp.int32, (n, n), 0)
    c = lax.broadcasted_iota(jnp.int32, (n, n), 1)
    return jnp.where(r >= c, 1.0, 0.0).astype(F32)


def _norm_matmul_kernel(x_ref, g_ref, w_ref, o_ref, h_ref):
    @pl.when(pl.program_id(1) == 0)
    def _():
        h_ref[...] = _rms(x_ref[...], g_ref[...]).astype(BF16)

    o_ref[...] = _dot(h_ref[...], w_ref[...]).astype(o_ref.dtype)


def norm_matmul(x, g, w, *, layer, tm, tn, out_dtype=F32, out_shape=None, out_map=None):
    m, k = x.shape
    n = w.shape[2]
    assert m % tm == 0 and n % tn == 0
    if out_shape is None:
        out_shape = (m, n)
    if out_map is None:
        out_map = lambda i, j: (i, j)
    return pl.pallas_call(
        _norm_matmul_kernel,
        grid=(m // tm, n // tn),
        in_specs=[pl.BlockSpec((tm, k), lambda i, j: (i, 0)),
                  pl.BlockSpec((1, k), lambda i, j: (0, 0)),
                  pl.BlockSpec((None, k, tn), lambda i, j: (layer, 0, j))],
        out_specs=pl.BlockSpec((tm, tn), out_map),
        out_shape=jax.ShapeDtypeStruct(out_shape, out_dtype),
        scratch_shapes=[pltpu.VMEM((tm, k), BF16)],
        compiler_params=_cparams("parallel", "arbitrary"),
        name="norm_matmul",
    )(x, g, w)


def _norm_matmul_layers_kernel(x_ref, g_ref, w_ref, o_ref, h_ref):
    @pl.when((pl.program_id(1) == 0) & (pl.program_id(2) == 0))
    def _():
        h_ref[...] = _rms(x_ref[...], g_ref[...]).astype(BF16)

    o_ref[...] = _dot(h_ref[...], w_ref[...]).astype(o_ref.dtype)


def norm_matmul_layers(x, g, w, *, tm, tn):
    m, k = x.shape
    layers, _, n = w.shape
    nj = n // tn
    return pl.pallas_call(
        _norm_matmul_layers_kernel,
        grid=(m // tm, layers, nj),
        in_specs=[pl.BlockSpec((tm, k), lambda i, l, j: (i, 0)),
                  pl.BlockSpec((1, k), lambda i, l, j: (0, 0)),
                  pl.BlockSpec((None, k, tn), lambda i, l, j: (l, 0, j))],
        out_specs=pl.BlockSpec((tm, tn), lambda i, l, j: (i, l * nj + j)),
        out_shape=jax.ShapeDtypeStruct((m, layers * n), BF16),
        scratch_shapes=[pltpu.VMEM((tm, k), BF16)],
        compiler_params=_cparams("parallel", "arbitrary", "arbitrary"),
        name="norm_matmul_layers",
    )(x, g, w)


def _norm_matmul2_kernel(x_ref, g_ref, w_ref, oa_ref, ob_ref):
    h = _rms(x_ref[...], g_ref[...]).astype(BF16)
    na = oa_ref.shape[1]
    oa_ref[...] = _dot(h, w_ref[:, :na])
    ob_ref[...] = _dot(h, w_ref[:, na:])


def norm_matmul2(x, g, w, *, layer, split, tm):
    m, k = x.shape
    n = w.shape[2]
    assert split % LANES == 0 and (n - split) % LANES == 0
    return pl.pallas_call(
        _norm_matmul2_kernel,
        grid=(m // tm,),
        in_specs=[pl.BlockSpec((tm, k), lambda i: (i, 0)),
                  pl.BlockSpec((1, k), lambda i: (0, 0)),
                  pl.BlockSpec((None, k, n), lambda i: (layer, 0, 0))],
        out_specs=(pl.BlockSpec((tm, split), lambda i: (i, 0)),
                   pl.BlockSpec((tm, n - split), lambda i: (i, 0))),
        out_shape=(jax.ShapeDtypeStruct((m, split), F32), jax.ShapeDtypeStruct((m, n - split), F32)),
        compiler_params=_cparams("parallel"),
        name="norm_matmul2",
    )(x, g, w)


def _hgrn2_kernel(q_ref, f_ref, i_ref, g_ref, lbl_ref, og_ref, o_ref, st_ref, *, layer_j, n_chunks):
    c = HG_CHUNK
    halves = [c >> (i + 1) for i in range(c.bit_length() - 1)]

    @pl.when(pl.program_id(2) == 0)
    def _():
        st_ref[...] = jnp.zeros_like(st_ref)

    logits = lbl_ref[...]
    e = jnp.exp(logits - jnp.max(logits, axis=0, keepdims=True))
    p = e / jnp.sum(e, axis=0, keepdims=True)
    lb = jnp.zeros((1, p.shape[1]), F32)
    for r in range(1, layer_j + 1):
        lb = lb + p[r:r + 1, :]

    og = og_ref[...]
    rr = lax.broadcasted_iota(jnp.int32, (c, c), 0)
    cc = lax.broadcasted_iota(jnp.int32, (c, c), 1)
    row = lax.broadcasted_iota(jnp.int32, (c, HG_DK), 0)

    sums = [rr >= cc]
    pair_masks = []
    for hb in halves:
        blk = 2 * hb
        sums.append(cc <= jnp.bitwise_and(rr, -blk) + (hb - 1))
        same = jnp.bitwise_and(rr, -blk) == jnp.bitwise_and(cc, -blk)
        pair_masks.append(jnp.where(same, jnp.bitwise_and(rr, hb) - jnp.bitwise_and(cc, hb), 0) > 0)
    prefix = jnp.concatenate([jnp.where(m, 1.0, 0.0) for m in sums], axis=0).astype(BF16)

    chunks = range(n_chunks)
    rows = [slice(ci * c, (ci + 1) * c) for ci in chunks]
    qs, ks, ivs, cums = [], [], [], []
    for ci in chunks:
        fr = f_ref[rows[ci], :]
        log_sig = jnp.minimum(fr, 0.0) - jnp.log1p(jnp.exp(-jnp.abs(fr)))
        log_f = log_sig + jnp.log1p(lb * jnp.exp(jnp.minimum(-fr, HG_EXP_CLIP)))
        cums.append(sum(_dot(prefix, part) for part in _split_bf16(log_f, 3)))
        qs.append(_silu(q_ref[rows[ci], :]))
        ks.append((1.0 - lb) * jax.nn.sigmoid(-fr))
        ivs.append(i_ref[rows[ci], :])
    ls = [cm[:c] for cm in cums]

    a_mats = [jnp.zeros((c, c), F32) for _ in chunks]
    for li, hb in enumerate(halves):
        upper = jnp.bitwise_and(row, hb) != 0
        for ci in chunks:
            d = ls[ci] - cums[ci][(li + 1) * c:(li + 2) * c]
            ex = jnp.exp(jnp.where(upper, d, -d))
            prod = _dot_nt((qs[ci] * ex).astype(BF16), (ks[ci] * ex).astype(BF16))
            a_mats[ci] = a_mats[ci] + jnp.where(pair_masks[li], prod, 0.0)

    os_, kvs, qes, ends = [], [], [], []
    for ci in chunks:
        q, k, l, iv = qs[ci], ks[ci], ls[ci], ivs[ci]
        iv_b = iv.astype(BF16)
        l_end = l[c - 1:c, :]
        os_.append(_dot(a_mats[ci].astype(BF16), iv_b) + jnp.sum(q * k, axis=-1, keepdims=True) * iv)
        qes.append((q * jnp.exp(l)).astype(BF16))
        kvs.append(_dot_tn(iv_b, (k * jnp.exp(l_end - l)).astype(BF16)))
        ends.append(jnp.exp(l_end))

    st = st_ref[...]
    for ci in range(n_chunks):
        rs = slice(ci * c, (ci + 1) * c)
        o = os_[ci] + _dot_nt(qes[ci], st.astype(BF16))
        st = st * ends[ci] + kvs[ci]
        o_ref[rs, :] = (_rms(o, og) * _silu(g_ref[rs, :])).astype(o_ref.dtype)
    st_ref[...] = st


def hgrn2_mix(p_hg, lb_logits, onorm_g, *, batch, seq, layer_j, tt=256):
    n, cols = p_hg.shape
    width = cols // 4
    heads = width // HG_DK
    assert seq % tt == 0 and tt % HG_CHUNK == 0
    tb = seq // tt
    n_ab = lb_logits.shape[0]

    def col_spec(part):
        return pl.BlockSpec((tt, HG_DK), lambda b, h, t, part=part: (b * tb + t, part * heads + h))

    kern = functools.partial(_hgrn2_kernel, layer_j=layer_j, n_chunks=tt // HG_CHUNK)
    return pl.pallas_call(
        kern,
        grid=(batch, heads, tb),
        in_specs=[col_spec(0), col_spec(1), col_spec(2), col_spec(3),
                  pl.BlockSpec((n_ab, HG_DK), lambda b, h, t: (0, h)),
                  pl.BlockSpec((1, HG_DK), lambda b, h, t: (0, 0))],
        out_specs=pl.BlockSpec((tt, HG_DK), lambda b, h, t: (b * tb + t, h)),
        out_shape=jax.ShapeDtypeStruct((n, width), BF16),
        scratch_shapes=[pltpu.VMEM((HG_DK, HG_DK), F32)],
        compiler_params=_cparams("parallel", "parallel", "arbitrary"),
        name="hgrn2_mix",
    )(p_hg, p_hg, p_hg, p_hg, lb_logits, onorm_g)


def _round_robin(chains):
    chains = list(chains)
    while chains:
        alive = []
        for ch in chains:
            try:
                next(ch)
                alive.append(ch)
            except StopIteration:
                pass
        chains = alive
        yield


def _rwkv7_stream(p_ref, mu_ref, w0_ref, w2_ref, a0_ref, a2_ref, g2_ref, kk_ref, ka_ref,
                  rk_ref, gng_ref, gnb_ref, ones_ref, o_ref, prev_ref, st_ref, *, heads, chunks):
    c = RW_CHUNK
    n = RW_N
    assert c == n
    width = heads * n
    gh = RW_GROUP_HEADS
    gw = gh * n
    gr = gh * c
    ngroups = heads // gh
    shift = c.bit_length() - 1

    @pl.when(pl.program_id(1) == 0)
    def _():
        prev_ref[...] = jnp.zeros_like(prev_ref)
        st_ref[...] = jnp.zeros_like(st_ref)

    ones = ones_ref[...]

    def head_sum(t):
        return sum(_dot(part, ones) for part in _split_bf16(t, 2))

    tri = _tri_incl(c).astype(BF16)
    rb = lax.broadcasted_iota(jnp.int32, (gr, gw), 0)
    cb = lax.broadcasted_iota(jnp.int32, (gr, gw), 1)
    same = jnp.right_shift(rb, shift) == jnp.right_shift(cb, shift)
    t_in = jnp.bitwise_and(rb, c - 1)
    s_in = jnp.bitwise_and(cb, c - 1)
    strict = t_in > s_in
    incl = t_in >= s_in
    eye = jnp.where(rb == cb, 1.0, 0.0).astype(F32)

    def bd(t):
        return jnp.where(same, jnp.concatenate([t] * gh, axis=0), 0.0).astype(BF16)

    pre = [None] * chunks
    outs = [[None] * ngroups for _ in range(chunks)]

    def prepare(ci):
        p = p_ref[ci * c:(ci + 1) * c, :]
        before = prev_ref[0:1, :] if ci == 0 else p_ref[ci * c - 1:ci * c, :]
        row = lax.broadcasted_iota(jnp.int32, p.shape, 0)
        shifted = jnp.where(row == 0, before, pltpu.roll(p, 1, axis=0))
        p = p + (shifted - p) * mu_ref[...]

        r = p[:, 0:width]
        k = p[:, width:2 * width]
        v = p[:, 2 * width:3 * width]
        off = 3 * width
        wd = p[:, off:off + RW_DECAY_LORA]
        ad = p[:, off + RW_DECAY_LORA:off + RW_DECAY_LORA + RW_A_LORA]
        gd = p[:, off + RW_DECAY_LORA + RW_A_LORA:off + RW_DECAY_LORA + RW_A_LORA + RW_GATE_LORA]

        wz = w0_ref[...] + _dot(jnp.tanh(wd).astype(BF16), w2_ref[...])
        w = jnp.minimum(wz, 0.0) - jnp.log1p(jnp.exp(-jnp.abs(wz))) - 0.5
        lw = -jnp.exp(w)
        a = jax.nn.sigmoid(a0_ref[...] + _dot(ad.astype(BF16), a2_ref[...]))
        g = _dot(jax.nn.sigmoid(gd).astype(BF16), g2_ref[...])
        k2 = k * (1.0 + (a - 1.0) * ka_ref[...])
        yield

        kk = k * kk_ref[...]
        kk = kk * lax.rsqrt(jnp.maximum(head_sum(kk * kk), 1e-24))
        b = kk * a
        cum = sum(_dot(tri, part) for part in _split_bf16(lw, 3))
        cum_end = cum[c - 1:c, :]
        e_inv = jnp.exp(-cum)
        e_fin = jnp.exp(cum_end - cum)
        e_end = jnp.exp(cum_end)
        at = -kk * jnp.exp(cum - lw)
        rt = r * jnp.exp(cum)
        bt = b * e_inv
        kt = k2 * e_inv
        bf = b * e_fin
        kf = k2 * e_fin
        yield

        groups = []
        for gi in range(ngroups):
            gs = slice(gi * gw, (gi + 1) * gw)
            groups.append(dict(
                art=jnp.concatenate([bd(at[:, gs]), bd(rt[:, gs])], axis=0),
                bt=bd(bt[:, gs]), kt=bd(kt[:, gs]), v=bd(v[:, gs]),
                bk=jnp.concatenate([bd(bf[:, gs]), bd(kf[:, gs])], axis=0), e_end=e_end[:, gs]))
            yield
        pre[ci] = dict(groups=groups, g=g, v=v, rkk=r * k2 * rk_ref[...])

    def group_chain(ci, gi):
        q = pre[ci]["groups"][gi]
        art, v_bd = q["art"], q["v"]
        g_b = _dot_nt(art, q["bt"])
        g_k = _dot_nt(art, q["kt"])
        yield
        a_ab = jnp.where(strict, g_b[:gr], 0.0)
        a_ak = jnp.where(strict, g_k[:gr], 0.0)
        a_rb = jnp.where(incl, g_b[gr:], 0.0)
        a_rk = jnp.where(incl, g_k[gr:], 0.0)

        tinv = eye + a_ab
        xb = a_ab.astype(BF16)
        for _ in range(shift - 1):
            xb = _dot(xb, xb).astype(BF16)
            yield
            tinv = tinv + _dot(tinv.astype(BF16), xb)
            yield

        st = st_ref[gi]
        arm = _dot_nt(art, st.astype(BF16))
        akv = _dot(jnp.concatenate([a_ak, a_rk], axis=0).astype(BF16), v_bd)
        yield
        u_b = _dot(tinv.astype(BF16), (arm[:gr] + akv[:gr]).astype(BF16)).astype(BF16)
        yield
        o_bd = arm[gr:] + akv[gr:] + _dot(a_rb.astype(BF16), u_b)
        st_ref[gi] = st * q["e_end"] + _dot_tn(jnp.concatenate([u_b, v_bd], axis=0), q["bk"])
        o_g = o_bd[0:c]
        for hh in range(1, gh):
            o_g = o_g + o_bd[hh * c:(hh + 1) * c]
        outs[ci][gi] = o_g

    def finish(ci):
        q = pre[ci]
        o = jnp.concatenate(outs[ci], axis=-1)
        mean = head_sum(o) * (1.0 / n)
        dev = o - mean
        yield
        var = head_sum(dev * dev) * (1.0 / n)
        on = dev * lax.rsqrt(var + RW_GN_EPS) * gng_ref[...] + gnb_ref[...]
        yield
        bonus = head_sum(q["rkk"]) * q["v"]
        o_ref[ci * c:(ci + 1) * c, :] = ((on + bonus) * q["g"]).astype(o_ref.dtype)

    yield from prepare(0)
    for ci in range(chunks):
        stages = [_round_robin([group_chain(ci, gi) for gi in range(ngroups)])]
        if ci + 1 < chunks:
            stages.append(prepare(ci + 1))
        if ci >= 1:
            stages.append(finish(ci - 1))
        yield from _round_robin(stages)
    yield from finish(chunks - 1)
    prev_ref[0:1, :] = p_ref[chunks * c - 1:chunks * c, :]


def _rwkv7_kernel(p_ref, *refs, heads, streams, chunks):
    *param_refs, o_ref, prev_ref, st_ref = refs
    chains = [_rwkv7_stream(p_ref.at[bi], *param_refs, o_ref.at[bi], prev_ref.at[bi],
                            st_ref.at[bi], heads=heads, chunks=chunks) for bi in range(streams)]
    for _ in _round_robin(chains):
        pass


def rwkv7_mix(p_rw, mu, w0, w2, a0, a2, g2, k_k, k_a, r_k, gn_g, gn_b, *, batch, seq):
    n_tok, cols = p_rw.shape
    width = w0.shape[1]
    heads = width // RW_N
    assert seq % RW_CHUNK == 0 and heads % RW_GROUP_HEADS == 0
    cb = seq // RW_CHUNK
    head_id = jnp.arange(width, dtype=jnp.int32) // RW_N
    ones = (head_id[:, None] == head_id[None, :]).astype(BF16)
    full = lambda a: pl.BlockSpec(a.shape, lambda b, t: (0,) * a.ndim)
    params = (mu, w0, w2, a0, a2, g2, k_k, k_a, r_k, gn_g, gn_b, ones)
    gdim = RW_GROUP_HEADS * RW_N
    nb = RW_STREAMS if batch % RW_STREAMS == 0 else 1
    cps = RW_CHUNKS_PER_STEP if cb % RW_CHUNKS_PER_STEP == 0 else 1
    rows = cps * RW_CHUNK
    out = pl.pallas_call(
        functools.partial(_rwkv7_kernel, heads=heads, streams=nb, chunks=cps),
        grid=(batch // nb, cb // cps),
        in_specs=[pl.BlockSpec((nb, rows, cols), lambda b, t: (b, t, 0))]
                 + [full(a) for a in params],
        out_specs=pl.BlockSpec((nb, rows, width), lambda b, t: (b, t, 0)),
        out_shape=jax.ShapeDtypeStruct((batch, seq, width), BF16),
        scratch_shapes=[pltpu.VMEM((nb, SUBLANES, cols), F32),
                        pltpu.VMEM((nb, heads // RW_GROUP_HEADS, gdim, gdim), F32)],
        compiler_params=_cparams("parallel", "arbitrary"),
        name="rwkv7_mix",
    )(p_rw.reshape(batch, seq, cols), *params)
    return out.reshape(n_tok, width)


def _s5_param_kernel(are_ref, aim_ref, ldt_ref, bre_ref, bim_ref,
                     abre_ref, abim_ref, bbre_ref, bbim_ref):
    lam_re = jnp.minimum(are_ref[...], -1e-4)
    lam_im = aim_ref[...]
    dt = jnp.exp(ldt_ref[...])
    mag = jnp.exp(lam_re * dt)
    ab_re = mag * jnp.cos(lam_im * dt)
    ab_im = mag * jnp.sin(lam_im * dt)
    den = lam_re * lam_re + lam_im * lam_im
    coef_re = ((ab_re - 1.0) * lam_re + ab_im * lam_im) / den
    coef_im = (ab_im * lam_re - (ab_re - 1.0) * lam_im) / den
    abre_ref[...] = ab_re
    abim_ref[...] = ab_im
    b_re = bre_ref[...]
    b_im = bim_ref[...]
    bbre_ref[...] = coef_re * b_re - coef_im * b_im
    bbim_ref[...] = coef_re * b_im + coef_im * b_re


def s5_params(a_re, a_im, log_dt, b_re_t, b_im_t):
    g, _, p = a_re.shape
    m = b_re_t.shape[1]
    return pl.pallas_call(
        _s5_param_kernel,
        out_shape=(jax.ShapeDtypeStruct((g, 1, p), F32), jax.ShapeDtypeStruct((g, 1, p), F32),
                   jax.ShapeDtypeStruct((g, m, p), F32), jax.ShapeDtypeStruct((g, m, p), F32)),
        name="s5_params",
    )(a_re, a_im, log_dt, b_re_t, b_im_t)


def _gelu_tanh(y):
    return 0.5 * y * (1.0 + jnp.tanh(math.sqrt(2.0 / math.pi) * (y + 0.044715 * (y * y * y))))


def _s5_scan_kernel(u_ref, bb_ref, cre_ref, cim_ref, are_ref, aim_ref, d_ref, o_ref,
                    xs_ref, st_ref, *, lt, bsz):
    ns = S5_BLOCK_STATE

    @pl.when(pl.program_id(1) == 0)
    def _():
        st_ref[...] = jnp.zeros_like(st_ref)

    a_re = jnp.broadcast_to(are_ref[...], (bsz, ns))
    a_im = jnp.broadcast_to(aim_ref[...], (bsz, ns))
    sub = min(S5_SUB_STEPS, lt)
    nsub = lt // sub
    piece = max(sub // 2, 1)
    state = [st_ref[0], st_ref[1]]

    def project(k):
        for t0 in range(k * sub, (k + 1) * sub, piece):
            u2 = u_ref[t0:t0 + piece].reshape(piece * bsz, LANES)
            xs_ref[t0 * bsz:(t0 + piece) * bsz, :] = _dot(u2.astype(BF16), bb_ref[...])
            yield

    def recur(k):
        x_re, x_im = state
        for t in range(k * sub, (k + 1) * sub):
            rs = slice(t * bsz, (t + 1) * bsz)
            n_re = a_re * x_re - a_im * x_im + xs_ref[rs, 0:ns]
            n_im = a_re * x_im + a_im * x_re + xs_ref[rs, ns:2 * ns]
            xs_ref[rs, 0:ns] = n_re
            xs_ref[rs, ns:2 * ns] = n_im
            x_re, x_im = n_re, n_im
            if (t + 1) % S5_YIELD_STEPS == 0:
                yield
        state[0], state[1] = x_re, x_im

    def readout(k):
        for t0 in range(k * sub, (k + 1) * sub, piece):
            rs = slice(t0 * bsz, (t0 + piece) * bsz)
            u2 = u_ref[t0:t0 + piece].reshape(piece * bsz, LANES)
            y = (_dot(xs_ref[rs, 0:ns].astype(BF16), cre_ref[...])
                 - _dot(xs_ref[rs, ns:2 * ns].astype(BF16), cim_ref[...]))
            y = y + d_ref[...] * u2
            o_ref[t0:t0 + piece] = _gelu_tanh(y).reshape(piece, bsz, LANES)
            yield

    for k in range(nsub + 2):
        stages = []
        if k < nsub:
            stages.append(project(k))
        if 0 <= k - 1 < nsub:
            stages.append(recur(k - 1))
        if 0 <= k - 2 < nsub:
            stages.append(readout(k - 2))
        for _ in _round_robin(stages):
            pass

    st_ref[0] = state[0]
    st_ref[1] = state[1]


def s5_scan(u_tm, bb_blk, cre_blk, cim_blk, ab_re, ab_im, d, *, lt=256):
    seq, bsz, dm = u_tm.shape
    nblk = dm // LANES
    ns = S5_BLOCK_STATE
    assert seq % lt == 0
    return pl.pallas_call(
        functools.partial(_s5_scan_kernel, lt=lt, bsz=bsz),
        grid=(nblk, seq // lt),
        in_specs=[pl.BlockSpec((lt, bsz, LANES), lambda c, t: (t, 0, c)),
                  pl.BlockSpec((None, LANES, 2 * ns), lambda c, t: (c, 0, 0)),
                  pl.BlockSpec((None, ns, LANES), lambda c, t: (c, 0, 0)),
                  pl.BlockSpec((None, ns, LANES), lambda c, t: (c, 0, 0)),
                  pl.BlockSpec((None, 1, ns), lambda c, t: (c, 0, 0)),
                  pl.BlockSpec((None, 1, ns), lambda c, t: (c, 0, 0)),
                  pl.BlockSpec((1, LANES), lambda c, t: (0, c))],
        out_specs=pl.BlockSpec((lt, bsz, LANES), lambda c, t: (t, 0, c)),
        out_shape=jax.ShapeDtypeStruct((seq, bsz, dm), F32),
        scratch_shapes=[pltpu.VMEM((lt * bsz, 2 * ns), F32),
                        pltpu.VMEM((2, bsz, ns), F32)],
        compiler_params=_cparams("parallel", "arbitrary"),
        name="s5_scan",
    )(u_tm, bb_blk, cre_blk, cim_blk, ab_re, ab_im, d)


def _xattn_core(x, g_ref, wq_ref, k_ref, v_ref, wo_ref, heads):
    d = x.shape[1]
    hd = d // heads
    h = _rms(x, g_ref[...]).astype(BF16)
    q = _dot(h, wq_ref[...]).astype(BF16)
    outs = []
    for i in range(heads):
        sl = slice(i * hd, (i + 1) * hd)
        s = _dot_nt(q[:, sl], k_ref[:, sl]) * (hd ** -0.5)
        s = s - jnp.max(s, axis=-1, keepdims=True)
        e = jnp.exp(s)
        p = e / jnp.sum(e, axis=-1, keepdims=True)
        outs.append(_dot(p.astype(BF16), v_ref[:, sl]))
    o = jnp.concatenate(outs, axis=-1).astype(BF16)
    return x + _dot(o, wo_ref[...])


def _ab_out_xattn_kernel(ya_ref, yb_ref, wa_ref, wb_ref, x_ref, g_ref, wq_ref, k_ref, v_ref, wo_ref,
                         o_ref, *, heads):
    x = x_ref[...] + _dot(ya_ref[...], wa_ref[...]) + _dot(yb_ref[...], wb_ref[...])
    o_ref[...] = _xattn_core(x, g_ref, wq_ref, k_ref, v_ref, wo_ref, heads)


def _glu_out_xattn_kernel(y_ref, wglu_ref, x_ref, g_ref, wq_ref, k_ref, v_ref, wo_ref, o_ref, *, heads):
    d = x_ref.shape[1]
    z = _dot(y_ref[...].astype(BF16), wglu_ref[...])
    x = x_ref[...] + z[:, :d] * jax.nn.sigmoid(z[:, d:])
    o_ref[...] = _xattn_core(x, g_ref, wq_ref, k_ref, v_ref, wo_ref, heads)


def _xattn_specs(tm, d, n_mem, layer, rows):
    return [pl.BlockSpec((tm, d), rows),
            pl.BlockSpec((1, d), lambda b, t: (0, 0)),
            pl.BlockSpec((None, d, d), lambda b, t: (layer, 0, 0)),
            pl.BlockSpec((n_mem, d), lambda b, t: (b, 2 * layer)),
            pl.BlockSpec((n_mem, d), lambda b, t: (b, 2 * layer + 1)),
            pl.BlockSpec((None, d, d), lambda b, t: (layer, 0, 0))]


def ab_out_cross_attention(ya, yb, w_out, x, g, wq, kv, wo, *, layer_j, layer, batch, seq, n_mem, tm):
    n, d = x.shape
    ka, kb = ya.shape[1], yb.shape[1]
    assert ka == kb
    tb = seq // tm
    rows = lambda b, t: (b * tb + t, 0)
    xa = _xattn_specs(tm, d, n_mem, layer, rows)
    return pl.pallas_call(
        functools.partial(_ab_out_xattn_kernel, heads=XA_HEADS),
        grid=(batch, tb),
        in_specs=[pl.BlockSpec((tm, ka), rows), pl.BlockSpec((tm, kb), rows),
                  pl.BlockSpec((None, ka, d), lambda b, t: (layer_j, 0, 0)),
                  pl.BlockSpec((None, kb, d), lambda b, t: (layer_j, 1, 0))] + xa,
        out_specs=pl.BlockSpec((tm, d), rows),
        out_shape=jax.ShapeDtypeStruct((n, d), F32),
        compiler_params=_cparams("parallel", "parallel"),
        name="ab_out_cross_attention",
    )(ya, yb, w_out, w_out, x, g, wq, kv, kv, wo)


def glu_out_cross_attention(y_tm2d, w_glu, x, g, wq, kv, wo, *, layer_j, layer, batch, seq, n_mem, tm):
    n, d = x.shape
    tb = seq // tm
    rows = lambda b, t: (b * tb + t, 0)
    xa = _xattn_specs(tm, d, n_mem, layer, rows)
    return pl.pallas_call(
        functools.partial(_glu_out_xattn_kernel, heads=XA_HEADS),
        grid=(batch, tb),
        in_specs=[pl.BlockSpec((tm, d), lambda b, t: (t, b)),
                  pl.BlockSpec((None, d, 2 * d), lambda b, t: (layer_j, 0, 0))] + xa,
        out_specs=pl.BlockSpec((tm, d), rows),
        out_shape=jax.ShapeDtypeStruct((n, d), F32),
        compiler_params=_cparams("parallel", "parallel"),
        name="glu_out_cross_attention",
    )(y_tm2d, w_glu, x, g, wq, kv, kv, wo)


def _ffn_kernel(x_ref, g_ref, wg_ref, wu_ref, w2_ref, *refs, n_cast):
    cast_in, (o_ref, *cast_out), (h_ref, acc_ref) = refs[:n_cast], refs[n_cast:2 * n_cast + 1], refs[-2:]
    c = pl.program_id(1)

    @pl.when(c == 0)
    def _():
        h_ref[...] = _rms(x_ref[...], g_ref[...]).astype(BF16)
        acc_ref[...] = jnp.zeros_like(acc_ref)

    h = h_ref[...]
    a = _silu(_dot(h, wg_ref[...])) * _dot(h, wu_ref[...])
    acc_ref[...] += _dot(a.astype(BF16), w2_ref[...])

    for src, dst in zip(cast_in, cast_out):
        dst[...] = src[...].astype(BF16)

    @pl.when(c == pl.num_programs(1) - 1)
    def _():
        o_ref[...] = x_ref[...] + acc_ref[...]


def ffn_residual(x, g, w13, w2, *, layer, tm, tf, cast=(), cast_layer=0):
    n, d = x.shape
    dff = w2.shape[1]
    nc = dff // tf
    steps = (n // tm) * nc
    cast_specs, cast_out_specs, cast_shapes = [], [], []
    for arr in cast:
        _, rows, cols = arr.shape
        assert rows % (steps * 2 * SUBLANES) == 0
        blk = rows // steps
        cast_specs.append(pl.BlockSpec((None, blk, cols), lambda i, c: (cast_layer, i * nc + c, 0)))
        cast_out_specs.append(pl.BlockSpec((blk, cols), lambda i, c: (i * nc + c, 0)))
        cast_shapes.append(jax.ShapeDtypeStruct((rows, cols), BF16))
    out = pl.pallas_call(
        functools.partial(_ffn_kernel, n_cast=len(cast)),
        grid=(n // tm, nc),
        in_specs=[pl.BlockSpec((tm, d), lambda i, c: (i, 0)),
                  pl.BlockSpec((1, d), lambda i, c: (0, 0)),
                  pl.BlockSpec((None, d, tf), lambda i, c: (layer, 0, c)),
                  pl.BlockSpec((None, d, tf), lambda i, c: (layer, 0, nc + c)),
                  pl.BlockSpec((None, tf, d), lambda i, c: (layer, c, 0))] + cast_specs,
        out_specs=[pl.BlockSpec((tm, d), lambda i, c: (i, 0))] + cast_out_specs,
        out_shape=[jax.ShapeDtypeStruct((n, d), F32)] + cast_shapes,
        scratch_shapes=[pltpu.VMEM((tm, d), BF16), pltpu.VMEM((tm, d), F32)],
        compiler_params=_cparams("parallel", "arbitrary"),
        name="ffn_residual",
    )(x, g, w13, w13, w2, *cast)
    return out[0], tuple(out[1:])


def _moe_route(logits_t, n_experts, rb):
    m = logits_t.shape[1]
    sub = lax.broadcasted_iota(jnp.int32, logits_t.shape, 0).astype(F32)
    neg = -jnp.inf
    m1 = jnp.max(logits_t, axis=0, keepdims=True)
    i1 = jnp.min(jnp.where(logits_t == m1, sub, float(n_experts)), axis=0, keepdims=True)
    first = sub == i1
    rest = jnp.where(first, neg, logits_t)
    m2 = jnp.max(rest, axis=0, keepdims=True)
    i2 = jnp.min(jnp.where(rest == m2, sub, float(n_experts)), axis=0, keepdims=True)
    second = sub == i2
    e2 = jnp.exp(m2 - m1)
    den = 1.0 + e2
    gate0 = 1.0 / den
    gate1 = e2 / den

    first_f = jnp.where(first, 1.0, 0.0)
    second_f = jnp.where(second, 1.0, 0.0)
    both = jnp.concatenate([first_f, second_f], axis=0)
    w = min(m, MOE_RANK_BLOCK)
    t_src = lax.broadcasted_iota(jnp.int32, (w, w), 0)
    t_dst = lax.broadcasted_iota(jnp.int32, (w, w), 1)
    before = jnp.where(t_src < t_dst, 1.0, 0.0).astype(BF16)
    cnt = jnp.zeros((2 * n_experts, 1), F32)
    ranks = []
    for kb in range(m // w):
        blk = both[:, kb * w:(kb + 1) * w]
        ranks.append(_dot(blk.astype(BF16), before) + cnt)
        cnt = cnt + jnp.sum(blk, axis=1, keepdims=True)
    ranks = jnp.concatenate(ranks, axis=1)
    cnt0 = cnt[:n_experts]
    cnt_e = cnt0 + cnt[n_experts:]
    padded = jnp.floor((cnt_e + (rb - 1)) * (1.0 / rb)) * rb
    sub_col = lax.broadcasted_iota(jnp.int32, (n_experts, 1), 0)
    start = jnp.zeros((n_experts, 1), F32)
    for ee in range(n_experts - 1):
        start = start + jnp.where(sub_col > ee, padded[ee:ee + 1, :], 0.0)
    dest0 = jnp.sum(first_f * (start + ranks[:n_experts]), axis=0, keepdims=True)
    dest1 = jnp.sum(second_f * (start + cnt0 + ranks[n_experts:]), axis=0, keepdims=True)
    return dest0, dest1, gate0, gate1, start, padded


def _moe_kernel(x_ref, g_ref, rt_ref, wg_ref, wu_ref, w2_ref, fg_ref, o_ref, xs_ref, gw_ref, y_ref,
                meta_ref, *, n_experts, rb, final_norm):
    c = pl.program_id(1)
    e = pl.program_id(2)
    n_rows = xs_ref.shape[0]
    tm = x_ref.shape[0]
    slab = MOE_SLAB_ROWS
    slabs_used = 2 * n_experts

    @pl.when((c == 0) & (e == 0))
    def _route():
        x = x_ref[...]
        hf = _rms(x, g_ref[...])
        o_ref[...] = x
        y_ref[...] = jnp.zeros_like(y_ref)
        h_b = hf.astype(BF16)
        h_lo = (hf - h_b.astype(F32)).astype(BF16)
        rt_hi, rt_lo = _split_bf16(rt_ref[...], 2)
        logits_t = _dot_nt(rt_hi, h_b) + _dot_nt(rt_hi, h_lo) + _dot_nt(rt_lo, h_b)
        dest0, dest1, gate0, gate1, start, padded = _moe_route(logits_t, n_experts, rb)
        for ee in range(n_experts):
            meta_ref[ee] = jnp.sum(start[ee:ee + 1, :]).astype(jnp.int32)
            meta_ref[n_experts + ee] = jnp.sum(padded[ee:ee + 1, :] * (1.0 / rb)).astype(jnp.int32)
        used = jnp.sum(jnp.floor((jnp.sum(padded, axis=0, keepdims=True) + (slab - 1)) * (1.0 / slab)))
        meta_ref[slabs_used] = used.astype(jnp.int32)

        nb = MOE_BUILD_ROWS

        def build(j, carry):
            r0 = pl.multiple_of(j * nb, nb)
            rows = (lax.broadcasted_iota(jnp.int32, (nb, tm), 0) + r0).astype(F32)
            hit0 = rows == dest0
            hit1 = rows == dest1
            onehot = jnp.where(hit0 | hit1, 1.0, 0.0).astype(BF16)
            xs_ref[pl.ds(r0, nb), :] = _dot(onehot, h_b).astype(BF16)
            gw_ref[pl.ds(r0, nb), :] = (jnp.where(hit0, gate0, 0.0)
                                        + jnp.where(hit1, gate1, 0.0)).astype(BF16)
            return carry

        lax.fori_loop(0, meta_ref[slabs_used] * (slab // nb), build, 0)

    seg_start = meta_ref[e]
    seg_blocks = meta_ref[n_experts + e]

    def expert_rows(r0, rows):
        xb = xs_ref[pl.ds(r0, rows), :]
        a = _silu(_dot(xb, wg_ref[...])) * _dot(xb, wu_ref[...])
        y = y_ref[pl.ds(r0, rows), :].astype(F32) + _dot(a.astype(BF16), w2_ref[...])
        y_ref[pl.ds(r0, rows), :] = y.astype(BF16)

    def expert_pair(j, carry):
        expert_rows(pl.multiple_of(seg_start + j * (2 * rb), rb), 2 * rb)
        return carry

    lax.fori_loop(0, seg_blocks // 2, expert_pair, 0)

    @pl.when(seg_blocks % 2 == 1)
    def _odd_block():
        expert_rows(pl.multiple_of(seg_start + (seg_blocks - 1) * rb, rb), rb)

    @pl.when((c == pl.num_programs(1) - 1) & (e == pl.num_programs(2) - 1))
    def _combine():
        for kb in range(n_rows // slab):
            @pl.when(kb < meta_ref[slabs_used])
            def _():
                rs = slice(kb * slab, (kb + 1) * slab)
                o_ref[...] += _dot_tn(gw_ref[rs, :], y_ref[rs, :])

        if final_norm:
            o_ref[...] = _rms(o_ref[...], fg_ref[...])


def moe_residual(x, g, router_t, w13, w2, final_g, *, layer, tm, tf, rb, final_norm):
    n, d = x.shape
    _, n_experts, dff, _ = w2.shape
    nc = dff // tf
    n_rows = -(-(2 * tm + n_experts * rb) // MOE_SLAB_ROWS) * MOE_SLAB_ROWS
    return pl.pallas_call(
        functools.partial(_moe_kernel, n_experts=n_experts, rb=rb, final_norm=final_norm),
        grid=(n // tm, nc, n_experts),
        in_specs=[pl.BlockSpec((tm, d), lambda i, c, e: (i, 0), pipeline_mode=pl.Buffered(1)),
                  pl.BlockSpec((1, d), lambda i, c, e: (0, 0)),
                  pl.BlockSpec((n_experts, d), lambda i, c, e: (0, 0)),
                  pl.BlockSpec((None, None, d, tf), lambda i, c, e: (layer, e, 0, c)),
                  pl.BlockSpec((None, None, d, tf), lambda i, c, e: (layer, e, 0, nc + c)),
                  pl.BlockSpec((None, None, tf, d), lambda i, c, e: (layer, e, c, 0)),
                  pl.BlockSpec((1, d), lambda i, c, e: (0, 0))],
        out_specs=pl.BlockSpec((tm, d), lambda i, c, e: (i, 0)),
        out_shape=jax.ShapeDtypeStruct((n, d), F32),
        scratch_shapes=[pltpu.VMEM((n_rows, d), BF16), pltpu.VMEM((n_rows, tm), BF16),
                        pltpu.VMEM((n_rows, d), BF16), pltpu.SMEM((2 * n_experts + 1,), jnp.int32)],
        compiler_params=_cparams("parallel", "arbitrary", "arbitrary"),
        name="moe_residual",
    )(x, g, router_t, w13, w13, w2, final_g)


def _final_norm_kernel(x_ref, g_ref, o_ref):
    o_ref[...] = _rms(x_ref[...], g_ref[...])


def final_norm(x, g, *, tm):
    n, d = x.shape
    return pl.pallas_call(
        _final_norm_kernel,
        grid=(n // tm,),
        in_specs=[pl.BlockSpec((tm, d), lambda i: (i, 0)), pl.BlockSpec((1, d), lambda i: (0, 0))],
        out_specs=pl.BlockSpec((tm, d), lambda i: (i, 0)),
        out_shape=jax.ShapeDtypeStruct((n, d), F32),
        compiler_params=_cparams("parallel"),
        name="final_norm",
    )(x, g)


def _row(v):
    return v.reshape(1, -1).astype(F32)


def _pad_cols(a, cols):
    return jnp.pad(a, ((0, 0), (0, cols - a.shape[1])))


def _row_tile(n, pref):
    t = min(pref, n)
    while n % t:
        t //= 2
    return t


def ab_mixer(x, norm_g, w_in, lb_logits, hg_onorm_g, rw_mu, rw_w0, rw_w2, rw_a0, rw_a2, rw_g2,
             rw_k_k, rw_k_a, rw_r_k, rw_gn_g, rw_gn_b, *, batch, seq, layer_j):
    n, d = x.shape
    hg_cols = 4 * lb_logits.shape[1]
    rw_pad = w_in.shape[2] - hg_cols
    p_hg, p_rw = norm_matmul2(x, _row(norm_g), w_in, layer=layer_j, split=hg_cols,
                              tm=_row_tile(n, 512))
    y_a = hgrn2_mix(p_hg, lb_logits.astype(F32), _row(hg_onorm_g), batch=batch, seq=seq,
                    layer_j=layer_j, tt=min(HG_TIME_BLOCK, seq))
    y_b = rwkv7_mix(p_rw, _pad_cols(_row(rw_mu), rw_pad), _row(rw_w0), rw_w2.astype(BF16),
                    _row(rw_a0), rw_a2.astype(BF16), rw_g2.astype(BF16), _row(rw_k_k),
                    _row(rw_k_a), _row(rw_r_k), _row(rw_gn_g), _row(rw_gn_b),
                    batch=batch, seq=seq)
    return y_a, y_b


def _block_diag(t):
    nblk, gpb, r, c = t.shape
    eye = jnp.eye(gpb, dtype=t.dtype)
    return jnp.einsum('bgrc,gh->bgrhc', t, eye).reshape(nblk, gpb * r, gpb * c)


def s5_mixer(x, norm_g, w_in, a_re, a_im, log_dt, b_re, b_im, c_re, c_im, d_skip, *, batch, seq, layer_j):
    n, d = x.shape
    groups, p_state = a_re.shape
    gpb = S5_GROUPS_PER_BLOCK
    nblk = groups // gpb
    tm = _row_tile(seq, 512)
    tb = seq // tm
    u_tm = norm_matmul(x, _row(norm_g), w_in, layer=layer_j, tm=tm, tn=d, out_shape=(seq, batch * d),
                       out_map=lambda i, j: (i % tb, i // tb))
    ab_re, ab_im, bb_re, bb_im = s5_params(
        a_re.reshape(groups, 1, p_state).astype(F32), a_im.reshape(groups, 1, p_state).astype(F32),
        log_dt.reshape(groups, 1, 1).astype(F32),
        jnp.swapaxes(b_re, 1, 2).astype(F32), jnp.swapaxes(b_im, 1, 2).astype(F32))
    m = bb_re.shape[1]
    bb_blk = jnp.concatenate([_block_diag(bb_re.reshape(nblk, gpb, m, p_state)),
                              _block_diag(bb_im.reshape(nblk, gpb, m, p_state))], axis=-1)
    cre_blk = _block_diag(jnp.swapaxes(c_re, 1, 2).reshape(nblk, gpb, p_state, m))
    cim_blk = _block_diag(jnp.swapaxes(c_im, 1, 2).reshape(nblk, gpb, p_state, m))
    y_tm = s5_scan(u_tm.reshape(seq, batch, d), bb_blk.astype(BF16), cre_blk.astype(BF16),
                   cim_blk.astype(BF16), ab_re.reshape(nblk, 1, gpb * p_state),
                   ab_im.reshape(nblk, 1, gpb * p_state), _row(d_skip), lt=min(256, seq))
    return y_tm.reshape(seq, batch * d)


def kernel(x, mem, mix_norm_g, xattn_norm_g, ffn_norm_g, mem_norm_g, final_norm_g, ab_w_in, ab_w_out, hg_lb_logits, hg_onorm_g, rw_mu, rw_w0, rw_w2, rw_a0, rw_a2, rw_g2, rw_k_k, rw_k_a, rw_r_k, rw_gn_g, rw_gn_b, c_w_in, s5_a_re, s5_a_im, s5_log_dt, s5_b_re, s5_b_im, s5_c_re, s5_c_im, s5_d, c_w_glu, xa_wq, xa_wkv, xa_wo, ffn_w13, ffn_w2, moe_router, moe_w13, moe_w2):
    batch, seq, d = x.shape
    n_mem = mem.shape[1]
    depth = mix_norm_g.shape[0]
    n = batch * seq
    x = x.reshape(n, d).astype(F32)
    mem2 = mem.reshape(batch * n_mem, d).astype(F32)
    tm = _row_tile(n, 512)
    tm_seq = _row_tile(seq, 512)
    dff = ffn_w2.shape[1]
    tf = dff // 2 if (dff // 2) % LANES == 0 else dff

    in_cols = ab_w_in.shape[2]
    w_in_b = jnp.pad(ab_w_in, ((0, 0), (0, 0), (0, -in_cols % LANES))).astype(BF16)
    w_out_b, c_w_in_b, c_w_glu_b = ab_w_out.astype(BF16), c_w_in.astype(BF16), c_w_glu.astype(BF16)
    wq_b, wkv_b, wo_b = xa_wq.astype(BF16), xa_wkv.astype(BF16), xa_wo.astype(BF16)
    ffn_w13_b, ffn_w2_b = ffn_w13.astype(BF16), ffn_w2.astype(BF16)
    n_moe, n_experts = moe_w13.shape[:2]
    moe_f32 = (moe_w13.reshape(n_moe, n_experts * d, 2 * dff), moe_w2.reshape(n_moe, n_experts * dff, d))
    ffn_steps = (n // tm) * (dff // tf)
    piggyback = all(a.shape[1] % (ffn_steps * 2 * SUBLANES) == 0 for a in moe_f32)
    moe_bf16 = {}
    final_g = _row(final_norm_g)

    kv = norm_matmul_layers(mem2, _row(mem_norm_g), wkv_b, tm=_row_tile(batch * n_mem, 512), tn=d)

    for layer in range(depth):
        j = layer // 2
        xa_args = dict(layer_j=j, layer=layer, batch=batch, seq=seq, n_mem=n_mem, tm=tm_seq)
        if layer % 2 == 0:
            y_a, y_b = ab_mixer(x, mix_norm_g[layer], w_in_b, hg_lb_logits, hg_onorm_g[j],
                                rw_mu[j], rw_w0[j], rw_w2[j], rw_a0[j], rw_a2[j], rw_g2[j], rw_k_k[j],
                                rw_k_a[j], rw_r_k[j], rw_gn_g[j], rw_gn_b[j],
                                batch=batch, seq=seq, layer_j=j)
            x = ab_out_cross_attention(y_a, y_b, w_out_b, x, _row(xattn_norm_g[layer]), wq_b, kv, wo_b,
                                       **xa_args)
            nxt = (layer + 1) // 2
            do_cast = piggyback and layer + 1 < depth
            x, cast_out = ffn_residual(x, _row(ffn_norm_g[layer]), ffn_w13_b, ffn_w2_b, layer=j, tm=tm,
                                       tf=tf, cast=moe_f32 if do_cast else (), cast_layer=nxt)
            if do_cast:
                moe_bf16[nxt] = cast_out
        else:
            y = s5_mixer(x, mix_norm_g[layer], c_w_in_b, s5_a_re[j], s5_a_im[j], s5_log_dt[j],
                         s5_b_re[j], s5_b_im[j], s5_c_re[j], s5_c_im[j], s5_d[j],
                         batch=batch, seq=seq, layer_j=j)
            x = glu_out_cross_attention(y, c_w_glu_b, x, _row(xattn_norm_g[layer]), wq_b, kv, wo_b,
                                        **xa_args)
            w13_b, w2_b = moe_bf16.get(j) or tuple(a[j].astype(BF16) for a in moe_f32)
            x = moe_residual(x, _row(ffn_norm_g[layer]), moe_router[j].T.astype(F32),
                             w13_b.reshape(1, n_experts, d, 2 * dff), w2_b.reshape(1, n_experts, dff, d),
                             final_g, layer=0, tm=_row_tile(n, MOE_TOKEN_TILE),
                             tf=tf, rb=MOE_ROW_BLOCK, final_norm=layer == depth - 1)

    if depth % 2 == 1:
        x = final_norm(x, final_g, tm=tm)
    return x.reshape(batch, seq, d)
```

```python
import functools
import math

import jax
import jax.numpy as jnp
from jax import lax
from jax.experimental import pallas as pl
from jax.experimental.pallas import tpu as pltpu

F32 = jnp.float32
BF16 = jnp.bfloat16

NORM_EPS = 1e-6
LANES = 128
SUBLANES = 8
VMEM_LIMIT_BYTES = 60 * 1024 * 1024

HG_DK = 128
HG_CHUNK = 64
HG_EXP_CLIP = 60.0
HG_TIME_BLOCK = 2048

RW_N = 64
RW_CHUNK = 64
RW_GROUP_HEADS = 4
RW_STREAMS = 2
RW_CHUNKS_PER_STEP = 4
RW_DECAY_LORA = 32
RW_A_LORA = 32
RW_GATE_LORA = 96
RW_GN_EPS = 64e-5

S5_GROUP = 16
S5_STATE = 64
S5_GROUPS_PER_BLOCK = LANES // S5_GROUP
S5_BLOCK_STATE = S5_GROUPS_PER_BLOCK * S5_STATE
S5_TIME_BLOCK = 512
S5_SUB_STEPS = 32
S5_YIELD_STEPS = 16

XA_HEADS = 4
MOE_TOKEN_TILE = 1024
MOE_ROW_BLOCK = 128
MOE_RANK_BLOCK = 256
MOE_SLAB_ROWS = 512
MOE_BUILD_ROWS = 256


def _cparams(*sem):
    return pltpu.CompilerParams(dimension_semantics=sem, vmem_limit_bytes=VMEM_LIMIT_BYTES)


def _dot(a, b, precision=None):
    return jnp.dot(a, b, preferred_element_type=F32, precision=precision)


def _dot_nt(a, b, precision=None):
    return lax.dot_general(a, b, (((1,), (1,)), ((), ())), preferred_element_type=F32,
                           precision=precision)


def _dot_tn(a, b, precision=None):
    return lax.dot_general(a, b, (((0,), (0,)), ((), ())), preferred_element_type=F32,
                           precision=precision)


def _split_bf16(t, terms):
    parts = []
    for _ in range(terms):
        hi = t.astype(BF16)
        parts.append(hi)
        t = t - hi.astype(F32)
    return parts


def _rms(x, g, eps=NORM_EPS):
    return x * lax.rsqrt(jnp.mean(x * x, axis=-1, keepdims=True) + eps) * g


def _silu(x):
    return x * jax.nn.sigmoid(x)


def _tri_incl(n):
    r = lax.broadcasted_iota(jnp.int32, (n, n), 0)
    c = lax.broadcasted_iota(jnp.int32, (n, n), 1)
    return jnp.where(r >= c, 1.0, 0.0).astype(F32)


def _norm_matmul_kernel(x_ref, g_ref, w_ref, o_ref, h_ref):
    @pl.when(pl.program_id(1) == 0)
    def _():
        h_ref[...] = _rms(x_ref[...], g_ref[...]).astype(BF16)

    o_ref[...] = _dot(h_ref[...], w_ref[...]).astype(o_ref.dtype)


def norm_matmul(x, g, w, *, layer, tm, tn, out_dtype=F32, out_shape=None, out_map=None):
    m, k = x.shape
    n = w.shape[2]
    assert m % tm == 0 and n % tn == 0
    if out_shape is None:
        out_shape = (m, n)
    if out_map is None:
        out_map = lambda i, j: (i, j)
    return pl.pallas_call(
        _norm_matmul_kernel,
        grid=(m // tm, n // tn),
        in_specs=[pl.BlockSpec((tm, k), lambda i, j: (i, 0)),
                  pl.BlockSpec((1, k), lambda i, j: (0, 0)),
                  pl.BlockSpec((None, k, tn), lambda i, j: (layer, 0, j))],
        out_specs=pl.BlockSpec((tm, tn), out_map),
        out_shape=jax.ShapeDtypeStruct(out_shape, out_dtype),
        scratch_shapes=[pltpu.VMEM((tm, k), BF16)],
        compiler_params=_cparams("parallel", "arbitrary"),
        name="norm_matmul",
    )(x, g, w)


def _norm_matmul_layers_kernel(x_ref, g_ref, w_ref, o_ref, h_ref):
    @pl.when((pl.program_id(1) == 0) & (pl.program_id(2) == 0))
    def _():
        h_ref[...] = _rms(x_ref[...], g_ref[...]).astype(BF16)

    o_ref[...] = _dot(h_ref[...], w_ref[...]).astype(o_ref.dtype)


def norm_matmul_layers(x, g, w, *, tm, tn):
    m, k = x.shape
    layers, _, n = w.shape
    nj = n // tn
    return pl.pallas_call(
        _norm_matmul_layers_kernel,
        grid=(m // tm, layers, nj),
        in_specs=[pl.BlockSpec((tm, k), lambda i, l, j: (i, 0)),
                  pl.BlockSpec((1, k), lambda i, l, j: (0, 0)),
                  pl.BlockSpec((None, k, tn), lambda i, l, j: (l, 0, j))],
        out_specs=pl.BlockSpec((tm, tn), lambda i, l, j: (i, l * nj + j)),
        out_shape=jax.ShapeDtypeStruct((m, layers * n), BF16),
        scratch_shapes=[pltpu.VMEM((tm, k), BF16)],
        compiler_params=_cparams("parallel", "arbitrary", "arbitrary"),
        name="norm_matmul_layers",
    )(x, g, w)


def _norm_matmul2_kernel(x_ref, g_ref, w_ref, oa_ref, ob_ref):
    h = _rms(x_ref[...], g_ref[...]).astype(BF16)
    na = oa_ref.shape[1]
    oa_ref[...] = _dot(h, w_ref[:, :na])
    ob_ref[...] = _dot(h, w_ref[:, na:])


def norm_matmul2(x, g, w, *, layer, split, tm):
    m, k = x.shape
    n = w.shape[2]
    assert split % LANES == 0 and (n - split) % LANES == 0
    return pl.pallas_call(
        _norm_matmul2_kernel,
        grid=(m // tm,),
        in_specs=[pl.BlockSpec((tm, k), lambda i: (i, 0)),
                  pl.BlockSpec((1, k), lambda i: (0, 0)),
                  pl.BlockSpec((None, k, n), lambda i: (layer, 0, 0))],
        out_specs=(pl.BlockSpec((tm, split), lambda i: (i, 0)),
                   pl.BlockSpec((tm, n - split), lambda i: (i, 0))),
        out_shape=(jax.ShapeDtypeStruct((m, split), F32), jax.ShapeDtypeStruct((m, n - split), F32)),
        compiler_params=_cparams("parallel"),
        name="norm_matmul2",
    )(x, g, w)


def _hgrn2_kernel(q_ref, f_ref, i_ref, g_ref, lbl_ref, og_ref, o_ref, st_ref, *, layer_j, n_chunks):
    c = HG_CHUNK
    halves = [c >> (i + 1) for i in range(c.bit_length() - 1)]

    @pl.when(pl.program_id(2) == 0)
    def _():
        st_ref[...] = jnp.zeros_like(st_ref)

    logits = lbl_ref[...]
    e = jnp.exp(logits - jnp.max(logits, axis=0, keepdims=True))
    p = e / jnp.sum(e, axis=0, keepdims=True)
    lb = jnp.zeros((1, p.shape[1]), F32)
    for r in range(1, layer_j + 1):
        lb = lb + p[r:r + 1, :]

    og = og_ref[...]
    rr = lax.broadcasted_iota(jnp.int32, (c, c), 0)
    cc = lax.broadcasted_iota(jnp.int32, (c, c), 1)
    row = lax.broadcasted_iota(jnp.int32, (c, HG_DK), 0)

    sums = [rr >= cc]
    pair_masks = []
    for hb in halves:
        blk = 2 * hb
        sums.append(cc <= jnp.bitwise_and(rr, -blk) + (hb - 1))
        same = jnp.bitwise_and(rr, -blk) == jnp.bitwise_and(cc, -blk)
        pair_masks.append(jnp.where(same, jnp.bitwise_and(rr, hb) - jnp.bitwise_and(cc, hb), 0) > 0)
    prefix = jnp.concatenate([jnp.where(m, 1.0, 0.0) for m in sums], axis=0).astype(BF16)

    chunks = range(n_chunks)
    rows = [slice(ci * c, (ci + 1) * c) for ci in chunks]
    qs, ks, ivs, cums = [], [], [], []
    for ci in chunks:
        fr = f_ref[rows[ci], :]
        log_sig = jnp.minimum(fr, 0.0) - jnp.log1p(jnp.exp(-jnp.abs(fr)))
        log_f = log_sig + jnp.log1p(lb * jnp.exp(jnp.minimum(-fr, HG_EXP_CLIP)))
        cums.append(sum(_dot(prefix, part) for part in _split_bf16(log_f, 3)))
        qs.append(_silu(q_ref[rows[ci], :]))
        ks.append((1.0 - lb) * jax.nn.sigmoid(-fr))
        ivs.append(i_ref[rows[ci], :])
    ls = [cm[:c] for cm in cums]

    a_mats = [jnp.zeros((c, c), F32) for _ in chunks]
    for li, hb in enumerate(halves):
        upper = jnp.bitwise_and(row, hb) != 0
        for ci in chunks:
            d = ls[ci] - cums[ci][(li + 1) * c:(li + 2) * c]
            ex = jnp.exp(jnp.where(upper, d, -d))
            prod = _dot_nt((qs[ci] * ex).astype(BF16), (ks[ci] * ex).astype(BF16))
            a_mats[ci] = a_mats[ci] + jnp.where(pair_masks[li], prod, 0.0)

    os_, kvs, qes, ends = [], [], [], []
    for ci in chunks:
        q, k, l, iv = qs[ci], ks[ci], ls[ci], ivs[ci]
        iv_b = iv.astype(BF16)
        l_end = l[c - 1:c, :]
        os_.append(_dot(a_mats[ci].astype(BF16), iv_b) + jnp.sum(q * k, axis=-1, keepdims=True) * iv)
        qes.append((q * jnp.exp(l)).astype(BF16))
        kvs.append(_dot_tn(iv_b, (k * jnp.exp(l_end - l)).astype(BF16)))
        ends.append(jnp.exp(l_end))

    st = st_ref[...]
    for ci in range(n_chunks):
        rs = slice(ci * c, (ci + 1) * c)
        o = os_[ci] + _dot_nt(qes[ci], st.astype(BF16))
        st = st * ends[ci] + kvs[ci]
        o_ref[rs, :] = (_rms(o, og) * _silu(g_ref[rs, :])).astype(o_ref.dtype)
    st_ref[...] = st


def hgrn2_mix(p_hg, lb_logits, onorm_g, *, batch, seq, layer_j, tt=256):
    n, cols = p_hg.shape
    width = cols // 4
    heads = width // HG_DK
    assert seq % tt == 0 and tt % HG_CHUNK == 0
    tb = seq // tt
    n_ab = lb_logits.shape[0]

    def col_spec(part):
        return pl.BlockSpec((tt, HG_DK), lambda b, h, t, part=part: (b * tb + t, part * heads + h))

    kern = functools.partial(_hgrn2_kernel, layer_j=layer_j, n_chunks=tt // HG_CHUNK)
    return pl.pallas_call(
        kern,
        grid=(batch, heads, tb),
        in_specs=[col_spec(0), col_spec(1), col_spec(2), col_spec(3),
                  pl.BlockSpec((n_ab, HG_DK), lambda b, h, t: (0, h)),
                  pl.BlockSpec((1, HG_DK), lambda b, h, t: (0, 0))],
        out_specs=pl.BlockSpec((tt, HG_DK), lambda b, h, t: (b * tb + t, h)),
        out_shape=jax.ShapeDtypeStruct((n, width), BF16),
        scratch_shapes=[pltpu.VMEM((HG_DK, HG_DK), F32)],
        compiler_params=_cparams("parallel", "parallel", "arbitrary"),
        name="hgrn2_mix",
    )(p_hg, p_hg, p_hg, p_hg, lb_logits, onorm_g)


def _round_robin(chains):
    chains = list(chains)
    while chains:
        alive = []
        for ch in chains:
            try:
                next(ch)
                alive.append(ch)
            except StopIteration:
                pass
        chains = alive
        yield


def _rwkv7_stream(p_ref, mu_ref, w0_ref, w2_ref, a0_ref, a2_ref, g2_ref, kk_ref, ka_ref,
                  rk_ref, gng_ref, gnb_ref, ones_ref, o_ref, prev_ref, st_ref, *, heads, chunks):
    c = RW_CHUNK
    n = RW_N
    assert c == n
    width = heads * n
    gh = RW_GROUP_HEADS
    gw = gh * n
    gr = gh * c
    ngroups = heads // gh
    shift = c.bit_length() - 1

    @pl.when(pl.program_id(1) == 0)
    def _():
        prev_ref[...] = jnp.zeros_like(prev_ref)
        st_ref[...] = jnp.zeros_like(st_ref)

    ones = ones_ref[...]

    def head_sum(t):
        return sum(_dot(part, ones) for part in _split_bf16(t, 2))

    tri = _tri_incl(c).astype(BF16)
    rb = lax.broadcasted_iota(jnp.int32, (gr, gw), 0)
    cb = lax.broadcasted_iota(jnp.int32, (gr, gw), 1)
    same = jnp.right_shift(rb, shift) == jnp.right_shift(cb, shift)
    t_in = jnp.bitwise_and(rb, c - 1)
    s_in = jnp.bitwise_and(cb, c - 1)
    strict = t_in > s_in
    incl = t_in >= s_in
    eye = jnp.where(rb == cb, 1.0, 0.0).astype(F32)

    def bd(t):
        return jnp.where(same, jnp.concatenate([t] * gh, axis=0), 0.0).astype(BF16)

    pre = [None] * chunks
    outs = [[None] * ngroups for _ in range(chunks)]

    def prepare(ci):
        p = p_ref[ci * c:(ci + 1) * c, :]
        before = prev_ref[0:1, :] if ci == 0 else p_ref[ci * c - 1:ci * c, :]
        row = lax.broadcasted_iota(jnp.int32, p.shape, 0)
        shifted = jnp.where(row == 0, before, pltpu.roll(p, 1, axis=0))
        p = p + (shifted - p) * mu_ref[...]

        r = p[:, 0:width]
        k = p[:, width:2 * width]
        v = p[:, 2 * width:3 * width]
        off = 3 * width
        wd = p[:, off:off + RW_DECAY_LORA]
        ad = p[:, off + RW_DECAY_LORA:off + RW_DECAY_LORA + RW_A_LORA]
        gd = p[:, off + RW_DECAY_LORA + RW_A_LORA:off + RW_DECAY_LORA + RW_A_LORA + RW_GATE_LORA]

        wz = w0_ref[...] + _dot(jnp.tanh(wd).astype(BF16), w2_ref[...])
        w = jnp.minimum(wz, 0.0) - jnp.log1p(jnp.exp(-jnp.abs(wz))) - 0.5
        lw = -jnp.exp(w)
        a = jax.nn.sigmoid(a0_ref[...] + _dot(ad.astype(BF16), a2_ref[...]))
        g = _dot(jax.nn.sigmoid(gd).astype(BF16), g2_ref[...])
        k2 = k * (1.0 + (a - 1.0) * ka_ref[...])
        yield

        kk = k * kk_ref[...]
        kk = kk * lax.rsqrt(jnp.maximum(head_sum(kk * kk), 1e-24))
        b = kk * a
        cum = sum(_dot(tri, part) for part in _split_bf16(lw, 3))
        cum_end = cum[c - 1:c, :]
        e_inv = jnp.exp(-cum)
        e_fin = jnp.exp(cum_end - cum)
        e_end = jnp.exp(cum_end)
        at = -kk * jnp.exp(cum - lw)
        rt = r * jnp.exp(cum)
        bt = b * e_inv
        kt = k2 * e_inv
        bf = b * e_fin
        kf = k2 * e_fin
        yield

        groups = []
        for gi in range(ngroups):
            gs = slice(gi * gw, (gi + 1) * gw)
            groups.append(dict(
                art=jnp.concatenate([bd(at[:, gs]), bd(rt[:, gs])], axis=0),
                bt=bd(bt[:, gs]), kt=bd(kt[:, gs]), v=bd(v[:, gs]),
                bk=jnp.concatenate([bd(bf[:, gs]), bd(kf[:, gs])], axis=0), e_end=e_end[:, gs]))
            yield
        pre[ci] = dict(groups=groups, g=g, v=v, rkk=r * k2 * rk_ref[...])

    def group_chain(ci, gi):
        q = pre[ci]["groups"][gi]
        art, v_bd = q["art"], q["v"]
        g_b = _dot_nt(art, q["bt"])
        g_k = _dot_nt(art, q["kt"])
        yield
        a_ab = jnp.where(strict, g_b[:gr], 0.0)
        a_ak = jnp.where(strict, g_k[:gr], 0.0)
        a_rb = jnp.where(incl, g_b[gr:], 0.0)
        a_rk = jnp.where(incl, g_k[gr:], 0.0)

        tinv = eye + a_ab
        xb = a_ab.astype(BF16)
        for _ in range(shift - 1):
            xb = _dot(xb, xb).astype(BF16)
            yield
            tinv = tinv + _dot(tinv.astype(BF16), xb)
            yield

        st = st_ref[gi]
        arm = _dot_nt(art, st.astype(BF16))
        akv = _dot(jnp.concatenate([a_ak, a_rk], axis=0).astype(BF16), v_bd)
        yield
        u_b = _dot(tinv.astype(BF16), (arm[:gr] + akv[:gr]).astype(BF16)).astype(BF16)
        yield
        o_bd = arm[gr:] + akv[gr:] + _dot(a_rb.astype(BF16), u_b)
        st_ref[gi] = st * q["e_end"] + _dot_tn(jnp.concatenate([u_b, v_bd], axis=0), q["bk"])
        o_g = o_bd[0:c]
        for hh in range(1, gh):
            o_g = o_g + o_bd[hh * c:(hh + 1) * c]
        outs[ci][gi] = o_g

    def finish(ci):
        q = pre[ci]
        o = jnp.concatenate(outs[ci], axis=-1)
        mean = head_sum(o) * (1.0 / n)
        dev = o - mean
        yield
        var = head_sum(dev * dev) * (1.0 / n)
        on = dev * lax.rsqrt(var + RW_GN_EPS) * gng_ref[...] + gnb_ref[...]
        yield
        bonus = head_sum(q["rkk"]) * q["v"]
        o_ref[ci * c:(ci + 1) * c, :] = ((on + bonus) * q["g"]).astype(o_ref.dtype)

    yield from prepare(0)
    for ci in range(chunks):
        stages = [_round_robin([group_chain(ci, gi) for gi in range(ngroups)])]
        if ci + 1 < chunks:
            stages.append(prepare(ci + 1))
        if ci >= 1:
            stages.append(finish(ci - 1))
        yield from _round_robin(stages)
    yield from finish(chunks - 1)
    prev_ref[0:1, :] = p_ref[chunks * c - 1:chunks * c, :]


def _rwkv7_kernel(p_ref, *refs, heads, streams, chunks):
    *param_refs, o_ref, prev_ref, st_ref = refs
    chains = [_rwkv7_stream(p_ref.at[bi], *param_refs, o_ref.at[bi], prev_ref.at[bi],
                            st_ref.at[bi], heads=heads, chunks=chunks) for bi in range(streams)]
    for _ in _round_robin(chains):
        pass


def rwkv7_mix(p_rw, mu, w0, w2, a0, a2, g2, k_k, k_a, r_k, gn_g, gn_b, *, batch, seq):
    n_tok, cols = p_rw.shape
    width = w0.shape[1]
    heads = width // RW_N
    assert seq % RW_CHUNK == 0 and heads % RW_GROUP_HEADS == 0
    cb = seq // RW_CHUNK
    head_id = jnp.arange(width, dtype=jnp.int32) // RW_N
    ones = (head_id[:, None] == head_id[None, :]).astype(BF16)
    full = lambda a: pl.BlockSpec(a.shape, lambda b, t: (0,) * a.ndim)
    params = (mu, w0, w2, a0, a2, g2, k_k, k_a, r_k, gn_g, gn_b, ones)
    gdim = RW_GROUP_HEADS * RW_N
    nb = RW_STREAMS if batch % RW_STREAMS == 0 else 1
    cps = RW_CHUNKS_PER_STEP if cb % RW_CHUNKS_PER_STEP == 0 else 1
    rows = cps * RW_CHUNK
    out = pl.pallas_call(
        functools.partial(_rwkv7_kernel, heads=heads, streams=nb, chunks=cps),
        grid=(batch // nb, cb // cps),
        in_specs=[pl.BlockSpec((nb, rows, cols), lambda b, t: (b, t, 0))]
                 + [full(a) for a in params],
        out_specs=pl.BlockSpec((nb, rows, width), lambda b, t: (b, t, 0)),
        out_shape=jax.ShapeDtypeStruct((batch, seq, width), BF16),
        scratch_shapes=[pltpu.VMEM((nb, SUBLANES, cols), F32),
                        pltpu.VMEM((nb, heads // RW_GROUP_HEADS, gdim, gdim), F32)],
        compiler_params=_cparams("parallel", "arbitrary"),
        name="rwkv7_mix",
    )(p_rw.reshape(batch, seq, cols), *params)
    return out.reshape(n_tok, width)


def _s5_param_kernel(are_ref, aim_ref, ldt_ref, bre_ref, bim_ref,
                     abre_ref, abim_ref, bbre_ref, bbim_ref):
    lam_re = jnp.minimum(are_ref[...], -1e-4)
    lam_im = aim_ref[...]
    dt = jnp.exp(ldt_ref[...])
    mag = jnp.exp(lam_re * dt)
    ab_re = mag * jnp.cos(lam_im * dt)
    ab_im = mag * jnp.sin(lam_im * dt)
    den = lam_re * lam_re + lam_im * lam_im
    coef_re = ((ab_re - 1.0) * lam_re + ab_im * lam_im) / den
    coef_im = (ab_im * lam_re - (ab_re - 1.0) * lam_im) / den
    abre_ref[...] = ab_re
    abim_ref[...] = ab_im
    b_re = bre_ref[...]
    b_im = bim_ref[...]
    bbre_ref[...] = coef_re * b_re - coef_im * b_im
    bbim_ref[...] = coef_re * b_im + coef_im * b_re


def s5_params(a_re, a_im, log_dt, b_re_t, b_im_t):
    g, _, p = a_re.shape
    m = b_re_t.shape[1]
    return pl.pallas_call(
        _s5_param_kernel,
        out_shape=(jax.ShapeDtypeStruct((g, 1, p), F32), jax.ShapeDtypeStruct((g, 1, p), F32),
                   jax.ShapeDtypeStruct((g, m, p), F32), jax.ShapeDtypeStruct((g, m, p), F32)),
        name="s5_params",
    )(a_re, a_im, log_dt, b_re_t, b_im_t)


def _gelu_tanh(y):
    return 0.5 * y * (1.0 + jnp.tanh(math.sqrt(2.0 / math.pi) * (y + 0.044715 * (y * y * y))))


def _s5_scan_kernel(u_ref, bb_ref, cre_ref, cim_ref, are_ref, aim_ref, d_ref, o_ref,
                    xs_ref, st_ref, *, lt, bsz):
    ns = S5_BLOCK_STATE

    @pl.when(pl.program_id(1) == 0)
    def _():
        st_ref[...] = jnp.zeros_like(st_ref)

    a_re = jnp.broadcast_to(are_ref[...], (bsz, ns))
    a_im = jnp.broadcast_to(aim_ref[...], (bsz, ns))
    sub = min(S5_SUB_STEPS, lt)
    nsub = lt // sub
    piece = max(sub // 2, 1)
    state = [st_ref[0], st_ref[1]]

    def project(k):
        for t0 in range(k * sub, (k + 1) * sub, piece):
            u2 = u_ref[t0:t0 + piece].reshape(piece * bsz, LANES)
            xs_ref[t0 * bsz:(t0 + piece) * bsz, :] = _dot(u2.astype(BF16), bb_ref[...])
            yield

    def recur(k):
        x_re, x_im = state
        for t in range(k * sub, (k + 1) * sub):
            rs = slice(t * bsz, (t + 1) * bsz)
            n_re = a_re * x_re - a_im * x_im + xs_ref[rs, 0:ns]
            n_im = a_re * x_im + a_im * x_re + xs_ref[rs, ns:2 * ns]
            xs_ref[rs, 0:ns] = n_re
            xs_ref[rs, ns:2 * ns] = n_im
            x_re, x_im = n_re, n_im
            if (t + 1) % S5_YIELD_STEPS == 0:
                yield
        state[0], state[1] = x_re, x_im

    def readout(k):
        for t0 in range(k * sub, (k + 1) * sub, piece):
            rs = slice(t0 * bsz, (t0 + piece) * bsz)
            u2 = u_ref[t0:t0 + piece].reshape(piece * bsz, LANES)
            y = (_dot(xs_ref[rs, 0:ns].astype(BF16), cre_ref[...])
                 - _dot(xs_ref[rs, ns:2 * ns].astype(BF16), cim_ref[...]))
            y = y + d_ref[...] * u2
            o_ref[t0:t0 + piece] = _gelu_tanh(y).reshape(piece, bsz, LANES)
            yield

    for k in range(nsub + 2):
        stages = []
        if k < nsub:
            stages.append(project(k))
        if 0 <= k - 1 < nsub:
            stages.append(recur(k - 1))
        if 0 <= k - 2 < nsub:
            stages.append(readout(k - 2))
        for _ in _round_robin(stages):
            pass

    st_ref[0] = state[0]
    st_ref[1] = state[1]


def s5_scan(u_tm, bb_blk, cre_blk, cim_blk, ab_re, ab_im, d, *, lt=256):
    seq, bsz, dm = u_tm.shape
    nblk = dm // LANES
    ns = S5_BLOCK_STATE
    assert seq % lt == 0
    return pl.pallas_call(
        functools.partial(_s5_scan_kernel, lt=lt, bsz=bsz),
        grid=(nblk, seq // lt),
        in_specs=[pl.BlockSpec((lt, bsz, LANES), lambda c, t: (t, 0, c)),
                  pl.BlockSpec((None, LANES, 2 * ns), lambda c, t: (c, 0, 0)),
                  pl.BlockSpec((None, ns, LANES), lambda c, t: (c, 0, 0)),
                  pl.BlockSpec((None, ns, LANES), lambda c, t: (c, 0, 0)),
                  pl.BlockSpec((None, 1, ns), lambda c, t: (c, 0, 0)),
                  pl.BlockSpec((None, 1, ns), lambda c, t: (c, 0, 0)),
                  pl.BlockSpec((1, LANES), lambda c, t: (0, c))],
        out_specs=pl.BlockSpec((lt, bsz, LANES), lambda c, t: (t, 0, c)),
        out_shape=jax.ShapeDtypeStruct((seq, bsz, dm), F32),
        scratch_shapes=[pltpu.VMEM((lt * bsz, 2 * ns), F32),
                        pltpu.VMEM((2, bsz, ns), F32)],
        compiler_params=_cparams("parallel", "arbitrary"),
        name="s5_scan",
    )(u_tm, bb_blk, cre_blk, cim_blk, ab_re, ab_im, d)


def _xattn_core(x, g_ref, wq_ref, k_ref, v_ref, wo_ref, heads):
    d = x.shape[1]
    hd = d // heads
    h = _rms(x, g_ref[...]).astype(BF16)
    q = _dot(h, wq_ref[...]).astype(BF16)
    outs = []
    for i in range(heads):
        sl = slice(i * hd, (i + 1) * hd)
        s = _dot_nt(q[:, sl], k_ref[:, sl]) * (hd ** -0.5)
        s = s - jnp.max(s, axis=-1, keepdims=True)
        e = jnp.exp(s)
        p = e / jnp.sum(e, axis=-1, keepdims=True)
        outs.append(_dot(p.astype(BF16), v_ref[:, sl]))
    o = jnp.concatenate(outs, axis=-1).astype(BF16)
    return x + _dot(o, wo_ref[...])


def _ab_out_xattn_kernel(ya_ref, yb_ref, wa_ref, wb_ref, x_ref, g_ref, wq_ref, k_ref, v_ref, wo_ref,
                         o_ref, *, heads):
    x = x_ref[...] + _dot(ya_ref[...], wa_ref[...]) + _dot(yb_ref[...], wb_ref[...])
    o_ref[...] = _xattn_core(x, g_ref, wq_ref, k_ref, v_ref, wo_ref, heads)


def _glu_out_xattn_kernel(y_ref, wglu_ref, x_ref, g_ref, wq_ref, k_ref, v_ref, wo_ref, o_ref, *, heads):
    d = x_ref.shape[1]
    z = _dot(y_ref[...].astype(BF16), wglu_ref[...])
    x = x_ref[...] + z[:, :d] * jax.nn.sigmoid(z[:, d:])
    o_ref[...] = _xattn_core(x, g_ref, wq_ref, k_ref, v_ref, wo_ref, heads)


def _xattn_specs(tm, d, n_mem, layer, rows):
    return [pl.BlockSpec((tm, d), rows),
            pl.BlockSpec((1, d), lambda b, t: (0, 0)),
            pl.BlockSpec((None, d, d), lambda b, t: (layer, 0, 0)),
            pl.BlockSpec((n_mem, d), lambda b, t: (b, 2 * layer)),
            pl.BlockSpec((n_mem, d), lambda b, t: (b, 2 * layer + 1)),
            pl.BlockSpec((None, d, d), lambda b, t: (layer, 0, 0))]


def ab_out_cross_attention(ya, yb, w_out, x, g, wq, kv, wo, *, layer_j, layer, batch, seq, n_mem, tm):
    n, d = x.shape
    ka, kb = ya.shape[1], yb.shape[1]
    assert ka == kb
    tb = seq // tm
    rows = lambda b, t: (b * tb + t, 0)
    xa = _xattn_specs(tm, d, n_mem, layer, rows)
    return pl.pallas_call(
        functools.partial(_ab_out_xattn_kernel, heads=XA_HEADS),
        grid=(batch, tb),
        in_specs=[pl.BlockSpec((tm, ka), rows), pl.BlockSpec((tm, kb), rows),
                  pl.BlockSpec((None, ka, d), lambda b, t: (layer_j, 0, 0)),
                  pl.BlockSpec((None, kb, d), lambda b, t: (layer_j, 1, 0))] + xa,
        out_specs=pl.BlockSpec((tm, d), rows),
        out_shape=jax.ShapeDtypeStruct((n, d), F32),
        compiler_params=_cparams("parallel", "parallel"),
        name="ab_out_cross_attention",
    )(ya, yb, w_out, w_out, x, g, wq, kv, kv, wo)


def glu_out_cross_attention(y_tm2d, w_glu, x, g, wq, kv, wo, *, layer_j, layer, batch, seq, n_mem, tm):
    n, d = x.shape
    tb = seq // tm
    rows = lambda b, t: (b * tb + t, 0)
    xa = _xattn_specs(tm, d, n_mem, layer, rows)
    return pl.pallas_call(
        functools.partial(_glu_out_xattn_kernel, heads=XA_HEADS),
        grid=(batch, tb),
        in_specs=[pl.BlockSpec((tm, d), lambda b, t: (t, b)),
                  pl.BlockSpec((None, d, 2 * d), lambda b, t: (layer_j, 0, 0))] + xa,
        out_specs=pl.BlockSpec((tm, d), rows),
        out_shape=jax.ShapeDtypeStruct((n, d), F32),
        compiler_params=_cparams("parallel", "parallel"),
        name="glu_out_cross_attention",
    )(y_tm2d, w_glu, x, g, wq, kv, kv, wo)


def _ffn_kernel(x_ref, g_ref, wg_ref, wu_ref, w2_ref, *refs, n_cast):
    cast_in, (o_ref, *cast_out), (h_ref, acc_ref) = refs[:n_cast], refs[n_cast:2 * n_cast + 1], refs[-2:]
    c = pl.program_id(1)

    @pl.when(c == 0)
    def _():
        h_ref[...] = _rms(x_ref[...], g_ref[...]).astype(BF16)
        acc_ref[...] = jnp.zeros_like(acc_ref)

    h = h_ref[...]
    a = _silu(_dot(h, wg_ref[...])) * _dot(h, wu_ref[...])
    acc_ref[...] += _dot(a.astype(BF16), w2_ref[...])

    for src, dst in zip(cast_in, cast_out):
        dst[...] = src[...].astype(BF16)

    @pl.when(c == pl.num_programs(1) - 1)
    def _():
        o_ref[...] = x_ref[...] + acc_ref[...]


def ffn_residual(x, g, w13, w2, *, layer, tm, tf, cast=(), cast_layer=0):
    n, d = x.shape
    dff = w2.shape[1]
    nc = dff // tf
    steps = (n // tm) * nc
    cast_specs, cast_out_specs, cast_shapes = [], [], []
    for arr in cast:
        _, rows, cols = arr.shape
        assert rows % (steps * 2 * SUBLANES) == 0
        blk = rows // steps
        cast_specs.append(pl.BlockSpec((None, blk, cols), lambda i, c: (cast_layer, i * nc + c, 0)))
        cast_out_specs.append(pl.BlockSpec((blk, cols), lambda i, c: (i * nc + c, 0)))
        cast_shapes.append(jax.ShapeDtypeStruct((rows, cols), BF16))
    out = pl.pallas_call(
        functools.partial(_ffn_kernel, n_cast=len(cast)),
        grid=(n // tm, nc),
        in_specs=[pl.BlockSpec((tm, d), lambda i, c: (i, 0)),
                  pl.BlockSpec((1, d), lambda i, c: (0, 0)),
                  pl.BlockSpec((None, d, tf), lambda i, c: (layer, 0, c)),
                  pl.BlockSpec((None, d, tf), lambda i, c: (layer, 0, nc + c)),
                  pl.BlockSpec((None, tf, d), lambda i, c: (layer, c, 0))] + cast_specs,
        out_specs=[pl.BlockSpec((tm, d), lambda i, c: (i, 0))] + cast_out_specs,
        out_shape=[jax.ShapeDtypeStruct((n, d), F32)] + cast_shapes,
        scratch_shapes=[pltpu.VMEM((tm, d), BF16), pltpu.VMEM((tm, d), F32)],
        compiler_params=_cparams("parallel", "arbitrary"),
        name="ffn_residual",
    )(x, g, w13, w13, w2, *cast)
    return out[0], tuple(out[1:])


def _moe_route(logits_t, n_experts, rb):
    m = logits_t.shape[1]
    sub = lax.broadcasted_iota(jnp.int32, logits_t.shape, 0).astype(F32)
    neg = -jnp.inf
    m1 = jnp.max(logits_t, axis=0, keepdims=True)
    i1 = jnp.min(jnp.where(logits_t == m1, sub, float(n_experts)), axis=0, keepdims=True)
    first = sub == i1
    rest = jnp.where(first, neg, logits_t)
    m2 = jnp.max(rest, axis=0, keepdims=True)
    i2 = jnp.min(jnp.where(rest == m2, sub, float(n_experts)), axis=0, keepdims=True)
    second = sub == i2
    e2 = jnp.exp(m2 - m1)
    den = 1.0 + e2
    gate0 = 1.0 / den
    gate1 = e2 / den

    first_f = jnp.where(first, 1.0, 0.0)
    second_f = jnp.where(second, 1.0, 0.0)
    both = jnp.concatenate([first_f, second_f], axis=0)
    w = min(m, MOE_RANK_BLOCK)
    t_src = lax.broadcasted_iota(jnp.int32, (w, w), 0)
    t_dst = lax.broadcasted_iota(jnp.int32, (w, w), 1)
    before = jnp.where(t_src < t_dst, 1.0, 0.0).astype(BF16)
    cnt = jnp.zeros((2 * n_experts, 1), F32)
    ranks = []
    for kb in range(m // w):
        blk = both[:, kb * w:(kb + 1) * w]
        ranks.append(_dot(blk.astype(BF16), before) + cnt)
        cnt = cnt + jnp.sum(blk, axis=1, keepdims=True)
    ranks = jnp.concatenate(ranks, axis=1)
    cnt0 = cnt[:n_experts]
    cnt_e = cnt0 + cnt[n_experts:]
    padded = jnp.floor((cnt_e + (rb - 1)) * (1.0 / rb)) * rb
    sub_col = lax.broadcasted_iota(jnp.int32, (n_experts, 1), 0)
    start = jnp.zeros((n_experts, 1), F32)
    for ee in range(n_experts - 1):
        start = start + jnp.where(sub_col > ee, padded[ee:ee + 1, :], 0.0)
    dest0 = jnp.sum(first_f * (start + ranks[:n_experts]), axis=0, keepdims=True)
    dest1 = jnp.sum(second_f * (start + cnt0 + ranks[n_experts:]), axis=0, keepdims=True)
    return dest0, dest1, gate0, gate1, start, padded


def _moe_kernel(x_ref, g_ref, rt_ref, wg_ref, wu_ref, w2_ref, fg_ref, o_ref, xs_ref, gw_ref, y_ref,
                meta_ref, *, n_experts, rb, final_norm):
    c = pl.program_id(1)
    e = pl.program_id(2)
    n_rows = xs_ref.shape[0]
    tm = x_ref.shape[0]
    slab = MOE_SLAB_ROWS
    slabs_used = 2 * n_experts

    @pl.when((c == 0) & (e == 0))
    def _route():
        x = x_ref[...]
        hf = _rms(x, g_ref[...])
        o_ref[...] = x
        y_ref[...] = jnp.zeros_like(y_ref)
        h_b = hf.astype(BF16)
        h_lo = (hf - h_b.astype(F32)).astype(BF16)
        rt_hi, rt_lo = _split_bf16(rt_ref[...], 2)
        logits_t = _dot_nt(rt_hi, h_b) + _dot_nt(rt_hi, h_lo) + _dot_nt(rt_lo, h_b)
        dest0, dest1, gate0, gate1, start, padded = _moe_route(logits_t, n_experts, rb)
        for ee in range(n_experts):
            meta_ref[ee] = jnp.sum(start[ee:ee + 1, :]).astype(jnp.int32)
            meta_ref[n_experts + ee] = jnp.sum(padded[ee:ee + 1, :] * (1.0 / rb)).astype(jnp.int32)
        used = jnp.sum(jnp.floor((jnp.sum(padded, axis=0, keepdims=True) + (slab - 1)) * (1.0 / slab)))
        meta_ref[slabs_used] = used.astype(jnp.int32)

        nb = MOE_BUILD_ROWS

        def build(j, carry):
            r0 = pl.multiple_of(j * nb, nb)
            rows = (lax.broadcasted_iota(jnp.int32, (nb, tm), 0) + r0).astype(F32)
            hit0 = rows == dest0
            hit1 = rows == dest1
            onehot = jnp.where(hit0 | hit1, 1.0, 0.0).astype(BF16)
            xs_ref[pl.ds(r0, nb), :] = _dot(onehot, h_b).astype(BF16)
            gw_ref[pl.ds(r0, nb), :] = (jnp.where(hit0, gate0, 0.0)
                                        + jnp.where(hit1, gate1, 0.0)).astype(BF16)
            return carry

        lax.fori_loop(0, meta_ref[slabs_used] * (slab // nb), build, 0)

    seg_start = meta_ref[e]
    seg_blocks = meta_ref[n_experts + e]

    def expert_rows(r0, rows):
        xb = xs_ref[pl.ds(r0, rows), :]
        a = _silu(_dot(xb, wg_ref[...])) * _dot(xb, wu_ref[...])
        y = y_ref[pl.ds(r0, rows), :].astype(F32) + _dot(a.astype(BF16), w2_ref[...])
        y_ref[pl.ds(r0, rows), :] = y.astype(BF16)

    def expert_pair(j, carry):
        expert_rows(pl.multiple_of(seg_start + j * (2 * rb), rb), 2 * rb)
        return carry

    lax.fori_loop(0, seg_blocks // 2, expert_pair, 0)

    @pl.when(seg_blocks % 2 == 1)
    def _odd_block():
        expert_rows(pl.multiple_of(seg_start + (seg_blocks - 1) * rb, rb), rb)

    @pl.when((c == pl.num_programs(1) - 1) & (e == pl.num_programs(2) - 1))
    def _combine():
        for kb in range(n_rows // slab):
            @pl.when(kb < meta_ref[slabs_used])
            def _():
                rs = slice(kb * slab, (kb + 1) * slab)
                o_ref[...] += _dot_tn(gw_ref[rs, :], y_ref[rs, :])

        if final_norm:
            o_ref[...] = _rms(o_ref[...], fg_ref[...])


def moe_residual(x, g, router_t, w13, w2, final_g, *, layer, tm, tf, rb, final_norm):
    n, d = x.shape
    _, n_experts, dff, _ = w2.shape
    nc = dff // tf
    n_rows = -(-(2 * tm + n_experts * rb) // MOE_SLAB_ROWS) * MOE_SLAB_ROWS
    return pl.pallas_call(
        functools.partial(_moe_kernel, n_experts=n_experts, rb=rb, final_norm=final_norm),
        grid=(n // tm, nc, n_experts),
        in_specs=[pl.BlockSpec((tm, d), lambda i, c, e: (i, 0), pipeline_mode=pl.Buffered(1)),
                  pl.BlockSpec((1, d), lambda i, c, e: (0, 0)),
                  pl.BlockSpec((n_experts, d), lambda i, c, e: (0, 0)),
                  pl.BlockSpec((None, None, d, tf), lambda i, c, e: (layer, e, 0, c)),
                  pl.BlockSpec((None, None, d, tf), lambda i, c, e: (layer, e, 0, nc + c)),
                  pl.BlockSpec((None, None, tf, d), lambda i, c, e: (layer, e, c, 0)),
                  pl.BlockSpec((1, d), lambda i, c, e: (0, 0))],
        out_specs=pl.BlockSpec((tm, d), lambda i, c, e: (i, 0)),
        out_shape=jax.ShapeDtypeStruct((n, d), F32),
        scratch_shapes=[pltpu.VMEM((n_rows, d), BF16), pltpu.VMEM((n_rows, tm), BF16),
                        pltpu.VMEM((n_rows, d), BF16), pltpu.SMEM((2 * n_experts + 1,), jnp.int32)],
        compiler_params=_cparams("parallel", "arbitrary", "arbitrary"),
        name="moe_residual",
    )(x, g, router_t, w13, w13, w2, final_g)


def _final_norm_kernel(x_ref, g_ref, o_ref):
    o_ref[...] = _rms(x_ref[...], g_ref[...])


def final_norm(x, g, *, tm):
    n, d = x.shape
    return pl.pallas_call(
        _final_norm_kernel,
        grid=(n // tm,),
        in_specs=[pl.BlockSpec((tm, d), lambda i: (i, 0)), pl.BlockSpec((1, d), lambda i: (0, 0))],
        out_specs=pl.BlockSpec((tm, d), lambda i: (i, 0)),
        out_shape=jax.ShapeDtypeStruct((n, d), F32),
        compiler_params=_cparams("parallel"),
        name="final_norm",
    )(x, g)


def _row(v):
    return v.reshape(1, -1).astype(F32)


def _pad_cols(a, cols):
    return jnp.pad(a, ((0, 0), (0, cols - a.shape[1])))


def _row_tile(n, pref):
    t = min(pref, n)
    while n % t:
        t //= 2
    return t


def ab_mixer(x, norm_g, w_in, lb_logits, hg_onorm_g, rw_mu, rw_w0, rw_w2, rw_a0, rw_a2, rw_g2,
             rw_k_k, rw_k_a, rw_r_k, rw_gn_g, rw_gn_b, *, batch, seq, layer_j):
    n, d = x.shape
    hg_cols = 4 * lb_logits.shape[1]
    rw_pad = w_in.shape[2] - hg_cols
    p_hg, p_rw = norm_matmul2(x, _row(norm_g), w_in, layer=layer_j, split=hg_cols,
                              tm=_row_tile(n, 512))
    y_a = hgrn2_mix(p_hg, lb_logits.astype(F32), _row(hg_onorm_g), batch=batch, seq=seq,
                    layer_j=layer_j, tt=min(HG_TIME_BLOCK, seq))
    y_b = rwkv7_mix(p_rw, _pad_cols(_row(rw_mu), rw_pad), _row(rw_w0), rw_w2.astype(BF16),
                    _row(rw_a0), rw_a2.astype(BF16), rw_g2.astype(BF16), _row(rw_k_k),
                    _row(rw_k_a), _row(rw_r_k), _row(rw_gn_g), _row(rw_gn_b),
                    batch=batch, seq=seq)
    return y_a, y_b


def _block_diag(t):
    nblk, gpb, r, c = t.shape
    eye = jnp.eye(gpb, dtype=t.dtype)
    return jnp.einsum('bgrc,gh->bgrhc', t, eye).reshape(nblk, gpb * r, gpb * c)


def s5_mixer(x, norm_g, w_in, a_re, a_im, log_dt, b_re, b_im, c_re, c_im, d_skip, *, batch, seq, layer_j):
    n, d = x.shape
    groups, p_state = a_re.shape
    gpb = S5_GROUPS_PER_BLOCK
    nblk = groups // gpb
    tm = _row_tile(seq, 512)
    tb = seq // tm
    u_tm = norm_matmul(x, _row(norm_g), w_in, layer=layer_j, tm=tm, tn=d, out_shape=(seq, batch * d),
                       out_map=lambda i, j: (i % tb, i // tb))
    ab_re, ab_im, bb_re, bb_im = s5_params(
        a_re.reshape(groups, 1, p_state).astype(F32), a_im.reshape(groups, 1, p_state).astype(F32),
        log_dt.reshape(groups, 1, 1).astype(F32),
        jnp.swapaxes(b_re, 1, 2).astype(F32), jnp.swapaxes(b_im, 1, 2).astype(F32))
    m = bb_re.shape[1]
    bb_blk = jnp.concatenate([_block_diag(bb_re.reshape(nblk, gpb, m, p_state)),
                              _block_diag(bb_im.reshape(nblk, gpb, m, p_state))], axis=-1)
    cre_blk = _block_diag(jnp.swapaxes(c_re, 1, 2).reshape(nblk, gpb, p_state, m))
    cim_blk = _block_diag(jnp.swapaxes(c_im, 1, 2).reshape(nblk, gpb, p_state, m))
    y_tm = s5_scan(u_tm.reshape(seq, batch, d), bb_blk.astype(BF16), cre_blk.astype(BF16),
                   cim_blk.astype(BF16), ab_re.reshape(nblk, 1, gpb * p_state),
                   ab_im.reshape(nblk, 1, gpb * p_state), _row(d_skip), lt=min(S5_TIME_BLOCK, seq))
    return y_tm.reshape(seq, batch * d)


def kernel(x, mem, mix_norm_g, xattn_norm_g, ffn_norm_g, mem_norm_g, final_norm_g, ab_w_in, ab_w_out, hg_lb_logits, hg_onorm_g, rw_mu, rw_w0, rw_w2, rw_a0, rw_a2, rw_g2, rw_k_k, rw_k_a, rw_r_k, rw_gn_g, rw_gn_b, c_w_in, s5_a_re, s5_a_im, s5_log_dt, s5_b_re, s5_b_im, s5_c_re, s5_c_im, s5_d, c_w_glu, xa_wq, xa_wkv, xa_wo, ffn_w13, ffn_w2, moe_router, moe_w13, moe_w2):
    batch, seq, d = x.shape
    n_mem = mem.shape[1]
    depth = mix_norm_g.shape[0]
    n = batch * seq
    x = x.reshape(n, d).astype(F32)
    mem2 = mem.reshape(batch * n_mem, d).astype(F32)
    tm = _row_tile(n, 512)
    tm_seq = _row_tile(seq, 1024)
    dff = ffn_w2.shape[1]
    tf = dff // 2 if (dff // 2) % LANES == 0 else dff

    in_cols = ab_w_in.shape[2]
    w_in_b = jnp.pad(ab_w_in, ((0, 0), (0, 0), (0, -in_cols % LANES))).astype(BF16)
    w_out_b, c_w_in_b, c_w_glu_b = ab_w_out.astype(BF16), c_w_in.astype(BF16), c_w_glu.astype(BF16)
    wq_b, wkv_b, wo_b = xa_wq.astype(BF16), xa_wkv.astype(BF16), xa_wo.astype(BF16)
    ffn_w13_b, ffn_w2_b = ffn_w13.astype(BF16), ffn_w2.astype(BF16)
    n_moe, n_experts = moe_w13.shape[:2]
    moe_f32 = (moe_w13.reshape(n_moe, n_experts * d, 2 * dff), moe_w2.reshape(n_moe, n_experts * dff, d))
    ffn_steps = (n // tm) * (dff // tf)
    piggyback = all(a.shape[1] % (ffn_steps * 2 * SUBLANES) == 0 for a in moe_f32)
    moe_bf16 = {}
    final_g = _row(final_norm_g)

    kv = norm_matmul_layers(mem2, _row(mem_norm_g), wkv_b, tm=_row_tile(batch * n_mem, 512), tn=d)

    for layer in range(depth):
        j = layer // 2
        xa_args = dict(layer_j=j, layer=layer, batch=batch, seq=seq, n_mem=n_mem, tm=tm_seq)
        if layer % 2 == 0:
            y_a, y_b = ab_mixer(x, mix_norm_g[layer], w_in_b, hg_lb_logits, hg_onorm_g[j],
                                rw_mu[j], rw_w0[j], rw_w2[j], rw_a0[j], rw_a2[j], rw_g2[j], rw_k_k[j],
                                rw_k_a[j], rw_r_k[j], rw_gn_g[j], rw_gn_b[j],
                                batch=batch, seq=seq, layer_j=j)
            x = ab_out_cross_attention(y_a, y_b, w_out_b, x, _row(xattn_norm_g[layer]), wq_b, kv, wo_b,
                                       **xa_args)
            nxt = (layer + 1) // 2
            do_cast = piggyback and layer + 1 < depth
            x, cast_out = ffn_residual(x, _row(ffn_norm_g[layer]), ffn_w13_b, ffn_w2_b, layer=j, tm=tm,
                                       tf=tf, cast=moe_f32 if do_cast else (), cast_layer=nxt)
            if do_cast:
                moe_bf16[nxt] = cast_out
        else:
            y = s5_mixer(x, mix_norm_g[layer], c_w_in_b, s5_a_re[j], s5_a_im[j], s5_log_dt[j],
                         s5_b_re[j], s5_b_im[j], s5_c_re[j], s5_c_im[j], s5_d[j],
                         batch=batch, seq=seq, layer_j=j)
            x = glu_out_cross_attention(y, c_w_glu_b, x, _row(xattn_norm_g[layer]), wq_b, kv, wo_b,
                                        **xa_args)
            w13_b, w2_b = moe_bf16.get(j) or tuple(a[j].astype(BF16) for a in moe_f32)
            x = moe_residual(x, _row(ffn_norm_g[layer]), moe_router[j].T.astype(F32),
                             w13_b.reshape(1, n_experts, d, 2 * dff), w2_b.reshape(1, n_experts, dff, d),
                             final_g, layer=0, tm=_row_tile(n, MOE_TOKEN_TILE),
                             tf=tf, rb=MOE_ROW_BLOCK, final_norm=layer == depth - 1)

    if depth % 2 == 1:
        x = final_norm(x, final_g, tm=tm)
    return x.reshape(batch, seq, d)
```

```python
import functools
import math

import jax
import jax.numpy as jnp
from jax import lax
from jax.experimental import pallas as pl
from jax.experimental.pallas import tpu as pltpu

F32 = jnp.float32
BF16 = jnp.bfloat16

NORM_EPS = 1e-6
LANES = 128
SUBLANES = 8
VMEM_LIMIT_BYTES = 60 * 1024 * 1024

HG_DK = 128
HG_CHUNK = 64
HG_EXP_CLIP = 60.0
HG_TIME_BLOCK = 2048

RW_N = 64
RW_CHUNK = 64
RW_GROUP_HEADS = 4
RW_STREAMS = 2
RW_CHUNKS_PER_STEP = 4
RW_DECAY_LORA = 32
RW_A_LORA = 32
RW_GATE_LORA = 96
RW_GN_EPS = 64e-5

S5_GROUP = 16
S5_STATE = 64
S5_GROUPS_PER_BLOCK = LANES // S5_GROUP
S5_BLOCK_STATE = S5_GROUPS_PER_BLOCK * S5_STATE
S5_TIME_BLOCK = 512
S5_SUB_STEPS = 32
S5_YIELD_STEPS = 16

XA_HEADS = 4
MOE_TOKEN_TILE = 1024
MOE_ROW_BLOCK = 128
MOE_RANK_BLOCK = 256
MOE_SLAB_ROWS = 512
MOE_BUILD_ROWS = 256


def _cparams(*sem):
    return pltpu.CompilerParams(dimension_semantics=sem, vmem_limit_bytes=VMEM_LIMIT_BYTES)


def _dot(a, b, precision=None):
    return jnp.dot(a, b, preferred_element_type=F32, precision=precision)


def _dot_nt(a, b, precision=None):
    return lax.dot_general(a, b, (((1,), (1,)), ((), ())), preferred_element_type=F32,
                           precision=precision)


def _dot_tn(a, b, precision=None):
    return lax.dot_general(a, b, (((0,), (0,)), ((), ())), preferred_element_type=F32,
                           precision=precision)


def _split_bf16(t, terms):
    parts = []
    for _ in range(terms):
        hi = t.astype(BF16)
        parts.append(hi)
        t = t - hi.astype(F32)
    return parts


def _rms(x, g, eps=NORM_EPS):
    return x * lax.rsqrt(jnp.mean(x * x, axis=-1, keepdims=True) + eps) * g


def _silu(x):
    return x * jax.nn.sigmoid(x)


def _tri_incl(n):
    r = lax.broadcasted_iota(jnp.int32, (n, n), 0)
    c = lax.broadcasted_iota(jnp.int32, (n, n), 1)
    return jnp.where(r >= c, 1.0, 0.0).astype(F32)


def _norm_matmul_kernel(x_ref, g_ref, w_ref, o_ref, h_ref):
    @pl.when(pl.program_id(1) == 0)
    def _():
        h_ref[...] = _rms(x_ref[...], g_ref[...]).astype(BF16)

    o_ref[...] = _dot(h_ref[...], w_ref[...]).astype(o_ref.dtype)


def norm_matmul(x, g, w, *, layer, tm, tn, out_dtype=F32, out_shape=None, out_map=None):
    m, k = x.shape
    n = w.shape[2]
    assert m % tm == 0 and n % tn == 0
    if out_shape is None:
        out_shape = (m, n)
    if out_map is None:
        out_map = lambda i, j: (i, j)
    return pl.pallas_call(
        _norm_matmul_kernel,
        grid=(m // tm, n // tn),
        in_specs=[pl.BlockSpec((tm, k), lambda i, j: (i, 0)),
                  pl.BlockSpec((1, k), lambda i, j: (0, 0)),
                  pl.BlockSpec((None, k, tn), lambda i, j: (layer, 0, j))],
        out_specs=pl.BlockSpec((tm, tn), out_map),
        out_shape=jax.ShapeDtypeStruct(out_shape, out_dtype),
        scratch_shapes=[pltpu.VMEM((tm, k), BF16)],
        compiler_params=_cparams("parallel", "arbitrary"),
        name="norm_matmul",
    )(x, g, w)


def _norm_matmul_layers_kernel(x_ref, g_ref, w_ref, o_ref, h_ref):
    @pl.when((pl.program_id(1) == 0) & (pl.program_id(2) == 0))
    def _():
        h_ref[...] = _rms(x_ref[...], g_ref[...]).astype(BF16)

    o_ref[...] = _dot(h_ref[...], w_ref[...]).astype(o_ref.dtype)


def norm_matmul_layers(x, g, w, *, tm, tn):
    m, k = x.shape
    layers, _, n = w.shape
    nj = n // tn
    return pl.pallas_call(
        _norm_matmul_layers_kernel,
        grid=(m // tm, layers, nj),
        in_specs=[pl.BlockSpec((tm, k), lambda i, l, j: (i, 0)),
                  pl.BlockSpec((1, k), lambda i, l, j: (0, 0)),
                  pl.BlockSpec((None, k, tn), lambda i, l, j: (l, 0, j))],
        out_specs=pl.BlockSpec((tm, tn), lambda i, l, j: (i, l * nj + j)),
        out_shape=jax.ShapeDtypeStruct((m, layers * n), BF16),
        scratch_shapes=[pltpu.VMEM((tm, k), BF16)],
        compiler_params=_cparams("parallel", "arbitrary", "arbitrary"),
        name="norm_matmul_layers",
    )(x, g, w)


def _norm_matmul2_kernel(x_ref, g_ref, w_ref, oa_ref, ob_ref):
    h = _rms(x_ref[...], g_ref[...]).astype(BF16)
    na = oa_ref.shape[1]
    oa_ref[...] = _dot(h, w_ref[:, :na])
    ob_ref[...] = _dot(h, w_ref[:, na:])


def norm_matmul2(x, g, w, *, layer, split, tm):
    m, k = x.shape
    n = w.shape[2]
    assert split % LANES == 0 and (n - split) % LANES == 0
    return pl.pallas_call(
        _norm_matmul2_kernel,
        grid=(m // tm,),
        in_specs=[pl.BlockSpec((tm, k), lambda i: (i, 0)),
                  pl.BlockSpec((1, k), lambda i: (0, 0)),
                  pl.BlockSpec((None, k, n), lambda i: (layer, 0, 0))],
        out_specs=(pl.BlockSpec((tm, split), lambda i: (i, 0)),
                   pl.BlockSpec((tm, n - split), lambda i: (i, 0))),
        out_shape=(jax.ShapeDtypeStruct((m, split), F32), jax.ShapeDtypeStruct((m, n - split), F32)),
        compiler_params=_cparams("parallel"),
        name="norm_matmul2",
    )(x, g, w)


def _hgrn2_kernel(q_ref, f_ref, i_ref, g_ref, lbl_ref, og_ref, o_ref, st_ref, *, layer_j, n_chunks):
    c = HG_CHUNK
    halves = [c >> (i + 1) for i in range(c.bit_length() - 1)]

    @pl.when(pl.program_id(2) == 0)
    def _():
        st_ref[...] = jnp.zeros_like(st_ref)

    logits = lbl_ref[...]
    e = jnp.exp(logits - jnp.max(logits, axis=0, keepdims=True))
    p = e / jnp.sum(e, axis=0, keepdims=True)
    lb = jnp.zeros((1, p.shape[1]), F32)
    for r in range(1, layer_j + 1):
        lb = lb + p[r:r + 1, :]

    og = og_ref[...]
    rr = lax.broadcasted_iota(jnp.int32, (c, c), 0)
    cc = lax.broadcasted_iota(jnp.int32, (c, c), 1)
    row = lax.broadcasted_iota(jnp.int32, (c, HG_DK), 0)

    sums = [rr >= cc]
    pair_masks = []
    for hb in halves:
        blk = 2 * hb
        sums.append(cc <= jnp.bitwise_and(rr, -blk) + (hb - 1))
        same = jnp.bitwise_and(rr, -blk) == jnp.bitwise_and(cc, -blk)
        pair_masks.append(jnp.where(same, jnp.bitwise_and(rr, hb) - jnp.bitwise_and(cc, hb), 0) > 0)
    prefix = jnp.concatenate([jnp.where(m, 1.0, 0.0) for m in sums], axis=0).astype(BF16)

    chunks = range(n_chunks)
    rows = [slice(ci * c, (ci + 1) * c) for ci in chunks]
    qs, ks, ivs, cums = [], [], [], []
    for ci in chunks:
        fr = f_ref[rows[ci], :]
        log_sig = jnp.minimum(fr, 0.0) - jnp.log1p(jnp.exp(-jnp.abs(fr)))
        log_f = log_sig + jnp.log1p(lb * jnp.exp(jnp.minimum(-fr, HG_EXP_CLIP)))
        cums.append(sum(_dot(prefix, part) for part in _split_bf16(log_f, 3)))
        qs.append(_silu(q_ref[rows[ci], :]))
        ks.append((1.0 - lb) * jax.nn.sigmoid(-fr))
        ivs.append(i_ref[rows[ci], :])
    ls = [cm[:c] for cm in cums]

    a_mats = [jnp.zeros((c, c), F32) for _ in chunks]
    for li, hb in enumerate(halves):
        upper = jnp.bitwise_and(row, hb) != 0
        for ci in chunks:
            d = ls[ci] - cums[ci][(li + 1) * c:(li + 2) * c]
            ex = jnp.exp(jnp.where(upper, d, -d))
            prod = _dot_nt((qs[ci] * ex).astype(BF16), (ks[ci] * ex).astype(BF16))
            a_mats[ci] = a_mats[ci] + jnp.where(pair_masks[li], prod, 0.0)

    os_, kvs, qes, ends = [], [], [], []
    for ci in chunks:
        q, k, l, iv = qs[ci], ks[ci], ls[ci], ivs[ci]
        iv_b = iv.astype(BF16)
        l_end = l[c - 1:c, :]
        os_.append(_dot(a_mats[ci].astype(BF16), iv_b) + jnp.sum(q * k, axis=-1, keepdims=True) * iv)
        qes.append((q * jnp.exp(l)).astype(BF16))
        kvs.append(_dot_tn(iv_b, (k * jnp.exp(l_end - l)).astype(BF16)))
        ends.append(jnp.exp(l_end))

    st = st_ref[...]
    for ci in range(n_chunks):
        rs = slice(ci * c, (ci + 1) * c)
        o = os_[ci] + _dot_nt(qes[ci], st.astype(BF16))
        st = st * ends[ci] + kvs[ci]
        o_ref[rs, :] = (_rms(o, og) * _silu(g_ref[rs, :])).astype(o_ref.dtype)
    st_ref[...] = st


def hgrn2_mix(p_hg, lb_logits, onorm_g, *, batch, seq, layer_j, tt=256):
    n, cols = p_hg.shape
    width = cols // 4
    heads = width // HG_DK
    assert seq % tt == 0 and tt % HG_CHUNK == 0
    tb = seq // tt
    n_ab = lb_logits.shape[0]

    def col_spec(part):
        return pl.BlockSpec((tt, HG_DK), lambda b, h, t, part=part: (b * tb + t, part * heads + h))

    kern = functools.partial(_hgrn2_kernel, layer_j=layer_j, n_chunks=tt // HG_CHUNK)
    return pl.pallas_call(
        kern,
        grid=(batch, heads, tb),
        in_specs=[col_spec(0), col_spec(1), col_spec(2), col_spec(3),
                  pl.BlockSpec((n_ab, HG_DK), lambda b, h, t: (0, h)),
                  pl.BlockSpec((1, HG_DK), lambda b, h, t: (0, 0))],
        out_specs=pl.BlockSpec((tt, HG_DK), lambda b, h, t: (b * tb + t, h)),
        out_shape=jax.ShapeDtypeStruct((n, width), BF16),
        scratch_shapes=[pltpu.VMEM((HG_DK, HG_DK), F32)],
        compiler_params=_cparams("parallel", "parallel", "arbitrary"),
        name="hgrn2_mix",
    )(p_hg, p_hg, p_hg, p_hg, lb_logits, onorm_g)


def _round_robin(chains):
    chains = list(chains)
    while chains:
        alive = []
        for ch in chains:
            try:
                next(ch)
                alive.append(ch)
            except StopIteration:
                pass
        chains = alive
        yield


def _rwkv7_stream(p_ref, mu_ref, w0_ref, w2_ref, a0_ref, a2_ref, g2_ref, kk_ref, ka_ref,
                  rk_ref, gng_ref, gnb_ref, ones_ref, o_ref, prev_ref, st_ref, *, heads, chunks):
    c = RW_CHUNK
    n = RW_N
    assert c == n
    width = heads * n
    gh = RW_GROUP_HEADS
    gw = gh * n
    gr = gh * c
    ngroups = heads // gh
    shift = c.bit_length() - 1

    @pl.when(pl.program_id(1) == 0)
    def _():
        prev_ref[...] = jnp.zeros_like(prev_ref)
        st_ref[...] = jnp.zeros_like(st_ref)

    ones = ones_ref[...]

    def head_sum(t):
        return sum(_dot(part, ones) for part in _split_bf16(t, 2))

    tri = _tri_incl(c).astype(BF16)
    rb = lax.broadcasted_iota(jnp.int32, (gr, gw), 0)
    cb = lax.broadcasted_iota(jnp.int32, (gr, gw), 1)
    same = jnp.right_shift(rb, shift) == jnp.right_shift(cb, shift)
    t_in = jnp.bitwise_and(rb, c - 1)
    s_in = jnp.bitwise_and(cb, c - 1)
    strict = t_in > s_in
    incl = t_in >= s_in
    eye = jnp.where(rb == cb, 1.0, 0.0).astype(F32)

    def bd(t):
        return jnp.where(same, jnp.concatenate([t] * gh, axis=0), 0.0).astype(BF16)

    pre = [None] * chunks
    outs = [[None] * ngroups for _ in range(chunks)]

    def prepare(ci):
        p = p_ref[ci * c:(ci + 1) * c, :]
        before = prev_ref[0:1, :] if ci == 0 else p_ref[ci * c - 1:ci * c, :]
        row = lax.broadcasted_iota(jnp.int32, p.shape, 0)
        shifted = jnp.where(row == 0, before, pltpu.roll(p, 1, axis=0))
        p = p + (shifted - p) * mu_ref[...]

        r = p[:, 0:width]
        k = p[:, width:2 * width]
        v = p[:, 2 * width:3 * width]
        off = 3 * width
        wd = p[:, off:off + RW_DECAY_LORA]
        ad = p[:, off + RW_DECAY_LORA:off + RW_DECAY_LORA + RW_A_LORA]
        gd = p[:, off + RW_DECAY_LORA + RW_A_LORA:off + RW_DECAY_LORA + RW_A_LORA + RW_GATE_LORA]

        wz = w0_ref[...] + _dot(jnp.tanh(wd).astype(BF16), w2_ref[...])
        w = jnp.minimum(wz, 0.0) - jnp.log1p(jnp.exp(-jnp.abs(wz))) - 0.5
        lw = -jnp.exp(w)
        a = jax.nn.sigmoid(a0_ref[...] + _dot(ad.astype(BF16), a2_ref[...]))
        g = _dot(jax.nn.sigmoid(gd).astype(BF16), g2_ref[...])
        k2 = k * (1.0 + (a - 1.0) * ka_ref[...])
        yield

        kk = k * kk_ref[...]
        kk = kk * lax.rsqrt(jnp.maximum(head_sum(kk * kk), 1e-24))
        b = kk * a
        cum = sum(_dot(tri, part) for part in _split_bf16(lw, 3))
        cum_end = cum[c - 1:c, :]
        e_inv = jnp.exp(-cum)
        e_fin = jnp.exp(cum_end - cum)
        e_end = jnp.exp(cum_end)
        at = -kk * jnp.exp(cum - lw)
        rt = r * jnp.exp(cum)
        bt = b * e_inv
        kt = k2 * e_inv
        bf = b * e_fin
        kf = k2 * e_fin
        yield

        groups = []
        for gi in range(ngroups):
            gs = slice(gi * gw, (gi + 1) * gw)
            groups.append(dict(
                art=jnp.concatenate([bd(at[:, gs]), bd(rt[:, gs])], axis=0),
                bt=bd(bt[:, gs]), kt=bd(kt[:, gs]), v=bd(v[:, gs]),
                bk=jnp.concatenate([bd(bf[:, gs]), bd(kf[:, gs])], axis=0), e_end=e_end[:, gs]))
            yield
        pre[ci] = dict(groups=groups, g=g, v=v, rkk=r * k2 * rk_ref[...])

    def group_chain(ci, gi):
        q = pre[ci]["groups"][gi]
        art, v_bd = q["art"], q["v"]
        g_b = _dot_nt(art, q["bt"])
        g_k = _dot_nt(art, q["kt"])
        yield
        a_ab = jnp.where(strict, g_b[:gr], 0.0)
        a_ak = jnp.where(strict, g_k[:gr], 0.0)
        a_rb = jnp.where(incl, g_b[gr:], 0.0)
        a_rk = jnp.where(incl, g_k[gr:], 0.0)

        tinv = eye + a_ab
        xb = a_ab.astype(BF16)
        for _ in range(shift - 1):
            xb = _dot(xb, xb).astype(BF16)
            yield
            tinv = tinv + _dot(tinv.astype(BF16), xb)
            yield

        st = st_ref[gi]
        arm = _dot_nt(art, st.astype(BF16))
        akv = _dot(jnp.concatenate([a_ak, a_rk], axis=0).astype(BF16), v_bd)
        yield
        u_b = _dot(tinv.astype(BF16), (arm[:gr] + akv[:gr]).astype(BF16)).astype(BF16)
        yield
        o_bd = arm[gr:] + akv[gr:] + _dot(a_rb.astype(BF16), u_b)
        st_ref[gi] = st * q["e_end"] + _dot_tn(jnp.concatenate([u_b, v_bd], axis=0), q["bk"])
        o_g = o_bd[0:c]
        for hh in range(1, gh):
            o_g = o_g + o_bd[hh * c:(hh + 1) * c]
        outs[ci][gi] = o_g

    def finish(ci):
        q = pre[ci]
        o = jnp.concatenate(outs[ci], axis=-1)
        mean = head_sum(o) * (1.0 / n)
        dev = o - mean
        yield
        var = head_sum(dev * dev) * (1.0 / n)
        on = dev * lax.rsqrt(var + RW_GN_EPS) * gng_ref[...] + gnb_ref[...]
        yield
        bonus = head_sum(q["rkk"]) * q["v"]
        o_ref[ci * c:(ci + 1) * c, :] = ((on + bonus) * q["g"]).astype(o_ref.dtype)

    yield from prepare(0)
    for ci in range(chunks):
        stages = [_round_robin([group_chain(ci, gi) for gi in range(ngroups)])]
        if ci + 1 < chunks:
            stages.append(prepare(ci + 1))
        if ci >= 1:
            stages.append(finish(ci - 1))
        yield from _round_robin(stages)
    yield from finish(chunks - 1)
    prev_ref[0:1, :] = p_ref[chunks * c - 1:chunks * c, :]


def _rwkv7_kernel(p_ref, *refs, heads, streams, chunks):
    *param_refs, o_ref, prev_ref, st_ref = refs
    chains = [_rwkv7_stream(p_ref.at[bi], *param_refs, o_ref.at[bi], prev_ref.at[bi],
                            st_ref.at[bi], heads=heads, chunks=chunks) for bi in range(streams)]
    for _ in _round_robin(chains):
        pass


def rwkv7_mix(p_rw, mu, w0, w2, a0, a2, g2, k_k, k_a, r_k, gn_g, gn_b, *, batch, seq):
    n_tok, cols = p_rw.shape
    width = w0.shape[1]
    heads = width // RW_N
    assert seq % RW_CHUNK == 0 and heads % RW_GROUP_HEADS == 0
    cb = seq // RW_CHUNK
    head_id = jnp.arange(width, dtype=jnp.int32) // RW_N
    ones = (head_id[:, None] == head_id[None, :]).astype(BF16)
    full = lambda a: pl.BlockSpec(a.shape, lambda b, t: (0,) * a.ndim)
    params = (mu, w0, w2, a0, a2, g2, k_k, k_a, r_k, gn_g, gn_b, ones)
    gdim = RW_GROUP_HEADS * RW_N
    nb = RW_STREAMS if batch % RW_STREAMS == 0 else 1
    cps = RW_CHUNKS_PER_STEP if cb % RW_CHUNKS_PER_STEP == 0 else 1
    rows = cps * RW_CHUNK
    out = pl.pallas_call(
        functools.partial(_rwkv7_kernel, heads=heads, streams=nb, chunks=cps),
        grid=(batch // nb, cb // cps),
        in_specs=[pl.BlockSpec((nb, rows, cols), lambda b, t: (b, t, 0))]
                 + [full(a) for a in params],
        out_specs=pl.BlockSpec((nb, rows, width), lambda b, t: (b, t, 0)),
        out_shape=jax.ShapeDtypeStruct((batch, seq, width), BF16),
        scratch_shapes=[pltpu.VMEM((nb, SUBLANES, cols), F32),
                        pltpu.VMEM((nb, heads // RW_GROUP_HEADS, gdim, gdim), F32)],
        compiler_params=_cparams("parallel", "arbitrary"),
        name="rwkv7_mix",
    )(p_rw.reshape(batch, seq, cols), *params)
    return out.reshape(n_tok, width)


def _s5_param_kernel(are_ref, aim_ref, ldt_ref, bre_ref, bim_ref,
                     abre_ref, abim_ref, bbre_ref, bbim_ref):
    lam_re = jnp.minimum(are_ref[...], -1e-4)
    lam_im = aim_ref[...]
    dt = jnp.exp(ldt_ref[...])
    mag = jnp.exp(lam_re * dt)
    ab_re = mag * jnp.cos(lam_im * dt)
    ab_im = mag * jnp.sin(lam_im * dt)
    den = lam_re * lam_re + lam_im * lam_im
    coef_re = ((ab_re - 1.0) * lam_re + ab_im * lam_im) / den
    coef_im = (ab_im * lam_re - (ab_re - 1.0) * lam_im) / den
    abre_ref[...] = ab_re
    abim_ref[...] = ab_im
    b_re = bre_ref[...]
    b_im = bim_ref[...]
    bbre_ref[...] = coef_re * b_re - coef_im * b_im
    bbim_ref[...] = coef_re * b_im + coef_im * b_re


def s5_params(a_re, a_im, log_dt, b_re_t, b_im_t):
    g, _, p = a_re.shape
    m = b_re_t.shape[1]
    return pl.pallas_call(
        _s5_param_kernel,
        out_shape=(jax.ShapeDtypeStruct((g, 1, p), F32), jax.ShapeDtypeStruct((g, 1, p), F32),
                   jax.ShapeDtypeStruct((g, m, p), F32), jax.ShapeDtypeStruct((g, m, p), F32)),
        name="s5_params",
    )(a_re, a_im, log_dt, b_re_t, b_im_t)


def _gelu_tanh(y):
    return 0.5 * y * (1.0 + jnp.tanh(math.sqrt(2.0 / math.pi) * (y + 0.044715 * (y * y * y))))


def _s5_scan_kernel(u_ref, bb_ref, cre_ref, cim_ref, are_ref, aim_ref, d_ref, o_ref,
                    xs_ref, st_ref, *, lt, bsz):
    ns = S5_BLOCK_STATE

    @pl.when(pl.program_id(1) == 0)
    def _():
        st_ref[...] = jnp.zeros_like(st_ref)

    a_re = jnp.broadcast_to(are_ref[...], (bsz, ns))
    a_im = jnp.broadcast_to(aim_ref[...], (bsz, ns))
    sub = min(S5_SUB_STEPS, lt)
    nsub = lt // sub
    piece = max(sub // 2, 1)
    state = [st_ref[0], st_ref[1]]

    def project(k):
        for t0 in range(k * sub, (k + 1) * sub, piece):
            u2 = u_ref[t0:t0 + piece].reshape(piece * bsz, LANES)
            xs_ref[t0 * bsz:(t0 + piece) * bsz, :] = _dot(u2.astype(BF16), bb_ref[...])
            yield

    def recur(k):
        x_re, x_im = state
        for t in range(k * sub, (k + 1) * sub):
            rs = slice(t * bsz, (t + 1) * bsz)
            n_re = a_re * x_re - a_im * x_im + xs_ref[rs, 0:ns]
            n_im = a_re * x_im + a_im * x_re + xs_ref[rs, ns:2 * ns]
            xs_ref[rs, 0:ns] = n_re
            xs_ref[rs, ns:2 * ns] = n_im
            x_re, x_im = n_re, n_im
            if (t + 1) % S5_YIELD_STEPS == 0:
                yield
        state[0], state[1] = x_re, x_im

    def readout(k):
        for t0 in range(k * sub, (k + 1) * sub, piece):
            rs = slice(t0 * bsz, (t0 + piece) * bsz)
            u2 = u_ref[t0:t0 + piece].reshape(piece * bsz, LANES)
            y = (_dot(xs_ref[rs, 0:ns].astype(BF16), cre_ref[...])
                 - _dot(xs_ref[rs, ns:2 * ns].astype(BF16), cim_ref[...]))
            y = y + d_ref[...] * u2
            o_ref[t0:t0 + piece] = _gelu_tanh(y).reshape(piece, bsz, LANES)
            yield

    for k in range(nsub + 2):
        stages = []
        if k < nsub:
            stages.append(project(k))
        if 0 <= k - 1 < nsub:
            stages.append(recur(k - 1))
        if 0 <= k - 2 < nsub:
            stages.append(readout(k - 2))
        for _ in _round_robin(stages):
            pass

    st_ref[0] = state[0]
    st_ref[1] = state[1]


def s5_scan(u_tm, bb_blk, cre_blk, cim_blk, ab_re, ab_im, d, *, lt=256):
    seq, bsz, dm = u_tm.shape
    nblk = dm // LANES
    ns = S5_BLOCK_STATE
    assert seq % lt == 0
    return pl.pallas_call(
        functools.partial(_s5_scan_kernel, lt=lt, bsz=bsz),
        grid=(nblk, seq // lt),
        in_specs=[pl.BlockSpec((lt, bsz, LANES), lambda c, t: (t, 0, c)),
                  pl.BlockSpec((None, LANES, 2 * ns), lambda c, t: (c, 0, 0)),
                  pl.BlockSpec((None, ns, LANES), lambda c, t: (c, 0, 0)),
                  pl.BlockSpec((None, ns, LANES), lambda c, t: (c, 0, 0)),
                  pl.BlockSpec((None, 1, ns), lambda c, t: (c, 0, 0)),
                  pl.BlockSpec((None, 1, ns), lambda c, t: (c, 0, 0)),
                  pl.BlockSpec((1, LANES), lambda c, t: (0, c))],
        out_specs=pl.BlockSpec((lt, bsz, LANES), lambda c, t: (t, 0, c)),
        out_shape=jax.ShapeDtypeStruct((seq, bsz, dm), F32),
        scratch_shapes=[pltpu.VMEM((lt * bsz, 2 * ns), F32),
                        pltpu.VMEM((2, bsz, ns), F32)],
        compiler_params=_cparams("parallel", "arbitrary"),
        name="s5_scan",
    )(u_tm, bb_blk, cre_blk, cim_blk, ab_re, ab_im, d)


def _xattn_core(x, g_ref, wq_ref, k_ref, v_ref, wo_ref, heads):
    d = x.shape[1]
    hd = d // heads
    h = _rms(x, g_ref[...]).astype(BF16)
    q = _dot(h, wq_ref[...]).astype(BF16)
    outs = []
    for i in range(heads):
        sl = slice(i * hd, (i + 1) * hd)
        s = _dot_nt(q[:, sl], k_ref[:, sl]) * (hd ** -0.5)
        s = s - jnp.max(s, axis=-1, keepdims=True)
        e = jnp.exp(s)
        p = e / jnp.sum(e, axis=-1, keepdims=True)
        outs.append(_dot(p.astype(BF16), v_ref[:, sl]))
    o = jnp.concatenate(outs, axis=-1).astype(BF16)
    return x + _dot(o, wo_ref[...])


def _ab_out_xattn_kernel(ya_ref, yb_ref, wa_ref, wb_ref, x_ref, g_ref, wq_ref, k_ref, v_ref, wo_ref,
                         o_ref, *, heads):
    x = x_ref[...] + _dot(ya_ref[...], wa_ref[...]) + _dot(yb_ref[...], wb_ref[...])
    o_ref[...] = _xattn_core(x, g_ref, wq_ref, k_ref, v_ref, wo_ref, heads)


def _glu_out_xattn_kernel(y_ref, wglu_ref, x_ref, g_ref, wq_ref, k_ref, v_ref, wo_ref, o_ref, *, heads):
    d = x_ref.shape[1]
    z = _dot(y_ref[...].astype(BF16), wglu_ref[...])
    x = x_ref[...] + z[:, :d] * jax.nn.sigmoid(z[:, d:])
    o_ref[...] = _xattn_core(x, g_ref, wq_ref, k_ref, v_ref, wo_ref, heads)


def _xattn_specs(tm, d, n_mem, layer, rows):
    return [pl.BlockSpec((tm, d), rows),
            pl.BlockSpec((1, d), lambda b, t: (0, 0)),
            pl.BlockSpec((None, d, d), lambda b, t: (layer, 0, 0)),
            pl.BlockSpec((n_mem, d), lambda b, t: (b, 2 * layer)),
            pl.BlockSpec((n_mem, d), lambda b, t: (b, 2 * layer + 1)),
            pl.BlockSpec((None, d, d), lambda b, t: (layer, 0, 0))]


def ab_out_cross_attention(ya, yb, w_out, x, g, wq, kv, wo, *, layer_j, layer, batch, seq, n_mem, tm):
    n, d = x.shape
    ka, kb = ya.shape[1], yb.shape[1]
    assert ka == kb
    tb = seq // tm
    rows = lambda b, t: (b * tb + t, 0)
    xa = _xattn_specs(tm, d, n_mem, layer, rows)
    return pl.pallas_call(
        functools.partial(_ab_out_xattn_kernel, heads=XA_HEADS),
        grid=(batch, tb),
        in_specs=[pl.BlockSpec((tm, ka), rows), pl.BlockSpec((tm, kb), rows),
                  pl.BlockSpec((None, ka, d), lambda b, t: (layer_j, 0, 0)),
                  pl.BlockSpec((None, kb, d), lambda b, t: (layer_j, 1, 0))] + xa,
        out_specs=pl.BlockSpec((tm, d), rows),
        out_shape=jax.ShapeDtypeStruct((n, d), F32),
        compiler_params=_cparams("parallel", "parallel"),
        name="ab_out_cross_attention",
    )(ya, yb, w_out, w_out, x, g, wq, kv, kv, wo)


def glu_out_cross_attention(y_tm2d, w_glu, x, g, wq, kv, wo, *, layer_j, layer, batch, seq, n_mem, tm):
    n, d = x.shape
    tb = seq // tm
    rows = lambda b, t: (b * tb + t, 0)
    xa = _xattn_specs(tm, d, n_mem, layer, rows)
    return pl.pallas_call(
        functools.partial(_glu_out_xattn_kernel, heads=XA_HEADS),
        grid=(batch, tb),
        in_specs=[pl.BlockSpec((tm, d), lambda b, t: (t, b)),
                  pl.BlockSpec((None, d, 2 * d), lambda b, t: (layer_j, 0, 0))] + xa,
        out_specs=pl.BlockSpec((tm, d), rows),
        out_shape=jax.ShapeDtypeStruct((n, d), F32),
        compiler_params=_cparams("parallel", "parallel"),
        name="glu_out_cross_attention",
    )(y_tm2d, w_glu, x, g, wq, kv, kv, wo)


def _ffn_kernel(x_ref, g_ref, wg_ref, wu_ref, w2_ref, *refs, n_cast):
    cast_in, (o_ref, *cast_out), (h_ref, acc_ref) = refs[:n_cast], refs[n_cast:2 * n_cast + 1], refs[-2:]
    c = pl.program_id(1)

    @pl.when(c == 0)
    def _():
        h_ref[...] = _rms(x_ref[...], g_ref[...]).astype(BF16)
        acc_ref[...] = jnp.zeros_like(acc_ref)

    h = h_ref[...]
    tf = wg_ref.shape[1]
    half = (tf // LANES + 1) // 2 * LANES
    part = jnp.zeros(acc_ref.shape, F32)
    for lo, hi in ((0, half), (half, tf)):
        a = _silu(_dot(h, wg_ref[:, lo:hi])) * _dot(h, wu_ref[:, lo:hi])
        part = part + _dot(a.astype(BF16), w2_ref[lo:hi, :])
    acc_ref[...] += part

    for src, dst in zip(cast_in, cast_out):
        dst[...] = src[...].astype(BF16)

    @pl.when(c == pl.num_programs(1) - 1)
    def _():
        o_ref[...] = x_ref[...] + acc_ref[...]


def ffn_residual(x, g, w13, w2, *, layer, tm, tf, cast=(), cast_layer=0):
    n, d = x.shape
    dff = w2.shape[1]
    nc = dff // tf
    steps = (n // tm) * nc
    cast_specs, cast_out_specs, cast_shapes = [], [], []
    for arr in cast:
        _, rows, cols = arr.shape
        assert rows % (steps * 2 * SUBLANES) == 0
        blk = rows // steps
        cast_specs.append(pl.BlockSpec((None, blk, cols), lambda i, c: (cast_layer, i * nc + c, 0)))
        cast_out_specs.append(pl.BlockSpec((blk, cols), lambda i, c: (i * nc + c, 0)))
        cast_shapes.append(jax.ShapeDtypeStruct((rows, cols), BF16))
    out = pl.pallas_call(
        functools.partial(_ffn_kernel, n_cast=len(cast)),
        grid=(n // tm, nc),
        in_specs=[pl.BlockSpec((tm, d), lambda i, c: (i, 0)),
                  pl.BlockSpec((1, d), lambda i, c: (0, 0)),
                  pl.BlockSpec((None, d, tf), lambda i, c: (layer, 0, c)),
                  pl.BlockSpec((None, d, tf), lambda i, c: (layer, 0, nc + c)),
                  pl.BlockSpec((None, tf, d), lambda i, c: (layer, c, 0))] + cast_specs,
        out_specs=[pl.BlockSpec((tm, d), lambda i, c: (i, 0))] + cast_out_specs,
        out_shape=[jax.ShapeDtypeStruct((n, d), F32)] + cast_shapes,
        scratch_shapes=[pltpu.VMEM((tm, d), BF16), pltpu.VMEM((tm, d), F32)],
        compiler_params=_cparams("parallel", "arbitrary"),
        name="ffn_residual",
    )(x, g, w13, w13, w2, *cast)
    return out[0], tuple(out[1:])


def _moe_route(logits_t, n_experts, rb):
    m = logits_t.shape[1]
    sub = lax.broadcasted_iota(jnp.int32, logits_t.shape, 0).astype(F32)
    neg = -jnp.inf
    m1 = jnp.max(logits_t, axis=0, keepdims=True)
    i1 = jnp.min(jnp.where(logits_t == m1, sub, float(n_experts)), axis=0, keepdims=True)
    first = sub == i1
    rest = jnp.where(first, neg, logits_t)
    m2 = jnp.max(rest, axis=0, keepdims=True)
    i2 = jnp.min(jnp.where(rest == m2, sub, float(n_experts)), axis=0, keepdims=True)
    second = sub == i2
    e2 = jnp.exp(m2 - m1)
    den = 1.0 + e2
    gate0 = 1.0 / den
    gate1 = e2 / den

    first_f = jnp.where(first, 1.0, 0.0)
    second_f = jnp.where(second, 1.0, 0.0)
    both = jnp.concatenate([first_f, second_f], axis=0)
    w = min(m, MOE_RANK_BLOCK)
    t_src = lax.broadcasted_iota(jnp.int32, (w, w), 0)
    t_dst = lax.broadcasted_iota(jnp.int32, (w, w), 1)
    before = jnp.where(t_src < t_dst, 1.0, 0.0).astype(BF16)
    cnt = jnp.zeros((2 * n_experts, 1), F32)
    ranks = []
    for kb in range(m // w):
        blk = both[:, kb * w:(kb + 1) * w]
        ranks.append(_dot(blk.astype(BF16), before) + cnt)
        cnt = cnt + jnp.sum(blk, axis=1, keepdims=True)
    ranks = jnp.concatenate(ranks, axis=1)
    cnt0 = cnt[:n_experts]
    cnt_e = cnt0 + cnt[n_experts:]
    padded = jnp.floor((cnt_e + (rb - 1)) * (1.0 / rb)) * rb
    sub_col = lax.broadcasted_iota(jnp.int32, (n_experts, 1), 0)
    start = jnp.zeros((n_experts, 1), F32)
    for ee in range(n_experts - 1):
        start = start + jnp.where(sub_col > ee, padded[ee:ee + 1, :], 0.0)
    dest0 = jnp.sum(first_f * (start + ranks[:n_experts]), axis=0, keepdims=True)
    dest1 = jnp.sum(second_f * (start + cnt0 + ranks[n_experts:]), axis=0, keepdims=True)
    return dest0, dest1, gate0, gate1, start, padded


def _moe_kernel(x_ref, g_ref, rt_ref, wg_ref, wu_ref, w2_ref, fg_ref, o_ref, xs_ref, gw_ref, y_ref,
                meta_ref, *, n_experts, rb, final_norm):
    c = pl.program_id(1)
    e = pl.program_id(2)
    n_rows = xs_ref.shape[0]
    tm = x_ref.shape[0]
    slab = MOE_SLAB_ROWS
    slabs_used = 2 * n_experts

    @pl.when((c == 0) & (e == 0))
    def _route():
        x = x_ref[...]
        hf = _rms(x, g_ref[...])
        o_ref[...] = x
        y_ref[...] = jnp.zeros_like(y_ref)
        h_b = hf.astype(BF16)
        h_lo = (hf - h_b.astype(F32)).astype(BF16)
        rt_hi, rt_lo = _split_bf16(rt_ref[...], 2)
        logits_t = _dot_nt(rt_hi, h_b) + _dot_nt(rt_hi, h_lo) + _dot_nt(rt_lo, h_b)
        dest0, dest1, gate0, gate1, start, padded = _moe_route(logits_t, n_experts, rb)
        for ee in range(n_experts):
            meta_ref[ee] = jnp.sum(start[ee:ee + 1, :]).astype(jnp.int32)
            meta_ref[n_experts + ee] = jnp.sum(padded[ee:ee + 1, :] * (1.0 / rb)).astype(jnp.int32)
        used = jnp.sum(jnp.floor((jnp.sum(padded, axis=0, keepdims=True) + (slab - 1)) * (1.0 / slab)))
        meta_ref[slabs_used] = used.astype(jnp.int32)

        nb = MOE_BUILD_ROWS

        def build(j, carry):
            r0 = pl.multiple_of(j * nb, nb)
            rows = (lax.broadcasted_iota(jnp.int32, (nb, tm), 0) + r0).astype(F32)
            hit0 = rows == dest0
            hit1 = rows == dest1
            onehot = jnp.where(hit0 | hit1, 1.0, 0.0).astype(BF16)
            xs_ref[pl.ds(r0, nb), :] = _dot(onehot, h_b).astype(BF16)
            gw_ref[pl.ds(r0, nb), :] = (jnp.where(hit0, gate0, 0.0)
                                        + jnp.where(hit1, gate1, 0.0)).astype(BF16)
            return carry

        lax.fori_loop(0, meta_ref[slabs_used] * (slab // nb), build, 0)

    seg_start = meta_ref[e]
    seg_blocks = meta_ref[n_experts + e]

    def expert_rows(r0, rows):
        xb = xs_ref[pl.ds(r0, rows), :]
        a = _silu(_dot(xb, wg_ref[...])) * _dot(xb, wu_ref[...])
        y = y_ref[pl.ds(r0, rows), :].astype(F32) + _dot(a.astype(BF16), w2_ref[...])
        y_ref[pl.ds(r0, rows), :] = y.astype(BF16)

    def expert_pair(j, carry):
        expert_rows(pl.multiple_of(seg_start + j * (2 * rb), rb), 2 * rb)
        return carry

    lax.fori_loop(0, seg_blocks // 2, expert_pair, 0)

    @pl.when(seg_blocks % 2 == 1)
    def _odd_block():
        expert_rows(pl.multiple_of(seg_start + (seg_blocks - 1) * rb, rb), rb)

    @pl.when((c == pl.num_programs(1) - 1) & (e == pl.num_programs(2) - 1))
    def _combine():
        for kb in range(n_rows // slab):
            @pl.when(kb < meta_ref[slabs_used])
            def _():
                rs = slice(kb * slab, (kb + 1) * slab)
                o_ref[...] += _dot_tn(gw_ref[rs, :], y_ref[rs, :])

        if final_norm:
            o_ref[...] = _rms(o_ref[...], fg_ref[...])


def moe_residual(x, g, router_t, w13, w2, final_g, *, layer, tm, tf, rb, final_norm):
    n, d = x.shape
    _, n_experts, dff, _ = w2.shape
    nc = dff // tf
    n_rows = -(-(2 * tm + n_experts * rb) // MOE_SLAB_ROWS) * MOE_SLAB_ROWS
    return pl.pallas_call(
        functools.partial(_moe_kernel, n_experts=n_experts, rb=rb, final_norm=final_norm),
        grid=(n // tm, nc, n_experts),
        in_specs=[pl.BlockSpec((tm, d), lambda i, c, e: (i, 0), pipeline_mode=pl.Buffered(1)),
                  pl.BlockSpec((1, d), lambda i, c, e: (0, 0)),
                  pl.BlockSpec((n_experts, d), lambda i, c, e: (0, 0)),
                  pl.BlockSpec((None, None, d, tf), lambda i, c, e: (layer, e, 0, c)),
                  pl.BlockSpec((None, None, d, tf), lambda i, c, e: (layer, e, 0, nc + c)),
                  pl.BlockSpec((None, None, tf, d), lambda i, c, e: (layer, e, c, 0)),
                  pl.BlockSpec((1, d), lambda i, c, e: (0, 0))],
        out_specs=pl.BlockSpec((tm, d), lambda i, c, e: (i, 0)),
        out_shape=jax.ShapeDtypeStruct((n, d), F32),
        scratch_shapes=[pltpu.VMEM((n_rows, d), BF16), pltpu.VMEM((n_rows, tm), BF16),
                        pltpu.VMEM((n_rows, d), BF16), pltpu.SMEM((2 * n_experts + 1,), jnp.int32)],
        compiler_params=_cparams("parallel", "arbitrary", "arbitrary"),
        name="moe_residual",
    )(x, g, router_t, w13, w13, w2, final_g)


def _final_norm_kernel(x_ref, g_ref, o_ref):
    o_ref[...] = _rms(x_ref[...], g_ref[...])


def final_norm(x, g, *, tm):
    n, d = x.shape
    return pl.pallas_call(
        _final_norm_kernel,
        grid=(n // tm,),
        in_specs=[pl.BlockSpec((tm, d), lambda i: (i, 0)), pl.BlockSpec((1, d), lambda i: (0, 0))],
        out_specs=pl.BlockSpec((tm, d), lambda i: (i, 0)),
        out_shape=jax.ShapeDtypeStruct((n, d), F32),
        compiler_params=_cparams("parallel"),
        name="final_norm",
    )(x, g)


def _row(v):
    return v.reshape(1, -1).astype(F32)


def _pad_cols(a, cols):
    return jnp.pad(a, ((0, 0), (0, cols - a.shape[1])))


def _row_tile(n, pref):
    t = min(pref, n)
    while n % t:
        t //= 2
    return t


def ab_mixer(x, norm_g, w_in, lb_logits, hg_onorm_g, rw_mu, rw_w0, rw_w2, rw_a0, rw_a2, rw_g2,
             rw_k_k, rw_k_a, rw_r_k, rw_gn_g, rw_gn_b, *, batch, seq, layer_j):
    n, d = x.shape
    hg_cols = 4 * lb_logits.shape[1]
    rw_pad = w_in.shape[2] - hg_cols
    p_hg, p_rw = norm_matmul2(x, _row(norm_g), w_in, layer=layer_j, split=hg_cols,
                              tm=_row_tile(n, 512))
    y_a = hgrn2_mix(p_hg, lb_logits.astype(F32), _row(hg_onorm_g), batch=batch, seq=seq,
                    layer_j=layer_j, tt=min(HG_TIME_BLOCK, seq))
    y_b = rwkv7_mix(p_rw, _pad_cols(_row(rw_mu), rw_pad), _row(rw_w0), rw_w2.astype(BF16),
                    _row(rw_a0), rw_a2.astype(BF16), rw_g2.astype(BF16), _row(rw_k_k),
                    _row(rw_k_a), _row(rw_r_k), _row(rw_gn_g), _row(rw_gn_b),
                    batch=batch, seq=seq)
    return y_a, y_b


def _block_diag(t):
    nblk, gpb, r, c = t.shape
    eye = jnp.eye(gpb, dtype=t.dtype)
    return jnp.einsum('bgrc,gh->bgrhc', t, eye).reshape(nblk, gpb * r, gpb * c)


def s5_mixer(x, norm_g, w_in, a_re, a_im, log_dt, b_re, b_im, c_re, c_im, d_skip, *, batch, seq, layer_j):
    n, d = x.shape
    groups, p_state = a_re.shape
    gpb = S5_GROUPS_PER_BLOCK
    nblk = groups // gpb
    tm = _row_tile(seq, 512)
    tb = seq // tm
    u_tm = norm_matmul(x, _row(norm_g), w_in, layer=layer_j, tm=tm, tn=d, out_shape=(seq, batch * d),
                       out_map=lambda i, j: (i % tb, i // tb))
    ab_re, ab_im, bb_re, bb_im = s5_params(
        a_re.reshape(groups, 1, p_state).astype(F32), a_im.reshape(groups, 1, p_state).astype(F32),
        log_dt.reshape(groups, 1, 1).astype(F32),
        jnp.swapaxes(b_re, 1, 2).astype(F32), jnp.swapaxes(b_im, 1, 2).astype(F32))
    m = bb_re.shape[1]
    bb_blk = jnp.concatenate([_block_diag(bb_re.reshape(nblk, gpb, m, p_state)),
                              _block_diag(bb_im.reshape(nblk, gpb, m, p_state))], axis=-1)
    cre_blk = _block_diag(jnp.swapaxes(c_re, 1, 2).reshape(nblk, gpb, p_state, m))
    cim_blk = _block_diag(jnp.swapaxes(c_im, 1, 2).reshape(nblk, gpb, p_state, m))
    y_tm = s5_scan(u_tm.reshape(seq, batch, d), bb_blk.astype(BF16), cre_blk.astype(BF16),
                   cim_blk.astype(BF16), ab_re.reshape(nblk, 1, gpb * p_state),
                   ab_im.reshape(nblk, 1, gpb * p_state), _row(d_skip), lt=min(S5_TIME_BLOCK, seq))
    return y_tm.reshape(seq, batch * d)


def kernel(x, mem, mix_norm_g, xattn_norm_g, ffn_norm_g, mem_norm_g, final_norm_g, ab_w_in, ab_w_out, hg_lb_logits, hg_onorm_g, rw_mu, rw_w0, rw_w2, rw_a0, rw_a2, rw_g2, rw_k_k, rw_k_a, rw_r_k, rw_gn_g, rw_gn_b, c_w_in, s5_a_re, s5_a_im, s5_log_dt, s5_b_re, s5_b_im, s5_c_re, s5_c_im, s5_d, c_w_glu, xa_wq, xa_wkv, xa_wo, ffn_w13, ffn_w2, moe_router, moe_w13, moe_w2):
    batch, seq, d = x.shape
    n_mem = mem.shape[1]
    depth = mix_norm_g.shape[0]
    n = batch * seq
    x = x.reshape(n, d).astype(F32)
    mem2 = mem.reshape(batch * n_mem, d).astype(F32)
    tm = _row_tile(n, 512)
    tm_seq = _row_tile(seq, 1024)
    dff = ffn_w2.shape[1]
    tf = dff // 2 if (dff // 2) % LANES == 0 else dff

    in_cols = ab_w_in.shape[2]
    w_in_b = jnp.pad(ab_w_in, ((0, 0), (0, 0), (0, -in_cols % LANES))).astype(BF16)
    w_out_b, c_w_in_b, c_w_glu_b = ab_w_out.astype(BF16), c_w_in.astype(BF16), c_w_glu.astype(BF16)
    wq_b, wkv_b, wo_b = xa_wq.astype(BF16), xa_wkv.astype(BF16), xa_wo.astype(BF16)
    ffn_w13_b, ffn_w2_b = ffn_w13.astype(BF16), ffn_w2.astype(BF16)
    n_moe, n_experts = moe_w13.shape[:2]
    moe_f32 = (moe_w13.reshape(n_moe, n_experts * d, 2 * dff), moe_w2.reshape(n_moe, n_experts * dff, d))
    ffn_steps = (n // tm) * (dff // tf)
    piggyback = all(a.shape[1] % (ffn_steps * 2 * SUBLANES) == 0 for a in moe_f32)
    moe_bf16 = {}
    final_g = _row(final_norm_g)

    kv = norm_matmul_layers(mem2, _row(mem_norm_g), wkv_b, tm=_row_tile(batch * n_mem, 512), tn=d)

    for layer in range(depth):
        j = layer // 2
        xa_args = dict(layer_j=j, layer=layer, batch=batch, seq=seq, n_mem=n_mem, tm=tm_seq)
        if layer % 2 == 0:
            y_a, y_b = ab_mixer(x, mix_norm_g[layer], w_in_b, hg_lb_logits, hg_onorm_g[j],
                                rw_mu[j], rw_w0[j], rw_w2[j], rw_a0[j], rw_a2[j], rw_g2[j], rw_k_k[j],
                                rw_k_a[j], rw_r_k[j], rw_gn_g[j], rw_gn_b[j],
                                batch=batch, seq=seq, layer_j=j)
            x = ab_out_cross_attention(y_a, y_b, w_out_b, x, _row(xattn_norm_g[layer]), wq_b, kv, wo_b,
                                       **xa_args)
            nxt = (layer + 1) // 2
            do_cast = piggyback and layer + 1 < depth
            x, cast_out = ffn_residual(x, _row(ffn_norm_g[layer]), ffn_w13_b, ffn_w2_b, layer=j, tm=tm,
                                       tf=tf, cast=moe_f32 if do_cast else (), cast_layer=nxt)
            if do_cast:
                moe_bf16[nxt] = cast_out
        else:
            y = s5_mixer(x, mix_norm_g[layer], c_w_in_b, s5_a_re[j], s5_a_im[j], s5_log_dt[j],
                         s5_b_re[j], s5_b_im[j], s5_c_re[j], s5_c_im[j], s5_d[j],
                         batch=batch, seq=seq, layer_j=j)
            x = glu_out_cross_attention(y, c_w_glu_b, x, _row(xattn_norm_g[layer]), wq_b, kv, wo_b,
                                        **xa_args)
            w13_b, w2_b = moe_bf16.get(j) or tuple(a[j].astype(BF16) for a in moe_f32)
            x = moe_residual(x, _row(ffn_norm_g[layer]), moe_router[j].T.astype(F32),
                             w13_b.reshape(1, n_experts, d, 2 * dff), w2_b.reshape(1, n_experts, dff, d),
                             final_g, layer=0, tm=_row_tile(n, MOE_TOKEN_TILE),
                             tf=tf, rb=MOE_ROW_BLOCK, final_norm=layer == depth - 1)

    if depth % 2 == 1:
        x = final_norm(x, final_g, tm=tm)
    return x.reshape(batch, seq, d)
```
